```python
import math
import jax, jax.numpy as jnp
from jax import lax
import numpy as np

D_MODEL = 1024
BATCH = 2
SEQ = 8192
DEPTH = 4

N_A_LAYERS = DEPTH // 2
N_B_LAYERS = DEPTH - N_A_LAYERS
ALPHA = (2.0 * DEPTH) ** 0.25
BETA = (8.0 * DEPTH) ** -0.25
LN_EPS = 1e-5
A_HEADS = 4
A_DQK = D_MODEL // 8
A_DV = D_MODEL // 4
A_CHUNK = 64
A_PROJ = 2 * A_HEADS * A_DQK + 2 * A_HEADS * A_DV + 2 * A_HEADS
A_NORM_EPS = 1e-6
B_WINDOWS = (128, 512, 2048)
B_DILATIONS = (1, 4, 16)
B_GROUPS = 3
B_HEADS = 8
B_HEAD_DIM = 64
B_WIDTH = B_HEADS * B_HEAD_DIM
ROPE_THETA = 10000.0
MOE_GROUPS = 4
MOE_EXPERTS_PER_GROUP = 8
MOE_EXPERTS = MOE_GROUPS * MOE_EXPERTS_PER_GROUP
MOE_TOPK = 2
MOE_HIDDEN = D_MODEL // 4
MOE_BLOCK = 128

kernel_name = "yoco_mlstm_dilated_hmoe_deepnorm_adaln"


def layer_norm(x, g, b):
    xf = x.astype(jnp.float32)
    mu = jnp.mean(xf, axis=-1, keepdims=True)
    var = jnp.mean(jnp.square(xf - mu), axis=-1, keepdims=True)
    return ((xf - mu) * lax.rsqrt(var + LN_EPS) * g.astype(jnp.float32) + b.astype(jnp.float32)).astype(x.dtype)


def rope(x, pos):
    dh = x.shape[-1]
    half = dh // 2
    inv = jnp.power(jnp.float32(ROPE_THETA), -jnp.arange(half, dtype=jnp.float32) * 2.0 / dh)
    ang = pos.astype(jnp.float32)[..., None] * inv
    ang = ang.reshape(ang.shape[:2] + (1,) * (x.ndim - 3) + (half,))
    cos, sin = jnp.cos(ang), jnp.sin(ang)
    xf = x.astype(jnp.float32)
    x1, x2 = xf[..., :half], xf[..., half:]
    return jnp.concatenate([x1 * cos - x2 * sin, x2 * cos + x1 * sin], axis=-1).astype(x.dtype)


def mlstm_chunkwise(q, k, v, i_pre, f_pre):
    bsz, nh, s, dqk = q.shape
    dv = v.shape[-1]
    nc = s // A_CHUNK

    def chunks(t):
        t = t.astype(jnp.float32).reshape((bsz, nh, nc, A_CHUNK) + t.shape[3:])
        return jnp.moveaxis(t, 2, 0)

    qc, kc, vc = chunks(q), chunks(k), chunks(v)
    ic = chunks(i_pre)
    lfc = chunks(jax.nn.log_sigmoid(f_pre.astype(jnp.float32)))
    causal = jnp.tril(jnp.ones((A_CHUNK, A_CHUNK), dtype=bool))

    def step(carry, inp):
        c_st, n_st, m_st = carry
        qq, kk, vv, ii, lf = inp
        b = jnp.cumsum(lf, axis=-1)
        dmat = jnp.where(causal, b[..., :, None] - b[..., None, :] + ii[..., None, :], -jnp.inf)
        inter = b + m_st[..., None]
        m_t = jnp.maximum(inter, jnp.max(dmat, axis=-1))
        dw = jnp.exp(dmat - m_t[..., None])
        iw = jnp.exp(inter - m_t)
        sc = jnp.einsum('bhtd,bhsd->bhts', qq, kk) * dw
        num = jnp.einsum('bhts,bhsv->bhtv', sc, vv) + iw[..., None] * jnp.einsum('bhtd,bhdv->bhtv', qq, c_st)
        den = jnp.sum(sc, axis=-1) + iw * jnp.einsum('bhtd,bhd->bht', qq, n_st)
        h = num / jnp.maximum(jnp.abs(den), jnp.exp(-m_t))[..., None]
        b_last = b[..., -1]
        ws_log = b_last[..., None] - b + ii
        m_new = jnp.maximum(b_last + m_st, jnp.max(ws_log, axis=-1))
        ws = jnp.exp(ws_log - m_new[..., None])
        decay = jnp.exp(b_last + m_st - m_new)
        c_new = decay[..., None, None] * c_st + jnp.einsum('bhs,bhsd,bhsv->bhdv', ws, kk, vv)
        n_new = decay[..., None] * n_st + jnp.einsum('bhs,bhsd->bhd', ws, kk)
        return (c_new, n_new, m_new), h

    init = (jnp.zeros((bsz, nh, dqk, dv), jnp.float32),
            jnp.zeros((bsz, nh, dqk), jnp.float32),
            jnp.zeros((bsz, nh), jnp.float32))
    _, h = lax.scan(step, init, (qc, kc, vc, ic, lfc))
    return jnp.moveaxis(h, 0, 2).reshape(bsz, nh, s, dv)


def mlstm_mixer(h, w_in, b_gate, norm_g, w_out):
    bsz, s, _ = h.shape
    proj = h @ w_in
    o1 = A_HEADS * A_DQK
    o2 = 2 * o1
    o3 = o2 + A_HEADS * A_DV
    o4 = o3 + A_HEADS * A_DV
    q = proj[..., :o1].reshape(bsz, s, A_HEADS, A_DQK)
    k = proj[..., o1:o2].reshape(bsz, s, A_HEADS, A_DQK) * (A_DQK ** -0.5)
    v = proj[..., o2:o3].reshape(bsz, s, A_HEADS, A_DV)
    o_gate = proj[..., o3:o4]
    gates = proj[..., o4:].astype(jnp.float32) + b_gate.astype(jnp.float32)
    i_pre = jnp.transpose(gates[..., :A_HEADS], (0, 2, 1))
    f_pre = jnp.transpose(gates[..., A_HEADS:], (0, 2, 1))
    ht = mlstm_chunkwise(jnp.transpose(q, (0, 2, 1, 3)), jnp.transpose(k, (0, 2, 1, 3)),
                         jnp.transpose(v, (0, 2, 1, 3)), i_pre, f_pre)
    mu = jnp.mean(ht, axis=-1, keepdims=True)
    var = jnp.mean(jnp.square(ht - mu), axis=-1, keepdims=True)
    hn = (ht - mu) * lax.rsqrt(var + A_NORM_EPS)
    hn = jnp.transpose(hn, (0, 2, 1, 3)).reshape(bsz, s, A_HEADS * A_DV) * norm_g.astype(jnp.float32)
    y = (hn * jax.nn.sigmoid(o_gate.astype(jnp.float32))).astype(h.dtype)
    return y @ w_out


def dilated_branch(q, k, v, window, dilation):
    bsz, s, nh, dh = q.shape
    wb = window // dilation
    span = wb * dilation
    p = -(-s // span) * span
    nb = p // span

    def to_blocks(t):
        t = jnp.pad(t, ((0, 0), (0, p - s), (0, 0), (0, 0)))
        t = t.reshape(bsz, p // dilation, dilation, nh, dh).transpose(0, 2, 1, 3, 4)
        return t.reshape(bsz, dilation, nb, wb, nh, dh)

    def with_prev(t):
        prev = jnp.concatenate([jnp.zeros_like(t[:, :, :1]), t[:, :, :-1]], axis=2)
        return jnp.concatenate([prev, t], axis=3)

    qb = to_blocks(q)
    kw = with_prev(to_blocks(k))
    vw = with_prev(to_blocks(v)).astype(jnp.float32)
    sc = jnp.einsum('brnqhe,brnkhe->brnhqk', qb, kw).astype(jnp.float32) * (dh ** -0.5)
    qi = jnp.arange(wb)[:, None]
    kj = jnp.arange(2 * wb)[None, :]
    band = (kj >= qi) & (kj <= qi + wb)
    valid = band[None] & ((jnp.arange(nb)[:, None, None] > 0) | (kj >= wb)[None])
    sc = jnp.where(valid[None, None, :, None], sc, -jnp.inf)
    m = jnp.max(sc, axis=-1, keepdims=True)
    pr = jnp.exp(sc - m)
    l = jnp.sum(pr, axis=-1)
    o = jnp.einsum('brnhqk,brnkhe->brnqhe', pr, vw) / jnp.swapaxes(l, -1, -2)[..., None]
    lse = jnp.swapaxes(m[..., 0] + jnp.log(l), -1, -2)

    def from_blocks(t):
        t = t.reshape((bsz, dilation, p // dilation) + t.shape[4:]).swapaxes(1, 2)
        return t.reshape((bsz, p) + t.shape[3:])[:, :s]

    return from_blocks(o), from_blocks(lse)


def dilated_mixer(h, positions, k_sh, v_sh, w_q, w_o):
    bsz, s, _ = h.shape
    q = rope((h @ w_q).reshape(bsz, s, B_GROUPS, B_HEADS, B_HEAD_DIM), positions)
    outs, lses = [], []
    for g in range(B_GROUPS):
        o_g, l_g = dilated_branch(q[:, :, g], k_sh[:, :, g], v_sh[:, :, g], B_WINDOWS[g], B_DILATIONS[g])
        outs.append(o_g)
        lses.append(l_g)
    wts = jax.nn.softmax(jnp.stack(lses, axis=0), axis=0)
    o = jnp.sum(wts[..., None] * jnp.stack(outs, axis=0), axis=0)
    return o.reshape(bsz, s, B_WIDTH).astype(h.dtype) @ w_o


def hier_moe(h, w_r1, b_r1, w_r2, b_r2, w_gate, w_up, w_down):
    bsz, s, d = h.shape
    t = h.reshape(-1, d)
    n_tok = t.shape[0]
    p1 = jax.nn.softmax((t @ w_r1).astype(jnp.float32) + b_r1.astype(jnp.float32), axis=-1)
    pg, g_idx = lax.top_k(p1, 1)
    lg2 = ((t @ w_r2).astype(jnp.float32) + b_r2.astype(jnp.float32)).reshape(n_tok, MOE_GROUPS, MOE_EXPERTS_PER_GROUP)
    sel = jnp.take_along_axis(lg2, g_idx[:, :, None], axis=1)[:, 0]
    v2, j2 = lax.top_k(sel, MOE_TOPK)
    weight = pg * jax.nn.softmax(v2, axis=-1)
    expert = g_idx * MOE_EXPERTS_PER_GROUP + j2

    n_as = n_tok * MOE_TOPK
    flat_e = expert.reshape(-1)
    flat_w = weight.reshape(-1)
    order = jnp.argsort(flat_e)
    se = flat_e[order]
    tok = order // MOE_TOPK
    counts = jnp.zeros((MOE_EXPERTS,), jnp.int32).at[flat_e].add(1)
    starts = jnp.cumsum(counts) - counts
    pcounts = (counts + MOE_BLOCK - 1) // MOE_BLOCK * MOE_BLOCK
    pends = jnp.cumsum(pcounts)
    pstarts = pends - pcounts
    dest = pstarts[se] + (jnp.arange(n_as, dtype=jnp.int32) - starts[se])
    n_blocks = -(-n_as // MOE_BLOCK) + MOE_EXPERTS
    xbuf = jnp.zeros((n_blocks * MOE_BLOCK, d), t.dtype).at[dest].set(t[tok])
    block_e = jnp.clip(jnp.searchsorted(pends, jnp.arange(n_blocks, dtype=jnp.int32) * MOE_BLOCK, side='right'),
                       0, MOE_EXPERTS - 1)

    def run(args):
        xb, e = args
        return (jax.nn.silu(xb @ w_gate[e]) * (xb @ w_up[e])) @ w_down[e]

    ybuf = lax.map(run, (xbuf.reshape(n_blocks, MOE_BLOCK, d), block_e)).reshape(n_blocks * MOE_BLOCK, d)
    y = ybuf[dest] * flat_w[order][:, None].astype(t.dtype)
    out = jnp.zeros((n_tok, d), t.dtype).at[tok].add(y)
    return out.reshape(bsz, s, d)


def setup_inputs(seed: int = 0) -> dict:
    key = jax.random.key(seed)
    ks = jax.random.split(key, 24)
    f32 = jnp.float32
    d = D_MODEL
    nrm = lambda k, shp: jax.random.normal(k, shp, f32)
    x = nrm(ks[0], (BATCH, SEQ, d))
    c = nrm(ks[1], (BATCH, d))
    offs = jax.random.randint(ks[2], (BATCH, 1), 0, 1024, dtype=jnp.int32)
    positions = (offs + jnp.arange(SEQ, dtype=jnp.int32)[None, :]).astype(jnp.int32)
    ada_w = nrm(ks[3], (DEPTH, d, 6 * d)) * (0.1 * d ** -0.5)
    ada_b = nrm(ks[4], (DEPTH, 6 * d)) * 0.01
    ln_g = 1.0 + 0.02 * nrm(ks[5], (DEPTH, 2, d))
    ln_b = 0.02 * nrm(ks[6], (DEPTH, 2, d))
    a_w_in = nrm(ks[7], (N_A_LAYERS, d, A_PROJ)) * d ** -0.5
    i_bias = 0.1 * nrm(ks[8], (N_A_LAYERS, A_HEADS))
    f_bias = jnp.linspace(3.0, 6.0, A_HEADS, dtype=f32)[None, :] + 0.1 * nrm(ks[9], (N_A_LAYERS, A_HEADS))
    a_b_gate = jnp.concatenate([i_bias, f_bias], axis=-1)
    a_norm_g = 1.0 + 0.02 * nrm(ks[10], (N_A_LAYERS, A_HEADS * A_DV))
    a_w_out = nrm(ks[11], (N_A_LAYERS, A_HEADS * A_DV, d)) * ((A_HEADS * A_DV) ** -0.5 * BETA)
    b_w_kv = nrm(ks[12], (d, 2 * B_GROUPS * B_WIDTH)) * d ** -0.5
    b_w_q = nrm(ks[13], (N_B_LAYERS, d, B_GROUPS * B_WIDTH)) * d ** -0.5
    b_w_o = nrm(ks[14], (N_B_LAYERS, B_WIDTH, d)) * (B_WIDTH ** -0.5 * BETA)
    moe_w_r1 = nrm(ks[15], (DEPTH, d, MOE_GROUPS)) * d ** -0.5
    moe_b_r1 = 0.01 * nrm(ks[16], (DEPTH, MOE_GROUPS))
    moe_w_r2 = nrm(ks[17], (DEPTH, d, MOE_EXPERTS)) * d ** -0.5
    moe_b_r2 = 0.01 * nrm(ks[18], (DEPTH, MOE_EXPERTS))
    moe_w_gate = nrm(ks[19], (DEPTH, MOE_EXPERTS, d, MOE_HIDDEN)) * d ** -0.5
    moe_w_up = nrm(ks[20], (DEPTH, MOE_EXPERTS, d, MOE_HIDDEN)) * d ** -0.5
    moe_w_down = nrm(ks[21], (DEPTH, MOE_EXPERTS, MOE_HIDDEN, d)) * (MOE_HIDDEN ** -0.5 * BETA)
    return {"x": x, "c": c, "positions": positions, "ada_w": ada_w, "ada_b": ada_b,
            "ln_g": ln_g, "ln_b": ln_b, "a_w_in": a_w_in, "a_b_gate": a_b_gate, "a_norm_g": a_norm_g,
            "a_w_out": a_w_out, "b_w_kv": b_w_kv, "b_w_q": b_w_q, "b_w_o": b_w_o,
            "moe_w_r1": moe_w_r1, "moe_b_r1": moe_b_r1, "moe_w_r2": moe_w_r2, "moe_b_r2": moe_b_r2,
            "moe_w_gate": moe_w_gate, "moe_w_up": moe_w_up, "moe_w_down": moe_w_down}


def reference(x, c, positions, ada_w, ada_b, ln_g, ln_b, a_w_in, a_b_gate, a_norm_g, a_w_out,
              b_w_kv, b_w_q, b_w_o, moe_w_r1, moe_b_r1, moe_w_r2, moe_b_r2, moe_w_gate, moe_w_up, moe_w_down):
    bsz, s, d = x.shape
    cond = jax.nn.silu(c)
    k_sh = None
    v_sh = None
    for l in range(DEPTH):
        mod = cond @ ada_w[l] + ada_b[l]
        sh1, sc1, g1, sh2, sc2, g2 = [m[:, None, :] for m in jnp.split(mod, 6, axis=-1)]
        h = x * (1.0 + sc1) + sh1
        if l < N_A_LAYERS:
            y = mlstm_mixer(h, a_w_in[l], a_b_gate[l], a_norm_g[l], a_w_out[l])
        else:
            y = dilated_mixer(h, positions, k_sh, v_sh, b_w_q[l - N_A_LAYERS], b_w_o[l - N_A_LAYERS])
        x = layer_norm(ALPHA * x + (1.0 + g1) * y, ln_g[l, 0], ln_b[l, 0])
        h = x * (1.0 + sc2) + sh2
        y = hier_moe(h, moe_w_r1[l], moe_b_r1[l], moe_w_r2[l], moe_b_r2[l],
                     moe_w_gate[l], moe_w_up[l], moe_w_down[l])
        x = layer_norm(ALPHA * x + (1.0 + g2) * y, ln_g[l, 1], ln_b[l, 1])
        if l == N_A_LAYERS - 1:
            kv = (x @ b_w_kv).reshape(bsz, s, 2, B_GROUPS, B_HEADS, B_HEAD_DIM)
            k_sh = rope(kv[:, :, 0], positions)
            v_sh = kv[:, :, 1]
    return x
```

```python
import functools

import jax
import jax.numpy as jnp
from jax import lax
from jax.experimental import pallas as pl
from jax.experimental.pallas import tpu as pltpu

F32 = jnp.float32
BF16 = jnp.bfloat16
I32 = jnp.int32
HI = lax.Precision.HIGHEST

D_MODEL = 1024
DEPTH = 4
N_A = DEPTH // 2
ALPHA = (2.0 * DEPTH) ** 0.25
LN_EPS = 1e-5
A_HEADS = 4
A_DQK = D_MODEL // 8
A_DV = D_MODEL // 4
A_NORM_EPS = 1e-6
A_QK = A_HEADS * A_DQK
A_V = A_HEADS * A_DV
B_WINDOWS = (128, 512, 2048)
B_DILATIONS = (1, 4, 16)
B_GROUPS = 3
B_HEADS = 8
B_HEAD_DIM = 64
B_WIDTH = B_HEADS * B_HEAD_DIM
B_WB = 128
ROPE_THETA = 10000.0
MOE_GROUPS = 4
MOE_EPG = 8
MOE_EXPERTS = MOE_GROUPS * MOE_EPG
MOE_HIDDEN = D_MODEL // 4

LANES = 128
SUBLANES = 8
VMEM_LIMIT = 56 * 1024 * 1024

A_CHUNK = 256
MOE_BM = 256
PLAN_T = 1024
DISP_T = 512
ROW_T = 512
ATT_J = (8, 8, 4)
MOD_TN = 1536


def _cparams(sem):
    return pltpu.CompilerParams(dimension_semantics=sem, vmem_limit_bytes=VMEM_LIMIT)


def _dot(a, b):
    return jnp.dot(a, b, preferred_element_type=F32)


def _dot_nt(a, b):
    return lax.dot_general(a, b, (((1,), (1,)), ((), ())), preferred_element_type=F32)


def _dot_tn(a, b):
    return lax.dot_general(a, b, (((0,), (0,)), ((), ())), preferred_element_type=F32)


def _layer_norm(v, g, b):
    mu = jnp.mean(v, axis=-1, keepdims=True)
    vc = v - mu
    var = jnp.mean(vc * vc, axis=-1, keepdims=True)
    return vc * lax.rsqrt(var + LN_EPS) * g + b


def _sigmoid(v):
    return 1.0 / (1.0 + jnp.exp(-v))


def _route_tile(h2, wr_ref, br_ref):
    lg = jnp.dot(h2, wr_ref[...], preferred_element_type=F32, precision=HI) + br_ref[...]
    lane = lax.broadcasted_iota(I32, lg.shape, 1).astype(F32)
    neg = -jnp.inf
    big = 1000.0
    m1 = jnp.where(lane < MOE_GROUPS, lg, neg)
    mx = jnp.max(m1, axis=-1, keepdims=True)
    pg = 1.0 / jnp.sum(jnp.exp(m1 - mx), axis=-1, keepdims=True)
    gi = jnp.min(jnp.where(m1 == mx, lane, big), axis=-1, keepdims=True)
    lo = MOE_GROUPS + gi * MOE_EPG
    m2 = jnp.where((lane >= lo) & (lane < lo + MOE_EPG), lg, neg)
    v0 = jnp.max(m2, axis=-1, keepdims=True)
    j0 = jnp.min(jnp.where(m2 == v0, lane, big), axis=-1, keepdims=True)
    m3 = jnp.where(lane == j0, neg, m2)
    v1 = jnp.max(m3, axis=-1, keepdims=True)
    j1 = jnp.min(jnp.where(m3 == v1, lane, big), axis=-1, keepdims=True)
    t = jnp.exp(v1 - v0)
    wa = 1.0 / (1.0 + t)
    wb = t * wa
    out = jnp.where(lane == 0, j0 - MOE_GROUPS,
          jnp.where(lane == 1, j1 - MOE_GROUPS,
          jnp.where(lane == 2, pg * wa,
          jnp.where(lane == 3, pg * wb, 0.0))))
    return out


def _mod_kernel(c_ref, w_ref, b_ref, o_ref):
    c = c_ref[...]
    cond = c * _sigmoid(c)
    o_ref[0] = jnp.dot(cond, w_ref[0], preferred_element_type=F32, precision=HI) + b_ref[0]


def _adaln_mod(c, ada_w, ada_b):
    bsz, d = c.shape
    depth, _, n6 = ada_w.shape
    rows = 8
    c_pad = jnp.zeros((rows, d), F32).at[:bsz].set(c)
    out = pl.pallas_call(
        _mod_kernel,
        grid=(depth, n6 // MOD_TN),
        in_specs=[pl.BlockSpec((rows, d), lambda l, j: (0, 0)),
                  pl.BlockSpec((1, d, MOD_TN), lambda l, j: (l, 0, j)),
                  pl.BlockSpec((1, 1, MOD_TN), lambda l, j: (l, 0, j))],
        out_specs=pl.BlockSpec((1, rows, MOD_TN), lambda l, j: (l, 0, j)),
        out_shape=jax.ShapeDtypeStruct((depth, rows, n6), F32),
        compiler_params=_cparams(("parallel", "parallel")),
        name="adaln_mod",
    )(c_pad, ada_w, ada_b.reshape(depth, 1, n6))
    return out[:, :bsz].reshape(depth, bsz, 6, d)


def _mlstm_layer_kernel(x_ref, mod_ref, win_ref, bg_ref, ng_ref, wout_ref, lng_ref, lnb_ref,
                        wr_ref, br_ref, x1_ref, h2_ref, route_ref, c_ref, n_ref, m_ref):
    L = A_CHUNK

    @pl.when(pl.program_id(1) == 0)
    def _():
        c_ref[...] = jnp.zeros_like(c_ref)
        n_ref[...] = jnp.zeros_like(n_ref)
        m_ref[...] = jnp.zeros_like(m_ref)

    x = x_ref[...]
    md = mod_ref[...]
    sh1, sc1, g1, sh2, sc2, g2 = [md[j:j + 1, :] for j in range(6)]
    hb = (x * (1.0 + sc1) + sh1).astype(BF16)

    o_q, o_k, o_v, o_o, o_g = 0, A_QK, 2 * A_QK, 2 * A_QK + A_V, 2 * A_QK + 2 * A_V
    q_all = _dot(hb, win_ref[:, o_q:o_k])
    k_all = _dot(hb, win_ref[:, o_k:o_v]) * (A_DQK ** -0.5)
    v_all = _dot(hb, win_ref[:, o_v:o_o])
    og_all = _dot(hb, win_ref[:, o_o:o_g])
    gates = _dot(hb, win_ref[:, o_g:o_g + LANES]) + bg_ref[...]

    lf = jnp.minimum(gates, 0.0) - jnp.log(1.0 + jnp.exp(-jnp.abs(gates)))
    row = lax.broadcasted_iota(I32, (L, L), 0)
    col = lax.broadcasted_iota(I32, (L, L), 1)
    causal = row >= col
    tri = jnp.where(causal, 1.0, 0.0).astype(F32)
    b_col = jnp.dot(tri, lf, preferred_element_type=F32, precision=HI)
    g_t = gates.T
    b_t = b_col.T

    acc = jnp.zeros((L, D_MODEL), F32)
    for h in range(A_HEADS):
        qh = q_all[:, h * A_DQK:(h + 1) * A_DQK]
        kh = k_all[:, h * A_DQK:(h + 1) * A_DQK]
        vh = v_all[:, h * A_DV:(h + 1) * A_DV]
        qb, kb, vb = qh.astype(BF16), kh.astype(BF16), vh.astype(BF16)
        fl = A_HEADS + h
        b_c = b_col[:, fl:fl + 1]
        i_c = gates[:, h:h + 1]
        b_r = b_t[fl:fl + 1, :]
        i_r = g_t[h:h + 1, :]
        m_st = m_ref[h:h + 1, 0:1]
        c_st = c_ref[h]
        n_st = n_ref[h:h + 1, :]

        dmat = jnp.where(causal, b_c - b_r + i_r, -jnp.inf)
        inter = b_c + m_st
        m_t = jnp.maximum(inter, jnp.max(dmat, axis=-1, keepdims=True))
        dw = jnp.exp(dmat - m_t)
        iw = jnp.exp(inter - m_t)
        sc = _dot_nt(qb, kb) * dw
        num = _dot(sc.astype(BF16), vb) + iw * _dot(qb, c_st.astype(BF16))
        den = jnp.sum(sc, axis=-1, keepdims=True) + iw * jnp.sum(qh * n_st, axis=-1, keepdims=True)
        hh = num * (1.0 / jnp.maximum(jnp.abs(den), jnp.exp(-m_t)))

        b_last = b_col[L - 1:L, fl:fl + 1]
        ws_log = b_last - b_c + i_c
        m_new = jnp.maximum(b_last + m_st, jnp.max(ws_log, axis=0, keepdims=True))
        ws = jnp.exp(ws_log - m_new)
        decay = jnp.exp(b_last + m_st - m_new)
        kw = kh * ws
        c_ref[h] = decay * c_st + _dot_tn(kw.astype(BF16), vb)
        n_ref[h:h + 1, :] = decay * n_st + jnp.sum(kw, axis=0, keepdims=True)
        m_ref[h:h + 1, :] = jnp.broadcast_to(m_new, (1, LANES))

        mu = jnp.mean(hh, axis=-1, keepdims=True)
        hc = hh - mu
        var = jnp.mean(hc * hc, axis=-1, keepdims=True)
        hn = hc * lax.rsqrt(var + A_NORM_EPS)
        og = og_all[:, h * A_DV:(h + 1) * A_DV]
        yh = hn * ng_ref[:, h * A_DV:(h + 1) * A_DV] * _sigmoid(og)
        acc = acc + _dot(yh.astype(BF16), wout_ref[h * A_DV:(h + 1) * A_DV, :])

    x1 = _layer_norm(ALPHA * x + (1.0 + g1) * acc, lng_ref[...], lnb_ref[...])
    x1_ref[...] = x1
    h2 = x1 * (1.0 + sc2) + sh2
    h2_ref[...] = h2
    route_ref[...] = _route_tile(h2, wr_ref, br_ref)


def _mlstm_layer(x, mod_l, win, bg, ng, wout, lng, lnb, wr, br):
    bsz, s, d = x.shape
    L = A_CHUNK
    nw = win.shape[1]
    const = lambda b, i: (0, 0)
    tok = pl.BlockSpec((None, L, d), lambda b, i: (b, i, 0))
    return pl.pallas_call(
        _mlstm_layer_kernel,
        grid=(bsz, s // L),
        in_specs=[tok,
                  pl.BlockSpec((None, 6, d), lambda b, i: (b, 0, 0)),
                  pl.BlockSpec((d, nw), const),
                  pl.BlockSpec((1, LANES), const),
                  pl.BlockSpec((1, A_V), const),
                  pl.BlockSpec((A_V, d), const),
                  pl.BlockSpec((1, d), const),
                  pl.BlockSpec((1, d), const),
                  pl.BlockSpec((d, LANES), const),
                  pl.BlockSpec((1, LANES), const)],
        out_specs=[tok, tok, pl.BlockSpec((None, L, LANES), lambda b, i: (b, i, 0))],
        out_shape=[jax.ShapeDtypeStruct((bsz, s, d), F32),
                   jax.ShapeDtypeStruct((bsz, s, d), F32),
                   jax.ShapeDtypeStruct((bsz, s, LANES), F32)],
        scratch_shapes=[pltpu.VMEM((A_HEADS, A_DQK, A_DV), F32),
                        pltpu.VMEM((8, A_DQK), F32),
                        pltpu.VMEM((8, LANES), F32)],
        compiler_params=_cparams(("parallel", "arbitrary")),
        name="mlstm_layer",
    )(x, mod_l, win, bg, ng, wout, lng, lnb, wr, br)


def _plan_rank_kernel(route_ref, rank_ref, counts_ref, carry_ref):
    n = PLAN_T

    @pl.when(pl.program_id(0) == 0)
    def _():
        carry_ref[...] = jnp.zeros_like(carry_ref)

    r = route_ref[...]
    lane = lax.broadcasted_iota(I32, (n, LANES), 1).astype(F32)
    oh0 = lane == r[:, 0:1]
    oh1 = lane == r[:, 1:2]
    oh = jnp.where(oh0 | oh1, 1.0, 0.0)
    row = lax.broadcasted_iota(I32, (n, n), 0)
    col = lax.broadcasted_iota(I32, (n, n), 1)
    before = jnp.where(row > col, 1.0, 0.0).astype(BF16)
    excl = _dot(before, oh.astype(BF16)) + carry_ref[...]
    r0 = jnp.sum(jnp.where(oh0, excl, 0.0), axis=-1, keepdims=True)
    r1 = jnp.sum(jnp.where(oh1, excl, 0.0), axis=-1, keepdims=True)
    rank_ref[...] = jnp.where(lane == 0, r0, jnp.where(lane == 1, r1, 0.0))
    carry = carry_ref[...] + jnp.sum(oh, axis=0, keepdims=True)
    carry_ref[...] = carry
    counts_ref[...] = carry


def _plan_dest_kernel(route_ref, rank_ref, pstart_ref, dest_ref):
    r = route_ref[...]
    rk = rank_ref[...]
    ps = pstart_ref[...]
    lane = lax.broadcasted_iota(I32, r.shape, 1).astype(F32)
    d0 = jnp.sum(jnp.where(lane == r[:, 0:1], ps, 0.0), axis=-1, keepdims=True) + rk[:, 0:1]
    d1 = jnp.sum(jnp.where(lane == r[:, 1:2], ps, 0.0), axis=-1, keepdims=True) + rk[:, 1:2]
    dest_ref[...] = jnp.where(lane == 0, d0, jnp.where(lane == 1, d1, 0.0)).astype(I32)


def _row_copy(src_ref, src_row, dst_ref, dst_row, sem):
    return pltpu.make_async_copy(src_ref.at[pl.ds(src_row, 1), :], dst_ref.at[pl.ds(dst_row, 1), :], sem)


def _dispatch_kernel(padstart_ref, padlen_ref, nu_ref, dest_ref, h2_ref, xbuf_ref, zero_ref, sem, zsem):
    i = pl.program_id(0)
    base = i * DISP_T

    def issue(t, carry):
        _row_copy(h2_ref, base + t, xbuf_ref, dest_ref[0, 0, 2 * t], sem.at[0]).start()
        _row_copy(h2_ref, base + t, xbuf_ref, dest_ref[0, 0, 2 * t + 1], sem.at[0]).start()
        return carry

    lax.fori_loop(0, DISP_T, issue, 0)

    @pl.when(i == 0)
    def _():
        zero_ref[...] = jnp.zeros_like(zero_ref)

        def per_expert(e, carry):
            start = padstart_ref[e]
            ln = padlen_ref[e]
            head = (-start) & (SUBLANES - 1)
            for r in range(SUBLANES - 1):
                @pl.when(r < head)
                def _(r=r):
                    cp = _row_copy(zero_ref, 0, xbuf_ref, start + r, zsem.at[0])
                    cp.start()
                    cp.wait()
            off = start + head
            rem = ln - head
            bit = MOE_BM // 2
            while bit >= SUBLANES:
                @pl.when((rem & bit) != 0)
                def _(off=off, bit=bit):
                    cp = pltpu.make_async_copy(zero_ref.at[pl.ds(0, bit), :],
                                               xbuf_ref.at[pl.ds(pl.multiple_of(off, SUBLANES), bit), :],
                                               zsem.at[0])
                    cp.start()
                    cp.wait()
                off = off + (rem & bit)
                bit //= 2
            return carry

        lax.fori_loop(0, MOE_EXPERTS, per_expert, 0)

        half = MOE_BM // 2

        def per_spare_half(j, carry):
            cp = pltpu.make_async_copy(zero_ref, xbuf_ref.at[pl.ds(pl.multiple_of(j * half, half), half), :],
                                       zsem.at[0])
            cp.start()
            cp.wait()
            return carry

        lax.fori_loop(2 * nu_ref[0], 2 * (xbuf_ref.shape[0] // MOE_BM), per_spare_half, 0)

    def drain(t, carry):
        _row_copy(h2_ref, 0, xbuf_ref, 0, sem.at[0]).wait()
        return carry

    lax.fori_loop(0, 2 * DISP_T, drain, 0)


def _expert_kernel(be_ref, nu_ref, x_ref, wg_ref, wu_ref, wd_ref, y_ref, wgu_s, wd_s):
    i = pl.program_id(0)
    changed = (i == 0) | (be_ref[i] != be_ref[jnp.maximum(i - 1, 0)])

    @pl.when(changed)
    def _():
        wgu_s[:, 0:MOE_HIDDEN] = wg_ref[...].astype(BF16)
        wgu_s[:, MOE_HIDDEN:2 * MOE_HIDDEN] = wu_ref[...].astype(BF16)
        wd_s[...] = wd_ref[...].astype(BF16)

    @pl.when(i < nu_ref[0])
    def _():
        gu = _dot(x_ref[...].astype(BF16), wgu_s[...])
        g = gu[:, 0:MOE_HIDDEN]
        u = gu[:, MOE_HIDDEN:2 * MOE_HIDDEN]
        hm = (g * _sigmoid(g) * u).astype(BF16)
        y_ref[...] = _dot(hm, wd_s[...])

    @pl.when(i >= nu_ref[0])
    def _():
        y_ref[...] = jnp.zeros_like(y_ref)


def _combine_kernel(dest_ref, x1_ref, route_ref, mod_ref, lng_ref, lnb_ref, ybuf_ref, x2_ref, gat_ref, sem):
    def issue(t, carry):
        _row_copy(ybuf_ref, dest_ref[0, 0, 2 * t], gat_ref.at[0], t, sem.at[0]).start()
        _row_copy(ybuf_ref, dest_ref[0, 0, 2 * t + 1], gat_ref.at[1], t, sem.at[0]).start()
        return carry

    lax.fori_loop(0, DISP_T, issue, 0)

    def drain(t, carry):
        _row_copy(ybuf_ref, 0, gat_ref.at[0], 0, sem.at[0]).wait()
        return carry

    lax.fori_loop(0, 2 * DISP_T, drain, 0)

    r = route_ref[...]
    md = mod_ref[...]
    g2 = md[5:6, :]
    y = r[:, 2:3] * gat_ref[0] + r[:, 3:4] * gat_ref[1]
    x2_ref[...] = _layer_norm(ALPHA * x1_ref[...] + (1.0 + g2) * y, lng_ref[...], lnb_ref[...])


def _moe_and_norm(x1, h2, route, mod_l, lng, lnb, wg, wu, wd):
    bsz, s, d = x1.shape
    n_tok = bsz * s
    n_as = 2 * n_tok
    nb = n_as // MOE_BM + MOE_EXPERTS
    n_rows = nb * MOE_BM
    x1f = x1.reshape(n_tok, d)
    h2f = h2.reshape(n_tok, d)
    rt = route.reshape(n_tok, LANES)

    rank, counts = pl.pallas_call(
        _plan_rank_kernel,
        grid=(n_tok // PLAN_T,),
        in_specs=[pl.BlockSpec((PLAN_T, LANES), lambda i: (i, 0))],
        out_specs=[pl.BlockSpec((PLAN_T, LANES), lambda i: (i, 0)),
                   pl.BlockSpec((1, LANES), lambda i: (0, 0))],
        out_shape=[jax.ShapeDtypeStruct((n_tok, LANES), F32),
                   jax.ShapeDtypeStruct((1, LANES), F32)],
        scratch_shapes=[pltpu.VMEM((1, LANES), F32)],
        compiler_params=_cparams(("arbitrary",)),
        name="moe_plan_rank",
    )(rt)

    cnt = counts[0, :MOE_EXPERTS].astype(I32)
    pcnt = (cnt + MOE_BM - 1) // MOE_BM * MOE_BM
    pend = jnp.cumsum(pcnt)
    pstart = pend - pcnt
    n_used = (pend[-1] // MOE_BM).astype(I32)
    blk = jnp.arange(nb, dtype=I32)
    block_e = jnp.sum((pend[None, :] <= (blk * MOE_BM)[:, None]).astype(I32), axis=1)
    block_e = jnp.minimum(block_e, block_e[jnp.maximum(n_used - 1, 0)]).astype(I32)
    pstart_row = jnp.zeros((1, LANES), F32).at[0, :MOE_EXPERTS].set(pstart.astype(F32))
    padstart = (pstart + cnt).astype(I32)
    padlen = (pcnt - cnt).astype(I32)

    dest = pl.pallas_call(
        _plan_dest_kernel,
        grid=(n_tok // PLAN_T,),
        in_specs=[pl.BlockSpec((PLAN_T, LANES), lambda i: (i, 0)),
                  pl.BlockSpec((PLAN_T, LANES), lambda i: (i, 0)),
                  pl.BlockSpec((1, LANES), lambda i: (0, 0))],
        out_specs=pl.BlockSpec((PLAN_T, LANES), lambda i: (i, 0)),
        out_shape=jax.ShapeDtypeStruct((n_tok, LANES), I32),
        compiler_params=_cparams(("parallel",)),
        name="moe_plan_dest",
    )(rt, rank, pstart_row)
    n_steps = n_tok // DISP_T
    dest3 = dest[:, :2].reshape(n_steps, 1, 2 * DISP_T)
    dest_spec = pl.BlockSpec((1, 1, 2 * DISP_T), lambda i, *_: (i, 0, 0), memory_space=pltpu.SMEM)

    xbuf = pl.pallas_call(
        _dispatch_kernel,
        grid_spec=pltpu.PrefetchScalarGridSpec(
            num_scalar_prefetch=3,
            grid=(n_steps,),
            in_specs=[dest_spec, pl.BlockSpec(memory_space=pl.ANY)],
            out_specs=pl.BlockSpec(memory_space=pl.ANY),
            scratch_shapes=[pltpu.VMEM((MOE_BM // 2, d), F32),
                            pltpu.SemaphoreType.DMA((1,)),
                            pltpu.SemaphoreType.DMA((1,))]),
        out_shape=jax.ShapeDtypeStruct((n_rows, d), F32),
        compiler_params=_cparams(("arbitrary",)),
        name="moe_dispatch",
    )(padstart, padlen, n_used.reshape(1), dest3, h2f)

    last = lambda i, be, nu: jnp.minimum(i, nu[0] - 1)
    ybuf = pl.pallas_call(
        _expert_kernel,
        grid_spec=pltpu.PrefetchScalarGridSpec(
            num_scalar_prefetch=2,
            grid=(nb,),
            in_specs=[pl.BlockSpec((MOE_BM, d), lambda i, be, nu: (last(i, be, nu), 0)),
                      pl.BlockSpec((None, d, MOE_HIDDEN), lambda i, be, nu: (be[i], 0, 0)),
                      pl.BlockSpec((None, d, MOE_HIDDEN), lambda i, be, nu: (be[i], 0, 0)),
                      pl.BlockSpec((None, MOE_HIDDEN, d), lambda i, be, nu: (be[i], 0, 0))],
            out_specs=pl.BlockSpec((MOE_BM, d), lambda i, be, nu: (i, 0)),
            scratch_shapes=[pltpu.VMEM((d, 2 * MOE_HIDDEN), BF16),
                            pltpu.VMEM((MOE_HIDDEN, d), BF16)]),
        out_shape=jax.ShapeDtypeStruct((n_rows, d), F32),
        compiler_params=_cparams(("arbitrary",)),
        name="moe_experts",
    )(block_e, n_used.reshape(1), xbuf, wg, wu, wd)

    per_b = s // DISP_T
    x2 = pl.pallas_call(
        _combine_kernel,
        grid=(n_steps,),
        in_specs=[pl.BlockSpec((1, 1, 2 * DISP_T), lambda i: (i, 0, 0), memory_space=pltpu.SMEM),
                  pl.BlockSpec((DISP_T, d), lambda i: (i, 0)),
                  pl.BlockSpec((DISP_T, LANES), lambda i: (i, 0)),
                  pl.BlockSpec((None, 6, d), lambda i: (i // per_b, 0, 0)),
                  pl.BlockSpec((1, d), lambda i: (0, 0)),
                  pl.BlockSpec((1, d), lambda i: (0, 0)),
                  pl.BlockSpec(memory_space=pl.ANY)],
        out_specs=pl.BlockSpec((DISP_T, d), lambda i: (i, 0)),
        out_shape=jax.ShapeDtypeStruct((n_tok, d), F32),
        scratch_shapes=[pltpu.VMEM((2, DISP_T, d), F32),
                        pltpu.SemaphoreType.DMA((1,))],
        compiler_params=_cparams(("arbitrary",)),
        name="moe_combine_norm",
    )(dest3, x1f, rt, mod_l, lng, lnb, ybuf)
    return x2.reshape(bsz, s, d)


def _rope_table_kernel(pos_ref, inv_ref, sgn_ref, cos_ref, sin_ref):
    ang = pos_ref[...] * inv_ref[...]
    cos_ref[...] = jnp.cos(ang)
    sin_ref[...] = jnp.sin(ang) * sgn_ref[...]


def _rope_tables(positions):
    bsz, s = positions.shape
    half = B_HEAD_DIM // 2
    inv = jnp.power(jnp.float32(ROPE_THETA), -jnp.arange(half, dtype=F32) * 2.0 / B_HEAD_DIM)
    inv_row = jnp.tile(inv, LANES // half).reshape(1, LANES)
    sgn_row = jnp.tile(jnp.concatenate([-jnp.ones((half,), F32), jnp.ones((half,), F32)]),
                       LANES // B_HEAD_DIM).reshape(1, LANES)
    pos = jnp.broadcast_to(positions.astype(F32)[:, :, None], (bsz, s, LANES))
    tok = pl.BlockSpec((None, ROW_T, LANES), lambda b, i: (b, i, 0))
    row = pl.BlockSpec((1, LANES), lambda b, i: (0, 0))
    return pl.pallas_call(
        _rope_table_kernel,
        grid=(bsz, s // ROW_T),
        in_specs=[tok, row, row],
        out_specs=[tok, tok],
        out_shape=[jax.ShapeDtypeStruct((bsz, s, LANES), F32)] * 2,
        compiler_params=_cparams(("parallel", "parallel")),
        name="rope_tables",
    )(pos, inv_row, sgn_row)


def _rope(t, cosf, sinf, first_half):
    n = t.shape[1]
    half = B_HEAD_DIM // 2
    fwd = pltpu.roll(t, n - half, 1)
    bwd = pltpu.roll(t, half, 1)
    return t * cosf + jnp.where(first_half, fwd, bwd) * sinf


def _rope_operands(cos_ref, sin_ref, rows):
    reps = B_WIDTH // LANES
    cosf = jnp.concatenate([cos_ref[...]] * reps, axis=1)
    sinf = jnp.concatenate([sin_ref[...]] * reps, axis=1)
    lane = lax.broadcasted_iota(I32, (rows, B_WIDTH), 1)
    first_half = (lane & (B_HEAD_DIM - 1)) < (B_HEAD_DIM // 2)
    return cosf, sinf, first_half


def _kv_kernel(x_ref, w_ref, cos_ref, sin_ref, k0, k1, k2, v0, v1, v2):
    xb = x_ref[...].astype(BF16)
    cosf, sinf, first_half = _rope_operands(cos_ref, sin_ref, ROW_T)
    for g, (k_ref, v_ref) in enumerate(((k0, v0), (k1, v1), (k2, v2))):
        kg = _dot(xb, w_ref[:, g * B_WIDTH:(g + 1) * B_WIDTH])
        k_ref[...] = _rope(kg, cosf, sinf, first_half).astype(BF16)
        vo = (B_GROUPS + g) * B_WIDTH
        v_ref[...] = _dot(xb, w_ref[:, vo:vo + B_WIDTH]).astype(BF16)


def _kv_project(x, wkv, cos_t, sin_t):
    bsz, s, d = x.shape
    tok = lambda w: pl.BlockSpec((None, ROW_T, w), lambda b, i: (b, i, 0))
    return pl.pallas_call(
        _kv_kernel,
        grid=(bsz, s // ROW_T),
        in_specs=[tok(d), pl.BlockSpec(wkv.shape, lambda b, i: (0, 0)), tok(LANES), tok(LANES)],
        out_specs=[tok(B_WIDTH)] * 6,
        out_shape=[jax.ShapeDtypeStruct((bsz, s, B_WIDTH), BF16)] * 6,
        compiler_params=_cparams(("parallel", "parallel")),
        name="kv_project",
    )(x, wkv, cos_t, sin_t)


def _q_kernel(x_ref, mod_ref, w_ref, cos_ref, sin_ref, q0, q1, q2):
    md = mod_ref[...]
    hb = (x_ref[...] * (1.0 + md[1:2, :]) + md[0:1, :]).astype(BF16)
    cosf, sinf, first_half = _rope_operands(cos_ref, sin_ref, ROW_T)
    for g, q_ref in enumerate((q0, q1, q2)):
        qg = _dot(hb, w_ref[:, g * B_WIDTH:(g + 1) * B_WIDTH])
        q_ref[...] = (_rope(qg, cosf, sinf, first_half) * (B_HEAD_DIM ** -0.5)).astype(BF16)


def _q_project(x, mod_l, wq, cos_t, sin_t):
    bsz, s, d = x.shape
    tok = lambda w: pl.BlockSpec((None, ROW_T, w), lambda b, i: (b, i, 0))
    return pl.pallas_call(
        _q_kernel,
        grid=(bsz, s // ROW_T),
        in_specs=[tok(d), pl.BlockSpec((None, 6, d), lambda b, i: (b, 0, 0)),
                  pl.BlockSpec(wq.shape, lambda b, i: (0, 0)), tok(LANES), tok(LANES)],
        out_specs=[tok(B_WIDTH)] * 3,
        out_shape=[jax.ShapeDtypeStruct((bsz, s, B_WIDTH), BF16)] * 3,
        compiler_params=_cparams(("parallel", "parallel")),
        name="q_project",
    )(x, mod_l, wq, cos_t, sin_t)


def _attn_kernel(q_ref, kc_ref, kp_ref, vc_ref, vp_ref, o_ref, lse_ref, kw_ref, vw_ref, *, nsub):
    wb = B_WB
    first_step = pl.program_id(2) == 0
    kw_ref[0:wb, :] = kp_ref[...]
    kw_ref[wb:, :] = kc_ref[...]
    vw_ref[0:wb, :] = vp_ref[...]
    vw_ref[wb:, :] = vc_ref[...]
    qi = lax.broadcasted_iota(I32, (wb, 2 * wb), 0)
    kj = lax.broadcasted_iota(I32, (wb, 2 * wb), 1)
    band = (kj >= qi) & (kj <= qi + wb)

    def sub_block(j, carry):
        r0 = pl.multiple_of(j * wb, wb)
        valid = band & ((kj >= wb) | (j > 0) | jnp.logical_not(first_step))
        q = q_ref[pl.ds(r0, wb), :]
        kwin = kw_ref[pl.ds(r0, 2 * wb), :]
        vwin = vw_ref[pl.ds(r0, 2 * wb), :]
        outs, lses = [], []
        for h in range(B_HEADS):
            sl = slice(h * B_HEAD_DIM, (h + 1) * B_HEAD_DIM)
            sc = jnp.where(valid, _dot_nt(q[:, sl], kwin[:, sl]), -jnp.inf)
            m = jnp.max(sc, axis=-1, keepdims=True)
            p = jnp.exp(sc - m)
            l = jnp.sum(p, axis=-1, keepdims=True)
            outs.append(_dot(p.astype(BF16), vwin[:, sl]) * (1.0 / l))
            lses.append(jnp.broadcast_to(m + jnp.log(l), (wb, B_HEAD_DIM)))
        o_ref[pl.ds(r0, wb), :] = jnp.concatenate(outs, axis=1)
        lse_ref[pl.ds(r0, wb), :] = jnp.concatenate(lses, axis=1)
        return carry

    lax.fori_loop(0, nsub, sub_block, 0)


def _dilated_attention(q, k, v, dil, nsub):
    bsz, s, w = q.shape
    rows = s // dil
    view = lambda t: t.reshape(bsz, rows, dil * w)
    blk = B_WB * nsub
    cur = pl.BlockSpec((None, blk, w), lambda b, r, n: (b, n, r))
    prev = pl.BlockSpec((None, B_WB, w), lambda b, r, n: (b, jnp.maximum(n * nsub - 1, 0), r))
    o, lse = pl.pallas_call(
        functools.partial(_attn_kernel, nsub=nsub),
        grid=(bsz, dil, rows // blk),
        in_specs=[cur, cur, prev, cur, prev],
        out_specs=[cur, cur],
        out_shape=[jax.ShapeDtypeStruct((bsz, rows, dil * w), F32)] * 2,
        scratch_shapes=[pltpu.VMEM((blk + B_WB, w), BF16), pltpu.VMEM((blk + B_WB, w), BF16)],
        compiler_params=_cparams(("parallel", "parallel", "arbitrary")),
        name=f"dilated_attention_d{dil}",
    )(view(q), view(k), view(k), view(v), view(v))
    return o.reshape(bsz, s, w), lse.reshape(bsz, s, w)


def _attn_out_kernel(x_ref, mod_ref, o0, o1, o2, l0, l1, l2, wo_ref, lng_ref, lnb_ref, wr_ref, br_ref,
                     x1_ref, h2_ref, route_ref):
    md = mod_ref[...]
    g1, sh2, sc2 = md[2:3, :], md[3:4, :], md[4:5, :]
    la, lb, lc = l0[...], l1[...], l2[...]
    mx = jnp.maximum(jnp.maximum(la, lb), lc)
    ea, eb, ec = jnp.exp(la - mx), jnp.exp(lb - mx), jnp.exp(lc - mx)
    o = (ea * o0[...] + eb * o1[...] + ec * o2[...]) * (1.0 / (ea + eb + ec))
    y = _dot(o.astype(BF16), wo_ref[...])
    x1 = _layer_norm(ALPHA * x_ref[...] + (1.0 + g1) * y, lng_ref[...], lnb_ref[...])
    x1_ref[...] = x1
    h2 = x1 * (1.0 + sc2) + sh2
    h2_ref[...] = h2
    route_ref[...] = _route_tile(h2, wr_ref, br_ref)


def _attn_out_layer(x, mod_l, outs, lses, wo, lng, lnb, wr, br):
    bsz, s, d = x.shape
    tok = lambda w: pl.BlockSpec((None, ROW_T, w), lambda b, i: (b, i, 0))
    const = lambda b, i: (0, 0)
    return pl.pallas_call(
        _attn_out_kernel,
        grid=(bsz, s // ROW_T),
        in_specs=[tok(d), pl.BlockSpec((None, 6, d), lambda b, i: (b, 0, 0))]
                 + [tok(B_WIDTH)] * 6
                 + [pl.BlockSpec((B_WIDTH, d), const), pl.BlockSpec((1, d), const), pl.BlockSpec((1, d), const),
                    pl.BlockSpec((d, LANES), const), pl.BlockSpec((1, LANES), const)],
        out_specs=[tok(d), tok(d), tok(LANES)],
        out_shape=[jax.ShapeDtypeStruct((bsz, s, d), F32),
                   jax.ShapeDtypeStruct((bsz, s, d), F32),
                   jax.ShapeDtypeStruct((bsz, s, LANES), F32)],
        compiler_params=_cparams(("parallel", "parallel")),
        name="attn_out_layer",
    )(x, mod_l, *outs, *lses, wo, lng, lnb, wr, br)


def _router_weights(w_r1, b_r1, w_r2, b_r2):
    d = w_r1.shape[0]
    n = MOE_GROUPS + MOE_EXPERTS
    wr = jnp.zeros((d, LANES), F32).at[:, :MOE_GROUPS].set(w_r1).at[:, MOE_GROUPS:n].set(w_r2)
    br = jnp.zeros((1, LANES), F32).at[0, :MOE_GROUPS].set(b_r1).at[0, MOE_GROUPS:n].set(b_r2)
    return wr, br


def kernel(x, c, positions, ada_w, ada_b, ln_g, ln_b, a_w_in, a_b_gate, a_norm_g, a_w_out, b_w_kv, b_w_q, b_w_o,
           moe_w_r1, moe_b_r1, moe_w_r2, moe_b_r2, moe_w_gate, moe_w_up, moe_w_down):
    bsz, s, d = x.shape
    assert d == D_MODEL and s % (B_WB * B_DILATIONS[-1]) == 0 and s % ROW_T == 0 and s % A_CHUNK == 0
    assert (bsz * s) % PLAN_T == 0 and (2 * bsz * s) % MOE_BM == 0
    mod = _adaln_mod(c, ada_w, ada_b)
    cos_t, sin_t = _rope_tables(positions)
    kv = None
    for l in range(DEPTH):
        lng1, lnb1 = ln_g[l, 0].reshape(1, d), ln_b[l, 0].reshape(1, d)
        lng2, lnb2 = ln_g[l, 1].reshape(1, d), ln_b[l, 1].reshape(1, d)
        wr, br = _router_weights(moe_w_r1[l], moe_b_r1[l], moe_w_r2[l], moe_b_r2[l])
        if l < N_A:
            n_main = 2 * A_QK + 2 * A_V
            win = jnp.zeros((d, n_main + LANES), BF16)
            win = win.at[:, :n_main + 2 * A_HEADS].set(a_w_in[l].astype(BF16))
            bg = jnp.zeros((1, LANES), F32).at[0, :2 * A_HEADS].set(a_b_gate[l])
            x1, h2, route = _mlstm_layer(x, mod[l], win, bg, a_norm_g[l].reshape(1, A_V),
                                         a_w_out[l].astype(BF16), lng1, lnb1, wr, br)
        else:
            lb = l - N_A
            qs = _q_project(x, mod[l], b_w_q[lb].astype(BF16), cos_t, sin_t)
            outs, lses = [], []
            for g in range(B_GROUPS):
                nsub = min(ATT_J[g], s // (B_WB * B_DILATIONS[g]))
                o_g, l_g = _dilated_attention(qs[g], kv[g], kv[B_GROUPS + g], B_DILATIONS[g], nsub)
                outs.append(o_g)
                lses.append(l_g)
            x1, h2, route = _attn_out_layer(x, mod[l], outs, lses, b_w_o[lb].astype(BF16), lng1, lnb1, wr, br)
        x = _moe_and_norm(x1, h2, route, mod[l], lng2, lnb2, moe_w_gate[l], moe_w_up[l], moe_w_down[l])
        if l == N_A - 1:
            kv = _kv_project(x, b_w_kv.astype(BF16), cos_t, sin_t)
    return x
```

```python
import functools

import jax
import jax.numpy as jnp
from jax import lax
from jax.experimental import pallas as pl
from jax.experimental.pallas import tpu as pltpu

F32 = jnp.float32
BF16 = jnp.bfloat16
I32 = jnp.int32
HI = lax.Precision.HIGHEST

D_MODEL = 1024
DEPTH = 4
N_A = DEPTH // 2
ALPHA = (2.0 * DEPTH) ** 0.25
LN_EPS = 1e-5
A_HEADS = 4
A_DQK = D_MODEL // 8
A_DV = D_MODEL // 4
A_NORM_EPS = 1e-6
A_QK = A_HEADS * A_DQK
A_V = A_HEADS * A_DV
B_WINDOWS = (128, 512, 2048)
B_DILATIONS = (1, 4, 16)
B_GROUPS = 3
B_HEADS = 8
B_HEAD_DIM = 64
B_WIDTH = B_HEADS * B_HEAD_DIM
B_WB = 128
ROPE_THETA = 10000.0
MOE_GROUPS = 4
MOE_EPG = 8
MOE_EXPERTS = MOE_GROUPS * MOE_EPG
MOE_HIDDEN = D_MODEL // 4

LANES = 128
SUBLANES = 8
VMEM_LIMIT = 56 * 1024 * 1024

A_CHUNK = 256
MOE_BM = 256
PLAN_T = 1024
DISP_T = 512
ROW_T = 512
ATT_J = (8, 8, 4)
MOD_TN = 1536


def _cparams(sem):
    return pltpu.CompilerParams(dimension_semantics=sem, vmem_limit_bytes=VMEM_LIMIT)


def _dot(a, b):
    return jnp.dot(a, b, preferred_element_type=F32)


def _dot_nt(a, b):
    return lax.dot_general(a, b, (((1,), (1,)), ((), ())), preferred_element_type=F32)


def _dot_tn(a, b):
    return lax.dot_general(a, b, (((0,), (0,)), ((), ())), preferred_element_type=F32)


def _layer_norm(v, g, b):
    mu = jnp.mean(v, axis=-1, keepdims=True)
    vc = v - mu
    var = jnp.mean(vc * vc, axis=-1, keepdims=True)
    return vc * lax.rsqrt(var + LN_EPS) * g + b


def _sigmoid(v):
    return 1.0 / (1.0 + jnp.exp(-v))


def _route_tile(h2, wr_ref, br_ref):
    lg = jnp.dot(h2, wr_ref[...], preferred_element_type=F32, precision=HI) + br_ref[...]
    lane = lax.broadcasted_iota(I32, lg.shape, 1).astype(F32)
    neg = -jnp.inf
    big = 1000.0
    m1 = jnp.where(lane < MOE_GROUPS, lg, neg)
    mx = jnp.max(m1, axis=-1, keepdims=True)
    pg = 1.0 / jnp.sum(jnp.exp(m1 - mx), axis=-1, keepdims=True)
    gi = jnp.min(jnp.where(m1 == mx, lane, big), axis=-1, keepdims=True)
    lo = MOE_GROUPS + gi * MOE_EPG
    m2 = jnp.where((lane >= lo) & (lane < lo + MOE_EPG), lg, neg)
    v0 = jnp.max(m2, axis=-1, keepdims=True)
    j0 = jnp.min(jnp.where(m2 == v0, lane, big), axis=-1, keepdims=True)
    m3 = jnp.where(lane == j0, neg, m2)
    v1 = jnp.max(m3, axis=-1, keepdims=True)
    j1 = jnp.min(jnp.where(m3 == v1, lane, big), axis=-1, keepdims=True)
    t = jnp.exp(v1 - v0)
    wa = 1.0 / (1.0 + t)
    wb = t * wa
    out = jnp.where(lane == 0, j0 - MOE_GROUPS,
          jnp.where(lane == 1, j1 - MOE_GROUPS,
          jnp.where(lane == 2, pg * wa,
          jnp.where(lane == 3, pg * wb, 0.0))))
    return out


def _mod_kernel(c_ref, w_ref, b_ref, o_ref):
    c = c_ref[...]
    cond = c * _sigmoid(c)
    o_ref[0] = jnp.dot(cond, w_ref[0], preferred_element_type=F32, precision=HI) + b_ref[0]


def _adaln_mod(c, ada_w, ada_b):
    bsz, d = c.shape
    depth, _, n6 = ada_w.shape
    rows = 8
    c_pad = jnp.zeros((rows, d), F32).at[:bsz].set(c)
    out = pl.pallas_call(
        _mod_kernel,
        grid=(depth, n6 // MOD_TN),
        in_specs=[pl.BlockSpec((rows, d), lambda l, j: (0, 0)),
                  pl.BlockSpec((1, d, MOD_TN), lambda l, j: (l, 0, j)),
                  pl.BlockSpec((1, 1, MOD_TN), lambda l, j: (l, 0, j))],
        out_specs=pl.BlockSpec((1, rows, MOD_TN), lambda l, j: (l, 0, j)),
        out_shape=jax.ShapeDtypeStruct((depth, rows, n6), F32),
        compiler_params=_cparams(("parallel", "parallel")),
        name="adaln_mod",
    )(c_pad, ada_w, ada_b.reshape(depth, 1, n6))
    return out[:, :bsz].reshape(depth, bsz, 6, d)


def _mlstm_layer_kernel(x_ref, mod_ref, win_ref, bg_ref, ng_ref, wout_ref, lng_ref, lnb_ref,
                        wr_ref, br_ref, x1_ref, h2_ref, route_ref, c_ref, n_ref, m_ref):
    L = A_CHUNK

    @pl.when(pl.program_id(1) == 0)
    def _():
        c_ref[...] = jnp.zeros_like(c_ref)
        n_ref[...] = jnp.zeros_like(n_ref)
        m_ref[...] = jnp.zeros_like(m_ref)

    x = x_ref[...]
    md = mod_ref[...]
    sh1, sc1, g1, sh2, sc2, g2 = [md[j:j + 1, :] for j in range(6)]
    hb = (x * (1.0 + sc1) + sh1).astype(BF16)

    o_q, o_k, o_v, o_o, o_g = 0, A_QK, 2 * A_QK, 2 * A_QK + A_V, 2 * A_QK + 2 * A_V
    q_all = _dot(hb, win_ref[:, o_q:o_k])
    k_all = _dot(hb, win_ref[:, o_k:o_v]) * (A_DQK ** -0.5)
    v_all = _dot(hb, win_ref[:, o_v:o_o])
    og_all = _dot(hb, win_ref[:, o_o:o_g])
    gates = _dot(hb, win_ref[:, o_g:o_g + LANES]) + bg_ref[...]

    lf = jnp.minimum(gates, 0.0) - jnp.log(1.0 + jnp.exp(-jnp.abs(gates)))
    row = lax.broadcasted_iota(I32, (L, L), 0)
    col = lax.broadcasted_iota(I32, (L, L), 1)
    causal = row >= col
    tri = jnp.where(causal, 1.0, 0.0).astype(F32)
    b_col = jnp.dot(tri, lf, preferred_element_type=F32, precision=HI)
    g_t = gates.T
    b_t = b_col.T

    acc = jnp.zeros((L, D_MODEL), F32)
    for h in range(A_HEADS):
        qh = q_all[:, h * A_DQK:(h + 1) * A_DQK]
        kh = k_all[:, h * A_DQK:(h + 1) * A_DQK]
        vh = v_all[:, h * A_DV:(h + 1) * A_DV]
        qb, kb, vb = qh.astype(BF16), kh.astype(BF16), vh.astype(BF16)
        fl = A_HEADS + h
        b_c = b_col[:, fl:fl + 1]
        i_c = gates[:, h:h + 1]
        b_r = b_t[fl:fl + 1, :]
        i_r = g_t[h:h + 1, :]
        m_st = m_ref[h:h + 1, 0:1]
        c_st = c_ref[h]
        n_st = n_ref[h:h + 1, :]

        dmat = jnp.where(causal, b_c - b_r + i_r, -jnp.inf)
        inter = b_c + m_st
        m_t = jnp.maximum(inter, jnp.max(dmat, axis=-1, keepdims=True))
        dw = jnp.exp(dmat - m_t)
        iw = jnp.exp(inter - m_t)
        sc = _dot_nt(qb, kb) * dw
        num = _dot(sc.astype(BF16), vb) + iw * _dot(qb, c_st.astype(BF16))
        den = jnp.sum(sc, axis=-1, keepdims=True) + iw * jnp.sum(qh * n_st, axis=-1, keepdims=True)
        hh = num * (1.0 / jnp.maximum(jnp.abs(den), jnp.exp(-m_t)))

        b_last = b_col[L - 1:L, fl:fl + 1]
        ws_log = b_last - b_c + i_c
        m_new = jnp.maximum(b_last + m_st, jnp.max(ws_log, axis=0, keepdims=True))
        ws = jnp.exp(ws_log - m_new)
        decay = jnp.exp(b_last + m_st - m_new)
        kw = kh * ws
        c_ref[h] = decay * c_st + _dot_tn(kw.astype(BF16), vb)
        n_ref[h:h + 1, :] = decay * n_st + jnp.sum(kw, axis=0, keepdims=True)
        m_ref[h:h + 1, :] = jnp.broadcast_to(m_new, (1, LANES))

        mu = jnp.mean(hh, axis=-1, keepdims=True)
        hc = hh - mu
        var = jnp.mean(hc * hc, axis=-1, keepdims=True)
        hn = hc * lax.rsqrt(var + A_NORM_EPS)
        og = og_all[:, h * A_DV:(h + 1) * A_DV]
        yh = hn * ng_ref[:, h * A_DV:(h + 1) * A_DV] * _sigmoid(og)
        acc = acc + _dot(yh.astype(BF16), wout_ref[h * A_DV:(h + 1) * A_DV, :])

    x1 = _layer_norm(ALPHA * x + (1.0 + g1) * acc, lng_ref[...], lnb_ref[...])
    x1_ref[...] = x1
    h2 = x1 * (1.0 + sc2) + sh2
    h2_ref[...] = h2
    route_ref[...] = _route_tile(h2, wr_ref, br_ref)


def _mlstm_layer(x, mod_l, win, bg, ng, wout, lng, lnb, wr, br):
    bsz, s, d = x.shape
    L = A_CHUNK
    nw = win.shape[1]
    const = lambda b, i: (0, 0)
    tok = pl.BlockSpec((None, L, d), lambda b, i: (b, i, 0))
    return pl.pallas_call(
        _mlstm_layer_kernel,
        grid=(bsz, s // L),
        in_specs=[tok,
                  pl.BlockSpec((None, 6, d), lambda b, i: (b, 0, 0)),
                  pl.BlockSpec((d, nw), const),
                  pl.BlockSpec((1, LANES), const),
                  pl.BlockSpec((1, A_V), const),
                  pl.BlockSpec((A_V, d), const),
                  pl.BlockSpec((1, d), const),
                  pl.BlockSpec((1, d), const),
                  pl.BlockSpec((d, LANES), const),
                  pl.BlockSpec((1, LANES), const)],
        out_specs=[tok, tok, pl.BlockSpec((None, L, LANES), lambda b, i: (b, i, 0))],
        out_shape=[jax.ShapeDtypeStruct((bsz, s, d), F32),
                   jax.ShapeDtypeStruct((bsz, s, d), F32),
                   jax.ShapeDtypeStruct((bsz, s, LANES), F32)],
        scratch_shapes=[pltpu.VMEM((A_HEADS, A_DQK, A_DV), F32),
                        pltpu.VMEM((8, A_DQK), F32),
                        pltpu.VMEM((8, LANES), F32)],
        compiler_params=_cparams(("parallel", "arbitrary")),
        name="mlstm_layer",
    )(x, mod_l, win, bg, ng, wout, lng, lnb, wr, br)


def _plan_rank_kernel(route_ref, rank_ref, counts_ref, carry_ref):
    n = PLAN_T

    @pl.when(pl.program_id(0) == 0)
    def _():
        carry_ref[...] = jnp.zeros_like(carry_ref)

    r = route_ref[...]
    lane = lax.broadcasted_iota(I32, (n, LANES), 1).astype(F32)
    oh0 = lane == r[:, 0:1]
    oh1 = lane == r[:, 1:2]
    oh = jnp.where(oh0 | oh1, 1.0, 0.0)
    row = lax.broadcasted_iota(I32, (n, n), 0)
    col = lax.broadcasted_iota(I32, (n, n), 1)
    before = jnp.where(row > col, 1.0, 0.0).astype(BF16)
    excl = _dot(before, oh.astype(BF16)) + carry_ref[...]
    r0 = jnp.sum(jnp.where(oh0, excl, 0.0), axis=-1, keepdims=True)
    r1 = jnp.sum(jnp.where(oh1, excl, 0.0), axis=-1, keepdims=True)
    rank_ref[...] = jnp.where(lane == 0, r0, jnp.where(lane == 1, r1, 0.0))
    carry = carry_ref[...] + jnp.sum(oh, axis=0, keepdims=True)
    carry_ref[...] = carry
    counts_ref[...] = carry


def _plan_dest_kernel(route_ref, rank_ref, pstart_ref, dest_ref):
    r = route_ref[...]
    rk = rank_ref[...]
    ps = pstart_ref[...]
    lane = lax.broadcasted_iota(I32, r.shape, 1).astype(F32)
    d0 = jnp.sum(jnp.where(lane == r[:, 0:1], ps, 0.0), axis=-1, keepdims=True) + rk[:, 0:1]
    d1 = jnp.sum(jnp.where(lane == r[:, 1:2], ps, 0.0), axis=-1, keepdims=True) + rk[:, 1:2]
    dest_ref[...] = jnp.where(lane == 0, d0, jnp.where(lane == 1, d1, 0.0)).astype(I32)


def _row_copy(src_ref, src_row, dst_ref, dst_row, sem):
    return pltpu.make_async_copy(src_ref.at[pl.ds(src_row, 1), :], dst_ref.at[pl.ds(dst_row, 1), :], sem)


def _dispatch_kernel(padstart_ref, padlen_ref, nu_ref, dest_ref, h2_ref, xbuf_ref, zero_ref, sem, zsem):
    i = pl.program_id(0)

    def issue(t, carry):
        _row_copy(h2_ref, t, xbuf_ref, dest_ref[0, 0, 2 * t], sem.at[0]).start()
        _row_copy(h2_ref, t, xbuf_ref, dest_ref[0, 0, 2 * t + 1], sem.at[0]).start()
        return carry

    lax.fori_loop(0, DISP_T, issue, 0)

    @pl.when(i == 0)
    def _():
        zero_ref[...] = jnp.zeros_like(zero_ref)

        def per_expert(e, carry):
            start = padstart_ref[e]
            ln = padlen_ref[e]
            head = (-start) & (SUBLANES - 1)
            for r in range(SUBLANES - 1):
                @pl.when(r < head)
                def _(r=r):
                    cp = _row_copy(zero_ref, 0, xbuf_ref, start + r, zsem.at[0])
                    cp.start()
                    cp.wait()
            off = start + head
            rem = ln - head
            bit = MOE_BM // 2
            while bit >= SUBLANES:
                @pl.when((rem & bit) != 0)
                def _(off=off, bit=bit):
                    cp = pltpu.make_async_copy(zero_ref.at[pl.ds(0, bit), :],
                                               xbuf_ref.at[pl.ds(pl.multiple_of(off, SUBLANES), bit), :],
                                               zsem.at[0])
                    cp.start()
                    cp.wait()
                off = off + (rem & bit)
                bit //= 2
            return carry

        lax.fori_loop(0, MOE_EXPERTS, per_expert, 0)

        half = MOE_BM // 2

        def per_spare_half(j, carry):
            cp = pltpu.make_async_copy(zero_ref, xbuf_ref.at[pl.ds(pl.multiple_of(j * half, half), half), :],
                                       zsem.at[0])
            cp.start()
            cp.wait()
            return carry

        lax.fori_loop(2 * nu_ref[0], 2 * (xbuf_ref.shape[0] // MOE_BM), per_spare_half, 0)

    def drain(t, carry):
        _row_copy(h2_ref, 0, xbuf_ref, 0, sem.at[0]).wait()
        return carry

    lax.fori_loop(0, 2 * DISP_T, drain, 0)


def _expert_kernel(be_ref, nu_ref, x_ref, wg_ref, wu_ref, wd_ref, y_ref, wgu_s, wd_s):
    i = pl.program_id(0)
    changed = (i == 0) | (be_ref[i] != be_ref[jnp.maximum(i - 1, 0)])

    @pl.when(changed)
    def _():
        wgu_s[:, 0:MOE_HIDDEN] = wg_ref[...].astype(BF16)
        wgu_s[:, MOE_HIDDEN:2 * MOE_HIDDEN] = wu_ref[...].astype(BF16)
        wd_s[...] = wd_ref[...].astype(BF16)

    @pl.when(i < nu_ref[0])
    def _():
        gu = _dot(x_ref[...].astype(BF16), wgu_s[...])
        g = gu[:, 0:MOE_HIDDEN]
        u = gu[:, MOE_HIDDEN:2 * MOE_HIDDEN]
        hm = (g * _sigmoid(g) * u).astype(BF16)
        y_ref[...] = _dot(hm, wd_s[...])

    @pl.when(i >= nu_ref[0])
    def _():
        y_ref[...] = jnp.zeros_like(y_ref)


def _combine_kernel(dest_ref, x1_ref, route_ref, mod_ref, lng_ref, lnb_ref, ybuf_ref, x2_ref, gat_ref, sem):
    def issue(t, carry):
        _row_copy(ybuf_ref, dest_ref[0, 0, 2 * t], gat_ref.at[0], t, sem.at[0]).start()
        _row_copy(ybuf_ref, dest_ref[0, 0, 2 * t + 1], gat_ref.at[1], t, sem.at[0]).start()
        return carry

    lax.fori_loop(0, DISP_T, issue, 0)

    def drain(t, carry):
        _row_copy(ybuf_ref, 0, gat_ref.at[0], 0, sem.at[0]).wait()
        return carry

    lax.fori_loop(0, 2 * DISP_T, drain, 0)

    r = route_ref[...]
    md = mod_ref[...]
    g2 = md[5:6, :]
    y = r[:, 2:3] * gat_ref[0] + r[:, 3:4] * gat_ref[1]
    x2_ref[...] = _layer_norm(ALPHA * x1_ref[...] + (1.0 + g2) * y, lng_ref[...], lnb_ref[...])


def _moe_and_norm(x1, h2, route, mod_l, lng, lnb, wg, wu, wd, layer):
    bsz, s, d = x1.shape
    n_tok = bsz * s
    n_as = 2 * n_tok
    nb = n_as // MOE_BM + MOE_EXPERTS
    n_rows = nb * MOE_BM
    x1f = x1.reshape(n_tok, d)
    h2f = h2.reshape(n_tok, d)
    rt = route.reshape(n_tok, LANES)

    rank, counts = pl.pallas_call(
        _plan_rank_kernel,
        grid=(n_tok // PLAN_T,),
        in_specs=[pl.BlockSpec((PLAN_T, LANES), lambda i: (i, 0))],
        out_specs=[pl.BlockSpec((PLAN_T, LANES), lambda i: (i, 0)),
                   pl.BlockSpec((1, LANES), lambda i: (0, 0))],
        out_shape=[jax.ShapeDtypeStruct((n_tok, LANES), F32),
                   jax.ShapeDtypeStruct((1, LANES), F32)],
        scratch_shapes=[pltpu.VMEM((1, LANES), F32)],
        compiler_params=_cparams(("arbitrary",)),
        name="moe_plan_rank",
    )(rt)

    cnt = counts[0, :MOE_EXPERTS].astype(I32)
    pcnt = (cnt + MOE_BM - 1) // MOE_BM * MOE_BM
    pend = jnp.cumsum(pcnt)
    pstart = pend - pcnt
    n_used = (pend[-1] // MOE_BM).astype(I32)
    blk = jnp.arange(nb, dtype=I32)
    block_e = jnp.sum((pend[None, :] <= (blk * MOE_BM)[:, None]).astype(I32), axis=1)
    block_e = jnp.minimum(block_e, block_e[jnp.maximum(n_used - 1, 0)]).astype(I32)
    pstart_row = jnp.zeros((1, LANES), F32).at[0, :MOE_EXPERTS].set(pstart.astype(F32))
    padstart = (pstart + cnt).astype(I32)
    padlen = (pcnt - cnt).astype(I32)

    dest = pl.pallas_call(
        _plan_dest_kernel,
        grid=(n_tok // PLAN_T,),
        in_specs=[pl.BlockSpec((PLAN_T, LANES), lambda i: (i, 0)),
                  pl.BlockSpec((PLAN_T, LANES), lambda i: (i, 0)),
                  pl.BlockSpec((1, LANES), lambda i: (0, 0))],
        out_specs=pl.BlockSpec((PLAN_T, LANES), lambda i: (i, 0)),
        out_shape=jax.ShapeDtypeStruct((n_tok, LANES), I32),
        compiler_params=_cparams(("parallel",)),
        name="moe_plan_dest",
    )(rt, rank, pstart_row)
    n_steps = n_tok // DISP_T
    dest3 = dest[:, :2].reshape(n_steps, 1, 2 * DISP_T)
    dest_spec = pl.BlockSpec((1, 1, 2 * DISP_T), lambda i, *_: (i, 0, 0), memory_space=pltpu.SMEM)

    xbuf = pl.pallas_call(
        _dispatch_kernel,
        grid_spec=pltpu.PrefetchScalarGridSpec(
            num_scalar_prefetch=3,
            grid=(n_steps,),
            in_specs=[dest_spec, pl.BlockSpec((DISP_T, d), lambda i, *_: (i, 0))],
            out_specs=pl.BlockSpec(memory_space=pl.ANY),
            scratch_shapes=[pltpu.VMEM((MOE_BM // 2, d), F32),
                            pltpu.SemaphoreType.DMA((1,)),
                            pltpu.SemaphoreType.DMA((1,))]),
        out_shape=jax.ShapeDtypeStruct((n_rows, d), F32),
        compiler_params=_cparams(("arbitrary",)),
        name="moe_dispatch",
    )(padstart, padlen, n_used.reshape(1), dest3, h2f)

    last = lambda i, be, nu: jnp.minimum(i, nu[0] - 1)
    ybuf = pl.pallas_call(
        _expert_kernel,
        grid_spec=pltpu.PrefetchScalarGridSpec(
            num_scalar_prefetch=2,
            grid=(nb,),
            in_specs=[pl.BlockSpec((MOE_BM, d), lambda i, be, nu: (last(i, be, nu), 0)),
                      pl.BlockSpec((None, None, d, MOE_HIDDEN), lambda i, be, nu: (layer, be[i], 0, 0)),
                      pl.BlockSpec((None, None, d, MOE_HIDDEN), lambda i, be, nu: (layer, be[i], 0, 0)),
                      pl.BlockSpec((None, None, MOE_HIDDEN, d), lambda i, be, nu: (layer, be[i], 0, 0))],
            out_specs=pl.BlockSpec((MOE_BM, d), lambda i, be, nu: (i, 0)),
            scratch_shapes=[pltpu.VMEM((d, 2 * MOE_HIDDEN), BF16),
                            pltpu.VMEM((MOE_HIDDEN, d), BF16)]),
        out_shape=jax.ShapeDtypeStruct((n_rows, d), F32),
        compiler_params=_cparams(("arbitrary",)),
        name="moe_experts",
    )(block_e, n_used.reshape(1), xbuf, wg, wu, wd)

    per_b = s // DISP_T
    x2 = pl.pallas_call(
        _combine_kernel,
        grid=(n_steps,),
        in_specs=[pl.BlockSpec((1, 1, 2 * DISP_T), lambda i: (i, 0, 0), memory_space=pltpu.SMEM),
                  pl.BlockSpec((DISP_T, d), lambda i: (i, 0)),
                  pl.BlockSpec((DISP_T, LANES), lambda i: (i, 0)),
                  pl.BlockSpec((None, 6, d), lambda i: (i // per_b, 0, 0)),
                  pl.BlockSpec((1, d), lambda i: (0, 0)),
                  pl.BlockSpec((1, d), lambda i: (0, 0)),
                  pl.BlockSpec(memory_space=pl.ANY)],
        out_specs=pl.BlockSpec((DISP_T, d), lambda i: (i, 0)),
        out_shape=jax.ShapeDtypeStruct((n_tok, d), F32),
        scratch_shapes=[pltpu.VMEM((2, DISP_T, d), F32),
                        pltpu.SemaphoreType.DMA((1,))],
        compiler_params=_cparams(("arbitrary",)),
        name="moe_combine_norm",
    )(dest3, x1f, rt, mod_l, lng, lnb, ybuf)
    return x2.reshape(bsz, s, d)


def _rope_table_kernel(pos_ref, inv_ref, sgn_ref, cos_ref, sin_ref):
    ang = pos_ref[...] * inv_ref[...]
    cos_ref[...] = jnp.cos(ang)
    sin_ref[...] = jnp.sin(ang) * sgn_ref[...]


def _rope_tables(positions):
    bsz, s = positions.shape
    half = B_HEAD_DIM // 2
    inv = jnp.power(jnp.float32(ROPE_THETA), -jnp.arange(half, dtype=F32) * 2.0 / B_HEAD_DIM)
    inv_row = jnp.tile(inv, LANES // half).reshape(1, LANES)
    sgn_row = jnp.tile(jnp.concatenate([-jnp.ones((half,), F32), jnp.ones((half,), F32)]),
                       LANES // B_HEAD_DIM).reshape(1, LANES)
    pos = jnp.broadcast_to(positions.astype(F32)[:, :, None], (bsz, s, LANES))
    tok = pl.BlockSpec((None, ROW_T, LANES), lambda b, i: (b, i, 0))
    row = pl.BlockSpec((1, LANES), lambda b, i: (0, 0))
    return pl.pallas_call(
        _rope_table_kernel,
        grid=(bsz, s // ROW_T),
        in_specs=[tok, row, row],
        out_specs=[tok, tok],
        out_shape=[jax.ShapeDtypeStruct((bsz, s, LANES), F32)] * 2,
        compiler_params=_cparams(("parallel", "parallel")),
        name="rope_tables",
    )(pos, inv_row, sgn_row)


def _rope(t, cosf, sinf, first_half):
    n = t.shape[1]
    half = B_HEAD_DIM // 2
    fwd = pltpu.roll(t, n - half, 1)
    bwd = pltpu.roll(t, half, 1)
    return t * cosf + jnp.where(first_half, fwd, bwd) * sinf


def _rope_operands(cos_ref, sin_ref, rows):
    reps = B_WIDTH // LANES
    cosf = jnp.concatenate([cos_ref[...]] * reps, axis=1)
    sinf = jnp.concatenate([sin_ref[...]] * reps, axis=1)
    lane = lax.broadcasted_iota(I32, (rows, B_WIDTH), 1)
    first_half = (lane & (B_HEAD_DIM - 1)) < (B_HEAD_DIM // 2)
    return cosf, sinf, first_half


def _kv_kernel(x_ref, w_ref, cos_ref, sin_ref, k0, k1, k2, v0, v1, v2):
    xb = x_ref[...].astype(BF16)
    cosf, sinf, first_half = _rope_operands(cos_ref, sin_ref, ROW_T)
    for g, (k_ref, v_ref) in enumerate(((k0, v0), (k1, v1), (k2, v2))):
        kg = _dot(xb, w_ref[:, g * B_WIDTH:(g + 1) * B_WIDTH])
        k_ref[...] = _rope(kg, cosf, sinf, first_half).astype(BF16)
        vo = (B_GROUPS + g) * B_WIDTH
        v_ref[...] = _dot(xb, w_ref[:, vo:vo + B_WIDTH]).astype(BF16)


def _kv_project(x, wkv, cos_t, sin_t):
    bsz, s, d = x.shape
    tok = lambda w: pl.BlockSpec((None, ROW_T, w), lambda b, i: (b, i, 0))
    return pl.pallas_call(
        _kv_kernel,
        grid=(bsz, s // ROW_T),
        in_specs=[tok(d), pl.BlockSpec(wkv.shape, lambda b, i: (0, 0)), tok(LANES), tok(LANES)],
        out_specs=[tok(B_WIDTH)] * 6,
        out_shape=[jax.ShapeDtypeStruct((bsz, s, B_WIDTH), BF16)] * 6,
        compiler_params=_cparams(("parallel", "parallel")),
        name="kv_project",
    )(x, wkv, cos_t, sin_t)


def _q_kernel(x_ref, mod_ref, w_ref, cos_ref, sin_ref, q0, q1, q2):
    md = mod_ref[...]
    hb = (x_ref[...] * (1.0 + md[1:2, :]) + md[0:1, :]).astype(BF16)
    cosf, sinf, first_half = _rope_operands(cos_ref, sin_ref, ROW_T)
    for g, q_ref in enumerate((q0, q1, q2)):
        qg = _dot(hb, w_ref[:, g * B_WIDTH:(g + 1) * B_WIDTH])
        q_ref[...] = (_rope(qg, cosf, sinf, first_half) * (B_HEAD_DIM ** -0.5)).astype(BF16)


def _q_project(x, mod_l, wq, cos_t, sin_t):
    bsz, s, d = x.shape
    tok = lambda w: pl.BlockSpec((None, ROW_T, w), lambda b, i: (b, i, 0))
    return pl.pallas_call(
        _q_kernel,
        grid=(bsz, s // ROW_T),
        in_specs=[tok(d), pl.BlockSpec((None, 6, d), lambda b, i: (b, 0, 0)),
                  pl.BlockSpec(wq.shape, lambda b, i: (0, 0)), tok(LANES), tok(LANES)],
        out_specs=[tok(B_WIDTH)] * 3,
        out_shape=[jax.ShapeDtypeStruct((bsz, s, B_WIDTH), BF16)] * 3,
        compiler_params=_cparams(("parallel", "parallel")),
        name="q_project",
    )(x, mod_l, wq, cos_t, sin_t)


def _attn_kernel(q_ref, kc_ref, kp_ref, vc_ref, vp_ref, o_ref, lse_ref, kw_ref, vw_ref, *, nsub):
    wb = B_WB
    first_step = pl.program_id(2) == 0
    kw_ref[0:wb, :] = kp_ref[...]
    kw_ref[wb:, :] = kc_ref[...]
    vw_ref[0:wb, :] = vp_ref[...]
    vw_ref[wb:, :] = vc_ref[...]
    qi = lax.broadcasted_iota(I32, (wb, 2 * wb), 0)
    kj = lax.broadcasted_iota(I32, (wb, 2 * wb), 1)
    band = (kj >= qi) & (kj <= qi + wb)

    def sub_block(j, carry):
        r0 = pl.multiple_of(j * wb, wb)
        valid = band & ((kj >= wb) | (j > 0) | jnp.logical_not(first_step))
        q = q_ref[pl.ds(r0, wb), :]
        kwin = kw_ref[pl.ds(r0, 2 * wb), :]
        vwin = vw_ref[pl.ds(r0, 2 * wb), :]
        outs, lses = [], []
        for h in range(B_HEADS):
            sl = slice(h * B_HEAD_DIM, (h + 1) * B_HEAD_DIM)
            sc = jnp.where(valid, _dot_nt(q[:, sl], kwin[:, sl]), -jnp.inf)
            m = jnp.max(sc, axis=-1, keepdims=True)
            p = jnp.exp(sc - m)
            l = jnp.sum(p, axis=-1, keepdims=True)
            outs.append(_dot(p.astype(BF16), vwin[:, sl]) * (1.0 / l))
            lses.append(jnp.broadcast_to(m + jnp.log(l), (wb, B_HEAD_DIM)))
        o_ref[pl.ds(r0, wb), :] = jnp.concatenate(outs, axis=1)
        lse_ref[pl.ds(r0, wb), :] = jnp.concatenate(lses, axis=1)
        return carry

    lax.fori_loop(0, nsub, sub_block, 0)


def _dilated_attention(q, k, v, dil, nsub):
    bsz, s, w = q.shape
    rows = s // dil
    view = lambda t: t.reshape(bsz, rows, dil * w)
    blk = B_WB * nsub
    cur = pl.BlockSpec((None, blk, w), lambda b, r, n: (b, n, r))
    prev = pl.BlockSpec((None, B_WB, w), lambda b, r, n: (b, jnp.maximum(n * nsub - 1, 0), r))
    o, lse = pl.pallas_call(
        functools.partial(_attn_kernel, nsub=nsub),
        grid=(bsz, dil, rows // blk),
        in_specs=[cur, cur, prev, cur, prev],
        out_specs=[cur, cur],
        out_shape=[jax.ShapeDtypeStruct((bsz, rows, dil * w), F32)] * 2,
        scratch_shapes=[pltpu.VMEM((blk + B_WB, w), BF16), pltpu.VMEM((blk + B_WB, w), BF16)],
        compiler_params=_cparams(("parallel", "parallel", "arbitrary")),
        name=f"dilated_attention_d{dil}",
    )(view(q), view(k), view(k), view(v), view(v))
    return o.reshape(bsz, s, w), lse.reshape(bsz, s, w)


def _attn_out_kernel(x_ref, mod_ref, o0, o1, o2, l0, l1, l2, wo_ref, lng_ref, lnb_ref, wr_ref, br_ref,
                     x1_ref, h2_ref, route_ref):
    md = mod_ref[...]
    g1, sh2, sc2 = md[2:3, :], md[3:4, :], md[4:5, :]
    la, lb, lc = l0[...], l1[...], l2[...]
    mx = jnp.maximum(jnp.maximum(la, lb), lc)
    ea, eb, ec = jnp.exp(la - mx), jnp.exp(lb - mx), jnp.exp(lc - mx)
    o = (ea * o0[...] + eb * o1[...] + ec * o2[...]) * (1.0 / (ea + eb + ec))
    y = _dot(o.astype(BF16), wo_ref[...])
    x1 = _layer_norm(ALPHA * x_ref[...] + (1.0 + g1) * y, lng_ref[...], lnb_ref[...])
    x1_ref[...] = x1
    h2 = x1 * (1.0 + sc2) + sh2
    h2_ref[...] = h2
    route_ref[...] = _route_tile(h2, wr_ref, br_ref)


def _attn_out_layer(x, mod_l, outs, lses, wo, lng, lnb, wr, br):
    bsz, s, d = x.shape
    tok = lambda w: pl.BlockSpec((None, ROW_T, w), lambda b, i: (b, i, 0))
    const = lambda b, i: (0, 0)
    return pl.pallas_call(
        _attn_out_kernel,
        grid=(bsz, s // ROW_T),
        in_specs=[tok(d), pl.BlockSpec((None, 6, d), lambda b, i: (b, 0, 0))]
                 + [tok(B_WIDTH)] * 6
                 + [pl.BlockSpec((B_WIDTH, d), const), pl.BlockSpec((1, d), const), pl.BlockSpec((1, d), const),
                    pl.BlockSpec((d, LANES), const), pl.BlockSpec((1, LANES), const)],
        out_specs=[tok(d), tok(d), tok(LANES)],
        out_shape=[jax.ShapeDtypeStruct((bsz, s, d), F32),
                   jax.ShapeDtypeStruct((bsz, s, d), F32),
                   jax.ShapeDtypeStruct((bsz, s, LANES), F32)],
        compiler_params=_cparams(("parallel", "parallel")),
        name="attn_out_layer",
    )(x, mod_l, *outs, *lses, wo, lng, lnb, wr, br)


def _router_weights(w_r1, b_r1, w_r2, b_r2):
    d = w_r1.shape[0]
    n = MOE_GROUPS + MOE_EXPERTS
    wr = jnp.zeros((d, LANES), F32).at[:, :MOE_GROUPS].set(w_r1).at[:, MOE_GROUPS:n].set(w_r2)
    br = jnp.zeros((1, LANES), F32).at[0, :MOE_GROUPS].set(b_r1).at[0, MOE_GROUPS:n].set(b_r2)
    return wr, br


def kernel(x, c, positions, ada_w, ada_b, ln_g, ln_b, a_w_in, a_b_gate, a_norm_g, a_w_out, b_w_kv, b_w_q, b_w_o,
           moe_w_r1, moe_b_r1, moe_w_r2, moe_b_r2, moe_w_gate, moe_w_up, moe_w_down):
    bsz, s, d = x.shape
    assert d == D_MODEL and s % (B_WB * B_DILATIONS[-1]) == 0 and s % ROW_T == 0 and s % A_CHUNK == 0
    assert (bsz * s) % PLAN_T == 0 and (2 * bsz * s) % MOE_BM == 0
    mod = _adaln_mod(c, ada_w, ada_b)
    cos_t, sin_t = _rope_tables(positions)
    kv = None
    for l in range(DEPTH):
        lng1, lnb1 = ln_g[l, 0].reshape(1, d), ln_b[l, 0].reshape(1, d)
        lng2, lnb2 = ln_g[l, 1].reshape(1, d), ln_b[l, 1].reshape(1, d)
        wr, br = _router_weights(moe_w_r1[l], moe_b_r1[l], moe_w_r2[l], moe_b_r2[l])
        if l < N_A:
            n_main = 2 * A_QK + 2 * A_V
            win = jnp.zeros((d, n_main + LANES), BF16)
            win = win.at[:, :n_main + 2 * A_HEADS].set(a_w_in[l].astype(BF16))
            bg = jnp.zeros((1, LANES), F32).at[0, :2 * A_HEADS].set(a_b_gate[l])
            x1, h2, route = _mlstm_layer(x, mod[l], win, bg, a_norm_g[l].reshape(1, A_V),
                                         a_w_out[l].astype(BF16), lng1, lnb1, wr, br)
        else:
            lb = l - N_A
            qs = _q_project(x, mod[l], b_w_q[lb].astype(BF16), cos_t, sin_t)
            outs, lses = [], []
            for g in range(B_GROUPS):
                nsub = min(ATT_J[g], s // (B_WB * B_DILATIONS[g]))
                o_g, l_g = _dilated_attention(qs[g], kv[g], kv[B_GROUPS + g], B_DILATIONS[g], nsub)
                outs.append(o_g)
                lses.append(l_g)
            x1, h2, route = _attn_out_layer(x, mod[l], outs, lses, b_w_o[lb].astype(BF16), lng1, lnb1, wr, br)
        x = _moe_and_norm(x1, h2, route, mod[l], lng2, lnb2, moe_w_gate, moe_w_up, moe_w_down, l)
        if l == N_A - 1:
            kv = _kv_project(x, b_w_kv.astype(BF16), cos_t, sin_t)
    return x
```

```python
import functools

import jax
import jax.numpy as jnp
from jax import lax
from jax.experimental import pallas as pl
from jax.experimental.pallas import tpu as pltpu

F32 = jnp.float32
BF16 = jnp.bfloat16
I32 = jnp.int32
HI = lax.Precision.HIGHEST

D_MODEL = 1024
DEPTH = 4
N_A = DEPTH // 2
ALPHA = (2.0 * DEPTH) ** 0.25
LN_EPS = 1e-5
A_HEADS = 4
A_DQK = D_MODEL // 8
A_DV = D_MODEL // 4
A_NORM_EPS = 1e-6
A_QK = A_HEADS * A_DQK
A_V = A_HEADS * A_DV
B_WINDOWS = (128, 512, 2048)
B_DILATIONS = (1, 4, 16)
B_GROUPS = 3
B_HEADS = 8
B_HEAD_DIM = 64
B_WIDTH = B_HEADS * B_HEAD_DIM
B_WB = 128
ROPE_THETA = 10000.0
MOE_GROUPS = 4
MOE_EPG = 8
MOE_EXPERTS = MOE_GROUPS * MOE_EPG
MOE_HIDDEN = D_MODEL // 4

LANES = 128
SUBLANES = 8
VMEM_LIMIT = 56 * 1024 * 1024

A_CHUNK = 256
SORT_T = 512
SEG_Q = 16
SORT_R = 1536
EXP_BM = 512
EXP_CPB = EXP_BM // SEG_Q
ROW_T = 512
ATT_J = (8, 8, 4)
MOD_TN = 1536


def _cparams(sem):
    return pltpu.CompilerParams(dimension_semantics=sem, vmem_limit_bytes=VMEM_LIMIT)


def _dot(a, b):
    return jnp.dot(a, b, preferred_element_type=F32)


def _dot_nt(a, b):
    return lax.dot_general(a, b, (((1,), (1,)), ((), ())), preferred_element_type=F32)


def _dot_tn(a, b):
    return lax.dot_general(a, b, (((0,), (0,)), ((), ())), preferred_element_type=F32)


def _layer_norm(v, g, b):
    mu = jnp.mean(v, axis=-1, keepdims=True)
    vc = v - mu
    var = jnp.mean(vc * vc, axis=-1, keepdims=True)
    return vc * lax.rsqrt(var + LN_EPS) * g + b


def _sigmoid(v):
    return 1.0 / (1.0 + jnp.exp(-v))


def _route_tile(h2, wr_ref, br_ref):
    lg = jnp.dot(h2, wr_ref[...], preferred_element_type=F32, precision=HI) + br_ref[...]
    lane = lax.broadcasted_iota(I32, lg.shape, 1).astype(F32)
    neg = -jnp.inf
    big = 1000.0
    m1 = jnp.where(lane < MOE_GROUPS, lg, neg)
    mx = jnp.max(m1, axis=-1, keepdims=True)
    pg = 1.0 / jnp.sum(jnp.exp(m1 - mx), axis=-1, keepdims=True)
    gi = jnp.min(jnp.where(m1 == mx, lane, big), axis=-1, keepdims=True)
    lo = MOE_GROUPS + gi * MOE_EPG
    m2 = jnp.where((lane >= lo) & (lane < lo + MOE_EPG), lg, neg)
    v0 = jnp.max(m2, axis=-1, keepdims=True)
    j0 = jnp.min(jnp.where(m2 == v0, lane, big), axis=-1, keepdims=True)
    m3 = jnp.where(lane == j0, neg, m2)
    v1 = jnp.max(m3, axis=-1, keepdims=True)
    j1 = jnp.min(jnp.where(m3 == v1, lane, big), axis=-1, keepdims=True)
    t = jnp.exp(v1 - v0)
    wa = 1.0 / (1.0 + t)
    wb = t * wa
    out = jnp.where(lane == 0, j0 - MOE_GROUPS,
          jnp.where(lane == 1, j1 - MOE_GROUPS,
          jnp.where(lane == 2, pg * wa,
          jnp.where(lane == 3, pg * wb, 0.0))))
    return out


def _mod_kernel(c_ref, w_ref, b_ref, o_ref):
    c = c_ref[...]
    cond = c * _sigmoid(c)
    o_ref[0] = jnp.dot(cond, w_ref[0], preferred_element_type=F32, precision=HI) + b_ref[0]


def _adaln_mod(c, ada_w, ada_b):
    bsz, d = c.shape
    depth, _, n6 = ada_w.shape
    rows = 8
    c_pad = jnp.zeros((rows, d), F32).at[:bsz].set(c)
    out = pl.pallas_call(
        _mod_kernel,
        grid=(depth, n6 // MOD_TN),
        in_specs=[pl.BlockSpec((rows, d), lambda l, j: (0, 0)),
                  pl.BlockSpec((1, d, MOD_TN), lambda l, j: (l, 0, j)),
                  pl.BlockSpec((1, 1, MOD_TN), lambda l, j: (l, 0, j))],
        out_specs=pl.BlockSpec((1, rows, MOD_TN), lambda l, j: (l, 0, j)),
        out_shape=jax.ShapeDtypeStruct((depth, rows, n6), F32),
        compiler_params=_cparams(("parallel", "parallel")),
        name="adaln_mod",
    )(c_pad, ada_w, ada_b.reshape(depth, 1, n6))
    return out[:, :bsz].reshape(depth, bsz, 6, d)


def _mlstm_layer_kernel(x_ref, mod_ref, win_ref, bg_ref, ng_ref, wout_ref, lng_ref, lnb_ref,
                        wr_ref, br_ref, x1_ref, h2_ref, route_ref, c_ref, n_ref, m_ref):
    L = A_CHUNK

    @pl.when(pl.program_id(1) == 0)
    def _():
        c_ref[...] = jnp.zeros_like(c_ref)
        n_ref[...] = jnp.zeros_like(n_ref)
        m_ref[...] = jnp.zeros_like(m_ref)

    x = x_ref[...]
    md = mod_ref[...]
    sh1, sc1, g1, sh2, sc2, g2 = [md[j:j + 1, :] for j in range(6)]
    hb = (x * (1.0 + sc1) + sh1).astype(BF16)

    o_q, o_k, o_v, o_o, o_g = 0, A_QK, 2 * A_QK, 2 * A_QK + A_V, 2 * A_QK + 2 * A_V
    q_all = _dot(hb, win_ref[:, o_q:o_k])
    k_all = _dot(hb, win_ref[:, o_k:o_v]) * (A_DQK ** -0.5)
    v_all = _dot(hb, win_ref[:, o_v:o_o])
    og_all = _dot(hb, win_ref[:, o_o:o_g])
    gates = _dot(hb, win_ref[:, o_g:o_g + LANES]) + bg_ref[...]

    lf = jnp.minimum(gates, 0.0) - jnp.log(1.0 + jnp.exp(-jnp.abs(gates)))
    row = lax.broadcasted_iota(I32, (L, L), 0)
    col = lax.broadcasted_iota(I32, (L, L), 1)
    causal = row >= col
    tri = jnp.where(causal, 1.0, 0.0).astype(F32)
    b_col = jnp.dot(tri, lf, preferred_element_type=F32, precision=HI)
    g_t = gates.T
    b_t = b_col.T

    acc = jnp.zeros((L, D_MODEL), F32)
    for h in range(A_HEADS):
        qh = q_all[:, h * A_DQK:(h + 1) * A_DQK]
        kh = k_all[:, h * A_DQK:(h + 1) * A_DQK]
        vh = v_all[:, h * A_DV:(h + 1) * A_DV]
        qb, kb, vb = qh.astype(BF16), kh.astype(BF16), vh.astype(BF16)
        fl = A_HEADS + h
        b_c = b_col[:, fl:fl + 1]
        i_c = gates[:, h:h + 1]
        b_r = b_t[fl:fl + 1, :]
        i_r = g_t[h:h + 1, :]
        m_st = m_ref[h:h + 1, 0:1]
        c_st = c_ref[h]
        n_st = n_ref[h:h + 1, :]

        dmat = jnp.where(causal, b_c - b_r + i_r, -jnp.inf)
        inter = b_c + m_st
        m_t = jnp.maximum(inter, jnp.max(dmat, axis=-1, keepdims=True))
        dw = jnp.exp(dmat - m_t)
        iw = jnp.exp(inter - m_t)
        sc = _dot_nt(qb, kb) * dw
        num = _dot(sc.astype(BF16), vb) + iw * _dot(qb, c_st.astype(BF16))
        den = jnp.sum(sc, axis=-1, keepdims=True) + iw * jnp.sum(qh * n_st, axis=-1, keepdims=True)
        hh = num * (1.0 / jnp.maximum(jnp.abs(den), jnp.exp(-m_t)))

        b_last = b_col[L - 1:L, fl:fl + 1]
        ws_log = b_last - b_c + i_c
        m_new = jnp.maximum(b_last + m_st, jnp.max(ws_log, axis=0, keepdims=True))
        ws = jnp.exp(ws_log - m_new)
        decay = jnp.exp(b_last + m_st - m_new)
        kw = kh * ws
        c_ref[h] = decay * c_st + _dot_tn(kw.astype(BF16), vb)
        n_ref[h:h + 1, :] = decay * n_st + jnp.sum(kw, axis=0, keepdims=True)
        m_ref[h:h + 1, :] = jnp.broadcast_to(m_new, (1, LANES))

        mu = jnp.mean(hh, axis=-1, keepdims=True)
        hc = hh - mu
        var = jnp.mean(hc * hc, axis=-1, keepdims=True)
        hn = hc * lax.rsqrt(var + A_NORM_EPS)
        og = og_all[:, h * A_DV:(h + 1) * A_DV]
        yh = hn * ng_ref[:, h * A_DV:(h + 1) * A_DV] * _sigmoid(og)
        acc = acc + _dot(yh.astype(BF16), wout_ref[h * A_DV:(h + 1) * A_DV, :])

    x1 = _layer_norm(ALPHA * x + (1.0 + g1) * acc, lng_ref[...], lnb_ref[...])
    x1_ref[...] = x1
    h2 = x1 * (1.0 + sc2) + sh2
    h2_ref[...] = h2
    route_ref[...] = _route_tile(h2, wr_ref, br_ref)


def _mlstm_layer(x, mod_l, win, bg, ng, wout, lng, lnb, wr, br):
    bsz, s, d = x.shape
    L = A_CHUNK
    nw = win.shape[1]
    const = lambda b, i: (0, 0)
    tok = pl.BlockSpec((None, L, d), lambda b, i: (b, i, 0))
    return pl.pallas_call(
        _mlstm_layer_kernel,
        grid=(bsz, s // L),
        in_specs=[tok,
                  pl.BlockSpec((None, 6, d), lambda b, i: (b, 0, 0)),
                  pl.BlockSpec((d, nw), const),
                  pl.BlockSpec((1, LANES), const),
                  pl.BlockSpec((1, A_V), const),
                  pl.BlockSpec((A_V, d), const),
                  pl.BlockSpec((1, d), const),
                  pl.BlockSpec((1, d), const),
                  pl.BlockSpec((d, LANES), const),
                  pl.BlockSpec((1, LANES), const)],
        out_specs=[tok, tok, pl.BlockSpec((None, L, LANES), lambda b, i: (b, i, 0))],
        out_shape=[jax.ShapeDtypeStruct((bsz, s, d), F32),
                   jax.ShapeDtypeStruct((bsz, s, d), F32),
                   jax.ShapeDtypeStruct((bsz, s, LANES), F32)],
        scratch_shapes=[pltpu.VMEM((A_HEADS, A_DQK, A_DV), F32),
                        pltpu.VMEM((8, A_DQK), F32),
                        pltpu.VMEM((8, LANES), F32)],
        compiler_params=_cparams(("parallel", "arbitrary")),
        name="mlstm_layer",
    )(x, mod_l, win, bg, ng, wout, lng, lnb, wr, br)


def _moe_sort_kernel(h2_ref, route_ref, xs_ref, ws_ref, pos_ref, cnt_ref):
    n, nr = SORT_T, SORT_R
    r = route_ref[...]
    lane = lax.broadcasted_iota(I32, (n, LANES), 1).astype(F32)
    oh0 = lane == r[:, 0:1]
    oh1 = lane == r[:, 1:2]
    oh = jnp.where(oh0 | oh1, 1.0, 0.0)
    row = lax.broadcasted_iota(I32, (n, n), 0)
    col = lax.broadcasted_iota(I32, (n, n), 1)
    before = jnp.where(row > col, 1.0, 0.0).astype(BF16)
    excl = _dot(before, oh.astype(BF16))
    cnt = jnp.sum(oh, axis=0, keepdims=True)
    pcnt = jnp.ceil(cnt * (1.0 / SEG_Q)) * SEG_Q
    ej = lax.broadcasted_iota(I32, (LANES, LANES), 0)
    el = lax.broadcasted_iota(I32, (LANES, LANES), 1)
    upper = jnp.where(ej < el, 1.0, 0.0).astype(F32)
    seg = jnp.dot(jnp.broadcast_to(pcnt, (SUBLANES, LANES)), upper,
                  preferred_element_type=F32, precision=HI)[0:1, :]
    base = excl + seg
    p0 = jnp.sum(jnp.where(oh0, base, 0.0), axis=-1, keepdims=True)
    p1 = jnp.sum(jnp.where(oh1, base, 0.0), axis=-1, keepdims=True)
    pos = jnp.where(lane == 0, p0, jnp.where(lane == 1, p1, 0.0))
    pos_ref[...] = pos
    cnt_ref[...] = cnt
    pos_t = pos.T
    ri = lax.broadcasted_iota(I32, (nr, n), 0).astype(F32)
    sel0 = ri == pos_t[0:1, :]
    sel1 = ri == pos_t[1:2, :]
    perm = jnp.where(sel0 | sel1, 1.0, 0.0).astype(BF16)
    xs_ref[...] = _dot(perm, h2_ref[...].astype(BF16)).astype(BF16)

    def split3(w):
        hi = w.astype(BF16)
        r1 = w - hi.astype(F32)
        mid = r1.astype(BF16)
        lo = (r1 - mid.astype(F32)).astype(BF16)
        return jnp.where(lane == 0, hi.astype(F32),
               jnp.where(lane == 1, mid.astype(F32),
               jnp.where(lane == 2, lo.astype(F32), 0.0))).astype(BF16)

    wsum = (_dot(jnp.where(sel0, 1.0, 0.0).astype(BF16), split3(r[:, 2:3]))
            + _dot(jnp.where(sel1, 1.0, 0.0).astype(BF16), split3(r[:, 3:4])))
    ws = wsum[:, 0:1] + wsum[:, 1:2] + wsum[:, 2:3]
    ws_ref[...] = jnp.broadcast_to(ws, (nr, LANES))


def _chunk_list_kernel(seg_ref, nch_ref, src_ref, be_ref, bc_ref, nb_ref, *, n_tiles):
    n_src = src_ref.shape[0]
    n_blk = be_ref.shape[0]

    def clear_src(k, c):
        src_ref[k] = 0
        return c

    lax.fori_loop(0, n_src, clear_src, 0)

    def per_expert(e, carry):
        k0, nb0 = carry

        def per_tile(i, k):
            s = seg_ref[i * MOE_EXPERTS + e]
            n = nch_ref[i * MOE_EXPERTS + e]

            def per_chunk(c, kk):
                src_ref[kk] = i * SORT_R + s + c * SEG_Q
                return kk + 1

            return lax.fori_loop(0, n, per_chunk, k)

        k1 = lax.fori_loop(0, n_tiles, per_tile, k0)
        tot = k1 - k0
        nblk = (tot + EXP_CPB - 1) // EXP_CPB

        def per_block(j, c):
            be_ref[nb0 + j] = e
            bc_ref[nb0 + j] = jnp.minimum(EXP_CPB, tot - j * EXP_CPB)
            return c

        lax.fori_loop(0, nblk, per_block, 0)
        return k0 + nblk * EXP_CPB, nb0 + nblk

    _, nb = lax.fori_loop(0, MOE_EXPERTS, per_expert, (jnp.int32(0), jnp.int32(0)))
    nb_ref[0] = nb

    def spare_block(j, c):
        be_ref[j] = MOE_EXPERTS - 1
        bc_ref[j] = 0
        return c

    lax.fori_loop(nb, n_blk, spare_block, 0)


def _chunk_copy(src_ref, src_row, dst_ref, dst_row, sem):
    return pltpu.make_async_copy(src_ref.at[pl.ds(pl.multiple_of(src_row, SEG_Q), SEG_Q), :],
                                 dst_ref.at[pl.ds(pl.multiple_of(dst_row, SEG_Q), SEG_Q), :], sem)


def _expert_kernel(src_ref, be_ref, bc_ref, nb_ref, wg_ref, wu_ref, wd_ref, xs_ref, ys_ref,
                   wgu_s, wd_s, xg, yg, sem_in, sem_out):
    b = pl.program_id(0)
    nb = nb_ref[0]

    def gather(blk, slot):
        def body(c, carry):
            _chunk_copy(xs_ref, src_ref[blk * EXP_CPB + c], xg.at[slot], c * SEG_Q, sem_in.at[slot]).start()
            return carry
        lax.fori_loop(0, bc_ref[blk], body, 0)

    def wait_gather(blk, slot):
        def body(c, carry):
            _chunk_copy(xs_ref, 0, xg.at[slot], 0, sem_in.at[slot]).wait()
            return carry
        lax.fori_loop(0, bc_ref[blk], body, 0)

    def scatter(blk, slot):
        def body(c, carry):
            _chunk_copy(yg.at[slot], c * SEG_Q, ys_ref, src_ref[blk * EXP_CPB + c], sem_out.at[slot]).start()
            return carry
        lax.fori_loop(0, bc_ref[blk], body, 0)

    def wait_scatter(blk, slot):
        def body(c, carry):
            _chunk_copy(yg.at[slot], 0, ys_ref, 0, sem_out.at[slot]).wait()
            return carry
        lax.fori_loop(0, bc_ref[blk], body, 0)

    @pl.when(b == 0)
    def _():
        xg[...] = jnp.zeros_like(xg)
        gather(0, 0)

    @pl.when(b + 1 < nb)
    def _():
        gather(b + 1, (b + 1) & 1)

    @pl.when((b == 0) | (be_ref[b] != be_ref[jnp.maximum(b - 1, 0)]))
    def _():
        wgu_s[:, 0:MOE_HIDDEN] = wg_ref[...].astype(BF16)
        wgu_s[:, MOE_HIDDEN:2 * MOE_HIDDEN] = wu_ref[...].astype(BF16)
        wd_s[...] = wd_ref[...].astype(BF16)

    @pl.when(b < nb)
    def _():
        slot = b & 1
        wait_gather(b, slot)
        gu = _dot(xg[slot], wgu_s[...])
        g = gu[:, 0:MOE_HIDDEN]
        u = gu[:, MOE_HIDDEN:2 * MOE_HIDDEN]
        y = _dot((g * _sigmoid(g) * u).astype(BF16), wd_s[...])

        @pl.when(b >= 2)
        def _():
            wait_scatter(b - 2, slot)

        yg[slot] = y.astype(BF16)
        scatter(b, slot)

    @pl.when(b == pl.num_programs(0) - 1)
    def _():
        @pl.when(nb >= 2)
        def _():
            wait_scatter(nb - 2, nb & 1)
        wait_scatter(nb - 1, (nb - 1) & 1)


def _combine_kernel(ys_ref, ws_ref, pos_ref, x1_ref, mod_ref, lng_ref, lnb_ref, x2_ref):
    n, nr = SORT_T, SORT_R
    ysw = (ys_ref[...].astype(F32) * ws_ref[:, 0:1]).astype(BF16)
    pos = pos_ref[...]
    ci = lax.broadcasted_iota(I32, (n, nr), 1).astype(F32)
    unperm = jnp.where((ci == pos[:, 0:1]) | (ci == pos[:, 1:2]), 1.0, 0.0).astype(BF16)
    y = _dot(unperm, ysw)
    g2 = mod_ref[5:6, :]
    x2_ref[...] = _layer_norm(ALPHA * x1_ref[...] + (1.0 + g2) * y, lng_ref[...], lnb_ref[...])


def _moe_and_norm(x1, h2, route, mod_l, lng, lnb, wg, wu, wd, layer):
    bsz, s, d = x1.shape
    n_tok = bsz * s
    nt = n_tok // SORT_T
    n_rows = nt * SORT_R
    x1f = x1.reshape(n_tok, d)
    h2f = h2.reshape(n_tok, d)
    rt = route.reshape(n_tok, LANES)

    xs, ws, pos, cnt = pl.pallas_call(
        _moe_sort_kernel,
        grid=(nt,),
        in_specs=[pl.BlockSpec((SORT_T, d), lambda i: (i, 0)),
                  pl.BlockSpec((SORT_T, LANES), lambda i: (i, 0))],
        out_specs=[pl.BlockSpec((SORT_R, d), lambda i: (i, 0)),
                   pl.BlockSpec((SORT_R, LANES), lambda i: (i, 0)),
                   pl.BlockSpec((SORT_T, LANES), lambda i: (i, 0)),
                   pl.BlockSpec((None, 1, LANES), lambda i: (i, 0, 0))],
        out_shape=[jax.ShapeDtypeStruct((n_rows, d), BF16),
                   jax.ShapeDtypeStruct((n_rows, LANES), F32),
                   jax.ShapeDtypeStruct((n_tok, LANES), F32),
                   jax.ShapeDtypeStruct((nt, 1, LANES), F32)],
        compiler_params=_cparams(("parallel",)),
        name="moe_sort",
    )(h2f, rt)

    cnt_i = cnt[:, 0, :MOE_EXPERTS].astype(I32)
    pcnt = (cnt_i + SEG_Q - 1) // SEG_Q * SEG_Q
    seg = (jnp.cumsum(pcnt, axis=1) - pcnt).reshape(-1).astype(I32)
    nch = (pcnt // SEG_Q).reshape(-1).astype(I32)

    n_src = n_rows // SEG_Q + MOE_EXPERTS * EXP_CPB
    n_blk = n_src // EXP_CPB
    smem = pl.BlockSpec(memory_space=pltpu.SMEM)
    src, block_e, block_n, n_used = pl.pallas_call(
        functools.partial(_chunk_list_kernel, n_tiles=nt),
        in_specs=[smem, smem],
        out_specs=[smem, smem, smem, smem],
        out_shape=[jax.ShapeDtypeStruct((n_src,), I32),
                   jax.ShapeDtypeStruct((n_blk,), I32),
                   jax.ShapeDtypeStruct((n_blk,), I32),
                   jax.ShapeDtypeStruct((1,), I32)],
        name="moe_chunk_list",
    )(seg, nch)

    ys = pl.pallas_call(
        _expert_kernel,
        grid_spec=pltpu.PrefetchScalarGridSpec(
            num_scalar_prefetch=4,
            grid=(n_blk,),
            in_specs=[pl.BlockSpec((None, None, d, MOE_HIDDEN), lambda b, src, be, bc, nb: (layer, be[b], 0, 0)),
                      pl.BlockSpec((None, None, d, MOE_HIDDEN), lambda b, src, be, bc, nb: (layer, be[b], 0, 0)),
                      pl.BlockSpec((None, None, MOE_HIDDEN, d), lambda b, src, be, bc, nb: (layer, be[b], 0, 0)),
                      pl.BlockSpec(memory_space=pl.ANY)],
            out_specs=pl.BlockSpec(memory_space=pl.ANY),
            scratch_shapes=[pltpu.VMEM((d, 2 * MOE_HIDDEN), BF16),
                            pltpu.VMEM((MOE_HIDDEN, d), BF16),
                            pltpu.VMEM((2, EXP_BM, d), BF16),
                            pltpu.VMEM((2, EXP_BM, d), BF16),
                            pltpu.SemaphoreType.DMA((2,)),
                            pltpu.SemaphoreType.DMA((2,))]),
        out_shape=jax.ShapeDtypeStruct((n_rows, d), BF16),
        input_output_aliases={7: 0},
        compiler_params=_cparams(("arbitrary",)),
        name="moe_experts",
    )(src, block_e, block_n, n_used, wg, wu, wd, xs)

    per_b = s // SORT_T
    x2 = pl.pallas_call(
        _combine_kernel,
        grid=(nt,),
        in_specs=[pl.BlockSpec((SORT_R, d), lambda i: (i, 0)),
                  pl.BlockSpec((SORT_R, LANES), lambda i: (i, 0)),
                  pl.BlockSpec((SORT_T, LANES), lambda i: (i, 0)),
                  pl.BlockSpec((SORT_T, d), lambda i: (i, 0)),
                  pl.BlockSpec((None, 6, d), lambda i: (i // per_b, 0, 0)),
                  pl.BlockSpec((1, d), lambda i: (0, 0)),
                  pl.BlockSpec((1, d), lambda i: (0, 0))],
        out_specs=pl.BlockSpec((SORT_T, d), lambda i: (i, 0)),
        out_shape=jax.ShapeDtypeStruct((n_tok, d), F32),
        compiler_params=_cparams(("parallel",)),
        name="moe_combine_norm",
    )(ys, ws, pos, x1f, mod_l, lng, lnb)
    return x2.reshape(bsz, s, d)


def _rope_table_kernel(pos_ref, inv_ref, sgn_ref, cos_ref, sin_ref):
    ang = pos_ref[...] * inv_ref[...]
    cos_ref[...] = jnp.cos(ang)
    sin_ref[...] = jnp.sin(ang) * sgn_ref[...]


def _rope_tables(positions):
    bsz, s = positions.shape
    half = B_HEAD_DIM // 2
    inv = jnp.power(jnp.float32(ROPE_THETA), -jnp.arange(half, dtype=F32) * 2.0 / B_HEAD_DIM)
    inv_row = jnp.tile(inv, LANES // half).reshape(1, LANES)
    sgn_row = jnp.tile(jnp.concatenate([-jnp.ones((half,), F32), jnp.ones((half,), F32)]),
                       LANES // B_HEAD_DIM).reshape(1, LANES)
    pos = jnp.broadcast_to(positions.astype(F32)[:, :, None], (bsz, s, LANES))
    tok = pl.BlockSpec((None, ROW_T, LANES), lambda b, i: (b, i, 0))
    row = pl.BlockSpec((1, LANES), lambda b, i: (0, 0))
    return pl.pallas_call(
        _rope_table_kernel,
        grid=(bsz, s // ROW_T),
        in_specs=[tok, row, row],
        out_specs=[tok, tok],
        out_shape=[jax.ShapeDtypeStruct((bsz, s, LANES), F32)] * 2,
        compiler_params=_cparams(("parallel", "parallel")),
        name="rope_tables",
    )(pos, inv_row, sgn_row)


def _rope(t, cosf, sinf, first_half):
    n = t.shape[1]
    half = B_HEAD_DIM // 2
    fwd = pltpu.roll(t, n - half, 1)
    bwd = pltpu.roll(t, half, 1)
    return t * cosf + jnp.where(first_half, fwd, bwd) * sinf


def _rope_operands(cos_ref, sin_ref, rows):
    reps = B_WIDTH // LANES
    cosf = jnp.concatenate([cos_ref[...]] * reps, axis=1)
    sinf = jnp.concatenate([sin_ref[...]] * reps, axis=1)
    lane = lax.broadcasted_iota(I32, (rows, B_WIDTH), 1)
    first_half = (lane & (B_HEAD_DIM - 1)) < (B_HEAD_DIM // 2)
    return cosf, sinf, first_half


def _kv_kernel(x_ref, w_ref, cos_ref, sin_ref, k0, k1, k2, v0, v1, v2):
    xb = x_ref[...].astype(BF16)
    cosf, sinf, first_half = _rope_operands(cos_ref, sin_ref, ROW_T)
    for g, (k_ref, v_ref) in enumerate(((k0, v0), (k1, v1), (k2, v2))):
        kg = _dot(xb, w_ref[:, g * B_WIDTH:(g + 1) * B_WIDTH])
        k_ref[...] = _rope(kg, cosf, sinf, first_half).astype(BF16)
        vo = (B_GROUPS + g) * B_WIDTH
        v_ref[...] = _dot(xb, w_ref[:, vo:vo + B_WIDTH]).astype(BF16)


def _kv_project(x, wkv, cos_t, sin_t):
    bsz, s, d = x.shape
    tok = lambda w: pl.BlockSpec((None, ROW_T, w), lambda b, i: (b, i, 0))
    return pl.pallas_call(
        _kv_kernel,
        grid=(bsz, s // ROW_T),
        in_specs=[tok(d), pl.BlockSpec(wkv.shape, lambda b, i: (0, 0)), tok(LANES), tok(LANES)],
        out_specs=[tok(B_WIDTH)] * 6,
        out_shape=[jax.ShapeDtypeStruct((bsz, s, B_WIDTH), BF16)] * 6,
        compiler_params=_cparams(("parallel", "parallel")),
        name="kv_project",
    )(x, wkv, cos_t, sin_t)


def _q_kernel(x_ref, mod_ref, w_ref, cos_ref, sin_ref, q0, q1, q2):
    md = mod_ref[...]
    hb = (x_ref[...] * (1.0 + md[1:2, :]) + md[0:1, :]).astype(BF16)
    cosf, sinf, first_half = _rope_operands(cos_ref, sin_ref, ROW_T)
    for g, q_ref in enumerate((q0, q1, q2)):
        qg = _dot(hb, w_ref[:, g * B_WIDTH:(g + 1) * B_WIDTH])
        q_ref[...] = (_rope(qg, cosf, sinf, first_half) * (B_HEAD_DIM ** -0.5)).astype(BF16)


def _q_project(x, mod_l, wq, cos_t, sin_t):
    bsz, s, d = x.shape
    tok = lambda w: pl.BlockSpec((None, ROW_T, w), lambda b, i: (b, i, 0))
    return pl.pallas_call(
        _q_kernel,
        grid=(bsz, s // ROW_T),
        in_specs=[tok(d), pl.BlockSpec((None, 6, d), lambda b, i: (b, 0, 0)),
                  pl.BlockSpec(wq.shape, lambda b, i: (0, 0)), tok(LANES), tok(LANES)],
        out_specs=[tok(B_WIDTH)] * 3,
        out_shape=[jax.ShapeDtypeStruct((bsz, s, B_WIDTH), BF16)] * 3,
        compiler_params=_cparams(("parallel", "parallel")),
        name="q_project",
    )(x, mod_l, wq, cos_t, sin_t)


def _attn_kernel(q_ref, kc_ref, kp_ref, vc_ref, vp_ref, o_ref, lse_ref, kw_ref, vw_ref, *, nsub):
    wb = B_WB
    first_step = pl.program_id(2) == 0
    kw_ref[0:wb, :] = kp_ref[...]
    kw_ref[wb:, :] = kc_ref[...]
    vw_ref[0:wb, :] = vp_ref[...]
    vw_ref[wb:, :] = vc_ref[...]
    qi = lax.broadcasted_iota(I32, (wb, 2 * wb), 0)
    kj = lax.broadcasted_iota(I32, (wb, 2 * wb), 1)
    band = (kj >= qi) & (kj <= qi + wb)

    def sub_block(j, carry):
        r0 = pl.multiple_of(j * wb, wb)
        valid = band & ((kj >= wb) | (j > 0) | jnp.logical_not(first_step))
        q = q_ref[pl.ds(r0, wb), :]
        kwin = kw_ref[pl.ds(r0, 2 * wb), :]
        vwin = vw_ref[pl.ds(r0, 2 * wb), :]
        outs, lses = [], []
        for h in range(B_HEADS):
            sl = slice(h * B_HEAD_DIM, (h + 1) * B_HEAD_DIM)
            sc = jnp.where(valid, _dot_nt(q[:, sl], kwin[:, sl]), -jnp.inf)
            m = jnp.max(sc, axis=-1, keepdims=True)
            p = jnp.exp(sc - m)
            l = jnp.sum(p, axis=-1, keepdims=True)
            outs.append(_dot(p.astype(BF16), vwin[:, sl]) * (1.0 / l))
            lses.append(jnp.broadcast_to(m + jnp.log(l), (wb, B_HEAD_DIM)))
        o_ref[pl.ds(r0, wb), :] = jnp.concatenate(outs, axis=1)
        lse_ref[pl.ds(r0, wb), :] = jnp.concatenate(lses, axis=1)
        return carry

    lax.fori_loop(0, nsub, sub_block, 0)


def _dilated_attention(q, k, v, dil, nsub):
    bsz, s, w = q.shape
    rows = s // dil
    view = lambda t: t.reshape(bsz, rows, dil * w)
    blk = B_WB * nsub
    cur = pl.BlockSpec((None, blk, w), lambda b, r, n: (b, n, r))
    prev = pl.BlockSpec((None, B_WB, w), lambda b, r, n: (b, jnp.maximum(n * nsub - 1, 0), r))
    o, lse = pl.pallas_call(
        functools.partial(_attn_kernel, nsub=nsub),
        grid=(bsz, dil, rows // blk),
        in_specs=[cur, cur, prev, cur, prev],
        out_specs=[cur, cur],
        out_shape=[jax.ShapeDtypeStruct((bsz, rows, dil * w), F32)] * 2,
        scratch_shapes=[pltpu.VMEM((blk + B_WB, w), BF16), pltpu.VMEM((blk + B_WB, w), BF16)],
        compiler_params=_cparams(("parallel", "parallel", "arbitrary")),
        name=f"dilated_attention_d{dil}",
    )(view(q), view(k), view(k), view(v), view(v))
    return o.reshape(bsz, s, w), lse.reshape(bsz, s, w)


def _attn_out_kernel(x_ref, mod_ref, o0, o1, o2, l0, l1, l2, wo_ref, lng_ref, lnb_ref, wr_ref, br_ref,
                     x1_ref, h2_ref, route_ref):
    md = mod_ref[...]
    g1, sh2, sc2 = md[2:3, :], md[3:4, :], md[4:5, :]
    la, lb, lc = l0[...], l1[...], l2[...]
    mx = jnp.maximum(jnp.maximum(la, lb), lc)
    ea, eb, ec = jnp.exp(la - mx), jnp.exp(lb - mx), jnp.exp(lc - mx)
    o = (ea * o0[...] + eb * o1[...] + ec * o2[...]) * (1.0 / (ea + eb + ec))
    y = _dot(o.astype(BF16), wo_ref[...])
    x1 = _layer_norm(ALPHA * x_ref[...] + (1.0 + g1) * y, lng_ref[...], lnb_ref[...])
    x1_ref[...] = x1
    h2 = x1 * (1.0 + sc2) + sh2
    h2_ref[...] = h2
    route_ref[...] = _route_tile(h2, wr_ref, br_ref)


def _attn_out_layer(x, mod_l, outs, lses, wo, lng, lnb, wr, br):
    bsz, s, d = x.shape
    tok = lambda w: pl.BlockSpec((None, ROW_T, w), lambda b, i: (b, i, 0))
    const = lambda b, i: (0, 0)
    return pl.pallas_call(
        _attn_out_kernel,
        grid=(bsz, s // ROW_T),
        in_specs=[tok(d), pl.BlockSpec((None, 6, d), lambda b, i: (b, 0, 0))]
                 + [tok(B_WIDTH)] * 6
                 + [pl.BlockSpec((B_WIDTH, d), const), pl.BlockSpec((1, d), const), pl.BlockSpec((1, d), const),
                    pl.BlockSpec((d, LANES), const), pl.BlockSpec((1, LANES), const)],
        out_specs=[tok(d), tok(d), tok(LANES)],
        out_shape=[jax.ShapeDtypeStruct((bsz, s, d), F32),
                   jax.ShapeDtypeStruct((bsz, s, d), F32),
                   jax.ShapeDtypeStruct((bsz, s, LANES), F32)],
        compiler_params=_cparams(("parallel", "parallel")),
        name="attn_out_layer",
    )(x, mod_l, *outs, *lses, wo, lng, lnb, wr, br)


def _router_weights(w_r1, b_r1, w_r2, b_r2):
    d = w_r1.shape[0]
    n = MOE_GROUPS + MOE_EXPERTS
    wr = jnp.zeros((d, LANES), F32).at[:, :MOE_GROUPS].set(w_r1).at[:, MOE_GROUPS:n].set(w_r2)
    br = jnp.zeros((1, LANES), F32).at[0, :MOE_GROUPS].set(b_r1).at[0, MOE_GROUPS:n].set(b_r2)
    return wr, br


def kernel(x, c, positions, ada_w, ada_b, ln_g, ln_b, a_w_in, a_b_gate, a_norm_g, a_w_out, b_w_kv, b_w_q, b_w_o,
           moe_w_r1, moe_b_r1, moe_w_r2, moe_b_r2, moe_w_gate, moe_w_up, moe_w_down):
    bsz, s, d = x.shape
    assert d == D_MODEL and s % (B_WB * B_DILATIONS[-1]) == 0 and s % ROW_T == 0 and s % A_CHUNK == 0
    assert (bsz * s) % SORT_T == 0 and s % SORT_T == 0
    assert 2 * SORT_T + MOE_EXPERTS * (SEG_Q - 1) <= SORT_R and SORT_R % SEG_Q == 0
    mod = _adaln_mod(c, ada_w, ada_b)
    cos_t, sin_t = _rope_tables(positions)
    kv = None
    for l in range(DEPTH):
        lng1, lnb1 = ln_g[l, 0].reshape(1, d), ln_b[l, 0].reshape(1, d)
        lng2, lnb2 = ln_g[l, 1].reshape(1, d), ln_b[l, 1].reshape(1, d)
        wr, br = _router_weights(moe_w_r1[l], moe_b_r1[l], moe_w_r2[l], moe_b_r2[l])
        if l < N_A:
            n_main = 2 * A_QK + 2 * A_V
            win = jnp.zeros((d, n_main + LANES), BF16)
            win = win.at[:, :n_main + 2 * A_HEADS].set(a_w_in[l].astype(BF16))
            bg = jnp.zeros((1, LANES), F32).at[0, :2 * A_HEADS].set(a_b_gate[l])
            x1, h2, route = _mlstm_layer(x, mod[l], win, bg, a_norm_g[l].reshape(1, A_V),
                                         a_w_out[l].astype(BF16), lng1, lnb1, wr, br)
        else:
            lb = l - N_A
            qs = _q_project(x, mod[l], b_w_q[lb].astype(BF16), cos_t, sin_t)
            outs, lses = [], []
            for g in range(B_GROUPS):
                nsub = min(ATT_J[g], s // (B_WB * B_DILATIONS[g]))
                o_g, l_g = _dilated_attention(qs[g], kv[g], kv[B_GROUPS + g], B_DILATIONS[g], nsub)
                outs.append(o_g)
                lses.append(l_g)
            x1, h2, route = _attn_out_layer(x, mod[l], outs, lses, b_w_o[lb].astype(BF16), lng1, lnb1, wr, br)
        x = _moe_and_norm(x1, h2, route, mod[l], lng2, lnb2, moe_w_gate, moe_w_up, moe_w_down, l)
        if l == N_A - 1:
            kv = _kv_project(x, b_w_kv.astype(BF16), cos_t, sin_t)
    return x
```

```python
import functools

import jax
import jax.numpy as jnp
from jax import lax
from jax.experimental import pallas as pl
from jax.experimental.pallas import tpu as pltpu

F32 = jnp.float32
BF16 = jnp.bfloat16
I32 = jnp.int32
HI = lax.Precision.HIGHEST

D_MODEL = 1024
DEPTH = 4
N_A = DEPTH // 2
ALPHA = (2.0 * DEPTH) ** 0.25
LN_EPS = 1e-5
A_HEADS = 4
A_DQK = D_MODEL // 8
A_DV = D_MODEL // 4
A_NORM_EPS = 1e-6
A_QK = A_HEADS * A_DQK
A_V = A_HEADS * A_DV
B_WINDOWS = (128, 512, 2048)
B_DILATIONS = (1, 4, 16)
B_GROUPS = 3
B_HEADS = 8
B_HEAD_DIM = 64
B_WIDTH = B_HEADS * B_HEAD_DIM
B_WB = 128
ROPE_THETA = 10000.0
MOE_GROUPS = 4
MOE_EPG = 8
MOE_EXPERTS = MOE_GROUPS * MOE_EPG
MOE_HIDDEN = D_MODEL // 4

LANES = 128
SUBLANES = 8
VMEM_LIMIT = 56 * 1024 * 1024

A_CHUNK = 256
SORT_T = 512
SEG_Q = 16
SORT_R = 1536
EXP_BM = 512
EXP_CPB = EXP_BM // SEG_Q
ROW_T = 512
ATT_J = (8, 8, 4)
MOD_TN = 1536


def _cparams(sem):
    return pltpu.CompilerParams(dimension_semantics=sem, vmem_limit_bytes=VMEM_LIMIT)


def _dot(a, b):
    return jnp.dot(a, b, preferred_element_type=F32)


def _dot_nt(a, b):
    return lax.dot_general(a, b, (((1,), (1,)), ((), ())), preferred_element_type=F32)


def _dot_tn(a, b):
    return lax.dot_general(a, b, (((0,), (0,)), ((), ())), preferred_element_type=F32)


def _layer_norm(v, g, b):
    mu = jnp.mean(v, axis=-1, keepdims=True)
    vc = v - mu
    var = jnp.mean(vc * vc, axis=-1, keepdims=True)
    return vc * lax.rsqrt(var + LN_EPS) * g + b


def _sigmoid(v):
    return 1.0 / (1.0 + jnp.exp(-v))


def _route_tile(h2, wr_ref, br_ref):
    lg = jnp.dot(h2, wr_ref[...], preferred_element_type=F32, precision=HI) + br_ref[...]
    lane = lax.broadcasted_iota(I32, lg.shape, 1).astype(F32)
    neg = -jnp.inf
    big = 1000.0
    m1 = jnp.where(lane < MOE_GROUPS, lg, neg)
    mx = jnp.max(m1, axis=-1, keepdims=True)
    pg = 1.0 / jnp.sum(jnp.exp(m1 - mx), axis=-1, keepdims=True)
    gi = jnp.min(jnp.where(m1 == mx, lane, big), axis=-1, keepdims=True)
    lo = MOE_GROUPS + gi * MOE_EPG
    m2 = jnp.where((lane >= lo) & (lane < lo + MOE_EPG), lg, neg)
    v0 = jnp.max(m2, axis=-1, keepdims=True)
    j0 = jnp.min(jnp.where(m2 == v0, lane, big), axis=-1, keepdims=True)
    m3 = jnp.where(lane == j0, neg, m2)
    v1 = jnp.max(m3, axis=-1, keepdims=True)
    j1 = jnp.min(jnp.where(m3 == v1, lane, big), axis=-1, keepdims=True)
    t = jnp.exp(v1 - v0)
    wa = 1.0 / (1.0 + t)
    wb = t * wa
    out = jnp.where(lane == 0, j0 - MOE_GROUPS,
          jnp.where(lane == 1, j1 - MOE_GROUPS,
          jnp.where(lane == 2, pg * wa,
          jnp.where(lane == 3, pg * wb, 0.0))))
    return out


def _mod_kernel(c_ref, w_ref, b_ref, o_ref):
    c = c_ref[...]
    cond = c * _sigmoid(c)
    o_ref[0] = jnp.dot(cond, w_ref[0], preferred_element_type=F32, precision=HI) + b_ref[0]


def _adaln_mod(c, ada_w, ada_b):
    bsz, d = c.shape
    depth, _, n6 = ada_w.shape
    rows = 8
    c_pad = jnp.zeros((rows, d), F32).at[:bsz].set(c)
    out = pl.pallas_call(
        _mod_kernel,
        grid=(depth, n6 // MOD_TN),
        in_specs=[pl.BlockSpec((rows, d), lambda l, j: (0, 0)),
                  pl.BlockSpec((1, d, MOD_TN), lambda l, j: (l, 0, j)),
                  pl.BlockSpec((1, 1, MOD_TN), lambda l, j: (l, 0, j))],
        out_specs=pl.BlockSpec((1, rows, MOD_TN), lambda l, j: (l, 0, j)),
        out_shape=jax.ShapeDtypeStruct((depth, rows, n6), F32),
        compiler_params=_cparams(("parallel", "parallel")),
        name="adaln_mod",
    )(c_pad, ada_w, ada_b.reshape(depth, 1, n6))
    return out[:, :bsz].reshape(depth, bsz, 6, d)


def _mlstm_layer_kernel(x_ref, mod_ref, win_ref, bg_ref, ng_ref, wout_ref, lng_ref, lnb_ref,
                        wr_ref, br_ref, x1_ref, h2_ref, route_ref, c_ref, n_ref, m_ref):
    L = A_CHUNK

    @pl.when(pl.program_id(1) == 0)
    def _():
        c_ref[...] = jnp.zeros_like(c_ref)
        n_ref[...] = jnp.zeros_like(n_ref)
        m_ref[...] = jnp.zeros_like(m_ref)

    x = x_ref[...]
    md = mod_ref[...]
    sh1, sc1, g1, sh2, sc2, g2 = [md[j:j + 1, :] for j in range(6)]
    hb = (x * (1.0 + sc1) + sh1).astype(BF16)

    o_q, o_k, o_v, o_o, o_g = 0, A_QK, 2 * A_QK, 2 * A_QK + A_V, 2 * A_QK + 2 * A_V
    q_all = _dot(hb, win_ref[:, o_q:o_k])
    k_all = _dot(hb, win_ref[:, o_k:o_v]) * (A_DQK ** -0.5)
    v_all = _dot(hb, win_ref[:, o_v:o_o])
    og_all = _dot(hb, win_ref[:, o_o:o_g])
    gates = _dot(hb, win_ref[:, o_g:o_g + LANES]) + bg_ref[...]

    lf = jnp.minimum(gates, 0.0) - jnp.log(1.0 + jnp.exp(-jnp.abs(gates)))
    row = lax.broadcasted_iota(I32, (L, L), 0)
    col = lax.broadcasted_iota(I32, (L, L), 1)
    causal = row >= col
    tri = jnp.where(causal, 1.0, 0.0).astype(F32)
    b_col = jnp.dot(tri, lf, preferred_element_type=F32, precision=HI)
    g_t = gates.T
    b_t = b_col.T

    acc = jnp.zeros((L, D_MODEL), F32)
    for h in range(A_HEADS):
        qh = q_all[:, h * A_DQK:(h + 1) * A_DQK]
        kh = k_all[:, h * A_DQK:(h + 1) * A_DQK]
        vh = v_all[:, h * A_DV:(h + 1) * A_DV]
        qb, kb, vb = qh.astype(BF16), kh.astype(BF16), vh.astype(BF16)
        fl = A_HEADS + h
        b_c = b_col[:, fl:fl + 1]
        i_c = gates[:, h:h + 1]
        b_r = b_t[fl:fl + 1, :]
        i_r = g_t[h:h + 1, :]
        m_st = m_ref[h:h + 1, 0:1]
        c_st = c_ref[h]
        n_st = n_ref[h:h + 1, :]

        dmat = jnp.where(causal, b_c - b_r + i_r, -jnp.inf)
        inter = b_c + m_st
        m_t = jnp.maximum(inter, jnp.max(dmat, axis=-1, keepdims=True))
        dw = jnp.exp(dmat - m_t)
        iw = jnp.exp(inter - m_t)
        sc = _dot_nt(qb, kb) * dw
        num = _dot(sc.astype(BF16), vb) + iw * _dot(qb, c_st.astype(BF16))
        den = jnp.sum(sc, axis=-1, keepdims=True) + iw * jnp.sum(qh * n_st, axis=-1, keepdims=True)
        hh = num * (1.0 / jnp.maximum(jnp.abs(den), jnp.exp(-m_t)))

        b_last = b_col[L - 1:L, fl:fl + 1]
        ws_log = b_last - b_c + i_c
        m_new = jnp.maximum(b_last + m_st, jnp.max(ws_log, axis=0, keepdims=True))
        ws = jnp.exp(ws_log - m_new)
        decay = jnp.exp(b_last + m_st - m_new)
        kw = kh * ws
        c_ref[h] = decay * c_st + _dot_tn(kw.astype(BF16), vb)
        n_ref[h:h + 1, :] = decay * n_st + jnp.sum(kw, axis=0, keepdims=True)
        m_ref[h:h + 1, :] = jnp.broadcast_to(m_new, (1, LANES))

        mu = jnp.mean(hh, axis=-1, keepdims=True)
        hc = hh - mu
        var = jnp.mean(hc * hc, axis=-1, keepdims=True)
        hn = hc * lax.rsqrt(var + A_NORM_EPS)
        og = og_all[:, h * A_DV:(h + 1) * A_DV]
        yh = hn * ng_ref[:, h * A_DV:(h + 1) * A_DV] * _sigmoid(og)
        acc = acc + _dot(yh.astype(BF16), wout_ref[h * A_DV:(h + 1) * A_DV, :])

    x1 = _layer_norm(ALPHA * x + (1.0 + g1) * acc, lng_ref[...], lnb_ref[...])
    x1_ref[...] = x1
    h2 = x1 * (1.0 + sc2) + sh2
    h2_ref[...] = h2
    route_ref[...] = _route_tile(h2, wr_ref, br_ref)


def _mlstm_layer(x, mod_l, win, bg, ng, wout, lng, lnb, wr, br):
    bsz, s, d = x.shape
    L = A_CHUNK
    nw = win.shape[1]
    const = lambda b, i: (0, 0)
    tok = pl.BlockSpec((None, L, d), lambda b, i: (b, i, 0))
    return pl.pallas_call(
        _mlstm_layer_kernel,
        grid=(bsz, s // L),
        in_specs=[tok,
                  pl.BlockSpec((None, 6, d), lambda b, i: (b, 0, 0)),
                  pl.BlockSpec((d, nw), const),
                  pl.BlockSpec((1, LANES), const),
                  pl.BlockSpec((1, A_V), const),
                  pl.BlockSpec((A_V, d), const),
                  pl.BlockSpec((1, d), const),
                  pl.BlockSpec((1, d), const),
                  pl.BlockSpec((d, LANES), const),
                  pl.BlockSpec((1, LANES), const)],
        out_specs=[tok, tok, pl.BlockSpec((None, L, LANES), lambda b, i: (b, i, 0))],
        out_shape=[jax.ShapeDtypeStruct((bsz, s, d), F32),
                   jax.ShapeDtypeStruct((bsz, s, d), F32),
                   jax.ShapeDtypeStruct((bsz, s, LANES), F32)],
        scratch_shapes=[pltpu.VMEM((A_HEADS, A_DQK, A_DV), F32),
                        pltpu.VMEM((8, A_DQK), F32),
                        pltpu.VMEM((8, LANES), F32)],
        compiler_params=_cparams(("parallel", "arbitrary")),
        name="mlstm_layer",
    )(x, mod_l, win, bg, ng, wout, lng, lnb, wr, br)


def _moe_sort_kernel(h2_ref, route_ref, xs_ref, ws_ref, pos_ref, cnt_ref):
    n, nr = SORT_T, SORT_R
    r = route_ref[...]
    lane = lax.broadcasted_iota(I32, (n, LANES), 1).astype(F32)
    oh0 = lane == r[:, 0:1]
    oh1 = lane == r[:, 1:2]
    oh = jnp.where(oh0 | oh1, 1.0, 0.0)
    row = lax.broadcasted_iota(I32, (n, n), 0)
    col = lax.broadcasted_iota(I32, (n, n), 1)
    before = jnp.where(row > col, 1.0, 0.0).astype(BF16)
    excl = _dot(before, oh.astype(BF16))
    cnt = jnp.sum(oh, axis=0, keepdims=True)
    pcnt = jnp.ceil(cnt * (1.0 / SEG_Q)) * SEG_Q
    ej = lax.broadcasted_iota(I32, (LANES, LANES), 0)
    el = lax.broadcasted_iota(I32, (LANES, LANES), 1)
    upper = jnp.where(ej < el, 1.0, 0.0).astype(F32)
    seg = jnp.dot(jnp.broadcast_to(pcnt, (SUBLANES, LANES)), upper,
                  preferred_element_type=F32, precision=HI)[0:1, :]
    base = excl + seg
    p0 = jnp.sum(jnp.where(oh0, base, 0.0), axis=-1, keepdims=True)
    p1 = jnp.sum(jnp.where(oh1, base, 0.0), axis=-1, keepdims=True)
    pos = jnp.where(lane == 0, p0, jnp.where(lane == 1, p1, 0.0))
    pos_ref[...] = pos
    cnt_ref[...] = cnt
    pos_t = pos.T
    ri = lax.broadcasted_iota(I32, (nr, n), 0).astype(F32)
    sel0 = ri == pos_t[0:1, :]
    sel1 = ri == pos_t[1:2, :]
    perm = jnp.where(sel0 | sel1, 1.0, 0.0).astype(BF16)
    xs_ref[...] = _dot(perm, h2_ref[...].astype(BF16)).astype(BF16)

    def split3(w):
        hi = w.astype(BF16)
        r1 = w - hi.astype(F32)
        mid = r1.astype(BF16)
        lo = (r1 - mid.astype(F32)).astype(BF16)
        return jnp.where(lane == 0, hi.astype(F32),
               jnp.where(lane == 1, mid.astype(F32),
               jnp.where(lane == 2, lo.astype(F32), 0.0))).astype(BF16)

    wsum = (_dot(jnp.where(sel0, 1.0, 0.0).astype(BF16), split3(r[:, 2:3]))
            + _dot(jnp.where(sel1, 1.0, 0.0).astype(BF16), split3(r[:, 3:4])))
    ws = wsum[:, 0:1] + wsum[:, 1:2] + wsum[:, 2:3]
    ws_ref[...] = jnp.broadcast_to(ws, (nr, LANES))


def _chunk_list_kernel(seg_ref, nch_ref, src_ref, be_ref, bc_ref, nb_ref, *, n_tiles):
    n_src = src_ref.shape[0]
    n_blk = be_ref.shape[0]

    def clear_src(k, c):
        src_ref[k] = 0
        return c

    lax.fori_loop(0, n_src, clear_src, 0)

    def per_expert(e, carry):
        k0, nb0 = carry

        def per_tile(i, k):
            s = seg_ref[i * MOE_EXPERTS + e]
            n = nch_ref[i * MOE_EXPERTS + e]

            def per_chunk(c, kk):
                src_ref[kk] = i * SORT_R + s + c * SEG_Q
                return kk + 1

            return lax.fori_loop(0, n, per_chunk, k)

        k1 = lax.fori_loop(0, n_tiles, per_tile, k0)
        tot = k1 - k0
        nblk = (tot + EXP_CPB - 1) // EXP_CPB

        def per_block(j, c):
            be_ref[nb0 + j] = e
            bc_ref[nb0 + j] = jnp.minimum(EXP_CPB, tot - j * EXP_CPB)
            return c

        lax.fori_loop(0, nblk, per_block, 0)
        return k0 + nblk * EXP_CPB, nb0 + nblk

    _, nb = lax.fori_loop(0, MOE_EXPERTS, per_expert, (jnp.int32(0), jnp.int32(0)))
    nb_ref[0] = nb

    def spare_block(j, c):
        be_ref[j] = MOE_EXPERTS - 1
        bc_ref[j] = 0
        return c

    lax.fori_loop(nb, n_blk, spare_block, 0)


def _chunk_copy(src_ref, src_row, dst_ref, dst_row, sem):
    return pltpu.make_async_copy(src_ref.at[pl.ds(pl.multiple_of(src_row, SEG_Q), SEG_Q), :],
                                 dst_ref.at[pl.ds(pl.multiple_of(dst_row, SEG_Q), SEG_Q), :], sem)


def _expert_kernel(src_ref, be_ref, bc_ref, nb_ref, wg_ref, wu_ref, wd_ref, xs_ref, ys_ref,
                   wgu_s, wd_s, xg, yg, sem_in, sem_out):
    b = pl.program_id(0)
    nb = nb_ref[0]

    def gather(blk, slot):
        def body(c, carry):
            _chunk_copy(xs_ref, src_ref[blk * EXP_CPB + c], xg.at[slot], c * SEG_Q, sem_in.at[slot]).start()
            return carry
        lax.fori_loop(0, bc_ref[blk], body, 0)

    def wait_gather(blk, slot):
        def body(c, carry):
            _chunk_copy(xs_ref, 0, xg.at[slot], 0, sem_in.at[slot]).wait()
            return carry
        lax.fori_loop(0, bc_ref[blk], body, 0)

    def scatter(blk, slot):
        def body(c, carry):
            _chunk_copy(yg.at[slot], c * SEG_Q, ys_ref, src_ref[blk * EXP_CPB + c], sem_out.at[slot]).start()
            return carry
        lax.fori_loop(0, bc_ref[blk], body, 0)

    def wait_scatter(blk, slot):
        def body(c, carry):
            _chunk_copy(yg.at[slot], 0, ys_ref, 0, sem_out.at[slot]).wait()
            return carry
        lax.fori_loop(0, bc_ref[blk], body, 0)

    @pl.when(b == 0)
    def _():
        xg[...] = jnp.zeros_like(xg)
        gather(0, 0)

    @pl.when(b + 1 < nb)
    def _():
        gather(b + 1, (b + 1) & 1)

    @pl.when((b == 0) | (be_ref[b] != be_ref[jnp.maximum(b - 1, 0)]))
    def _():
        wgu_s[:, 0:MOE_HIDDEN] = wg_ref[...].astype(BF16)
        wgu_s[:, MOE_HIDDEN:2 * MOE_HIDDEN] = wu_ref[...].astype(BF16)
        wd_s[...] = wd_ref[...].astype(BF16)

    @pl.when(b < nb)
    def _():
        slot = b & 1
        wait_gather(b, slot)
        gu = _dot(xg[slot], wgu_s[...])
        g = gu[:, 0:MOE_HIDDEN]
        u = gu[:, MOE_HIDDEN:2 * MOE_HIDDEN]
        y = _dot((g * _sigmoid(g) * u).astype(BF16), wd_s[...])

        @pl.when(b >= 2)
        def _():
            wait_scatter(b - 2, slot)

        yg[slot] = y.astype(BF16)
        scatter(b, slot)

    @pl.when(b == pl.num_programs(0) - 1)
    def _():
        @pl.when(nb >= 2)
        def _():
            wait_scatter(nb - 2, nb & 1)
        wait_scatter(nb - 1, (nb - 1) & 1)


def _combine_kernel(ys_ref, ws_ref, pos_ref, x1_ref, mod_ref, lng_ref, lnb_ref, x2_ref):
    n, nr = SORT_T, SORT_R
    ysw = (ys_ref[...].astype(F32) * ws_ref[:, 0:1]).astype(BF16)
    pos = pos_ref[...]
    ci = lax.broadcasted_iota(I32, (n, nr), 1).astype(F32)
    unperm = jnp.where((ci == pos[:, 0:1]) | (ci == pos[:, 1:2]), 1.0, 0.0).astype(BF16)
    y = _dot(unperm, ysw)
    g2 = mod_ref[5:6, :]
    x2_ref[...] = _layer_norm(ALPHA * x1_ref[...] + (1.0 + g2) * y, lng_ref[...], lnb_ref[...])


def _moe_and_norm(x1, h2, route, mod_l, lng, lnb, wg, wu, wd, layer):
    bsz, s, d = x1.shape
    n_tok = bsz * s
    nt = n_tok // SORT_T
    n_rows = nt * SORT_R
    x1f = x1.reshape(n_tok, d)
    h2f = h2.reshape(n_tok, d)
    rt = route.reshape(n_tok, LANES)

    xs, ws, pos, cnt = pl.pallas_call(
        _moe_sort_kernel,
        grid=(nt,),
        in_specs=[pl.BlockSpec((SORT_T, d), lambda i: (i, 0)),
                  pl.BlockSpec((SORT_T, LANES), lambda i: (i, 0))],
        out_specs=[pl.BlockSpec((SORT_R, d), lambda i: (i, 0)),
                   pl.BlockSpec((SORT_R, LANES), lambda i: (i, 0)),
                   pl.BlockSpec((SORT_T, LANES), lambda i: (i, 0)),
                   pl.BlockSpec((None, 1, LANES), lambda i: (i, 0, 0))],
        out_shape=[jax.ShapeDtypeStruct((n_rows, d), BF16),
                   jax.ShapeDtypeStruct((n_rows, LANES), F32),
                   jax.ShapeDtypeStruct((n_tok, LANES), F32),
                   jax.ShapeDtypeStruct((nt, 1, LANES), F32)],
        compiler_params=_cparams(("parallel",)),
        name="moe_sort",
    )(h2f, rt)

    cnt_i = cnt[:, 0, :MOE_EXPERTS].astype(I32)
    pcnt = (cnt_i + SEG_Q - 1) // SEG_Q * SEG_Q
    seg = (jnp.cumsum(pcnt, axis=1) - pcnt).reshape(-1).astype(I32)
    nch = (pcnt // SEG_Q).reshape(-1).astype(I32)

    n_src = n_rows // SEG_Q + MOE_EXPERTS * EXP_CPB
    n_blk = n_src // EXP_CPB
    smem = pl.BlockSpec(memory_space=pltpu.SMEM)
    src, block_e, block_n, n_used = pl.pallas_call(
        functools.partial(_chunk_list_kernel, n_tiles=nt),
        in_specs=[smem, smem],
        out_specs=[smem, smem, smem, smem],
        out_shape=[jax.ShapeDtypeStruct((n_src,), I32),
                   jax.ShapeDtypeStruct((n_blk,), I32),
                   jax.ShapeDtypeStruct((n_blk,), I32),
                   jax.ShapeDtypeStruct((1,), I32)],
        name="moe_chunk_list",
    )(seg, nch)

    ys = pl.pallas_call(
        _expert_kernel,
        grid_spec=pltpu.PrefetchScalarGridSpec(
            num_scalar_prefetch=4,
            grid=(n_blk,),
            in_specs=[pl.BlockSpec((None, None, d, MOE_HIDDEN), lambda b, src, be, bc, nb: (layer, be[b], 0, 0)),
                      pl.BlockSpec((None, None, d, MOE_HIDDEN), lambda b, src, be, bc, nb: (layer, be[b], 0, 0)),
                      pl.BlockSpec((None, None, MOE_HIDDEN, d), lambda b, src, be, bc, nb: (layer, be[b], 0, 0)),
                      pl.BlockSpec(memory_space=pl.ANY)],
            out_specs=pl.BlockSpec(memory_space=pl.ANY),
            scratch_shapes=[pltpu.VMEM((d, 2 * MOE_HIDDEN), BF16),
                            pltpu.VMEM((MOE_HIDDEN, d), BF16),
                            pltpu.VMEM((2, EXP_BM, d), BF16),
                            pltpu.VMEM((2, EXP_BM, d), BF16),
                            pltpu.SemaphoreType.DMA((2,)),
                            pltpu.SemaphoreType.DMA((2,))]),
        out_shape=jax.ShapeDtypeStruct((n_rows, d), BF16),
        input_output_aliases={7: 0},
        compiler_params=_cparams(("arbitrary",)),
        name="moe_experts",
    )(src, block_e, block_n, n_used, wg, wu, wd, xs)

    per_b = s // SORT_T
    x2 = pl.pallas_call(
        _combine_kernel,
        grid=(nt,),
        in_specs=[pl.BlockSpec((SORT_R, d), lambda i: (i, 0)),
                  pl.BlockSpec((SORT_R, LANES), lambda i: (i, 0)),
                  pl.BlockSpec((SORT_T, LANES), lambda i: (i, 0)),
                  pl.BlockSpec((SORT_T, d), lambda i: (i, 0)),
                  pl.BlockSpec((None, 6, d), lambda i: (i // per_b, 0, 0)),
                  pl.BlockSpec((1, d), lambda i: (0, 0)),
                  pl.BlockSpec((1, d), lambda i: (0, 0))],
        out_specs=pl.BlockSpec((SORT_T, d), lambda i: (i, 0)),
        out_shape=jax.ShapeDtypeStruct((n_tok, d), F32),
        compiler_params=_cparams(("parallel",)),
        name="moe_combine_norm",
    )(ys, ws, pos, x1f, mod_l, lng, lnb)
    return x2.reshape(bsz, s, d)


def _rope_table_kernel(pos_ref, inv_ref, sgn_ref, cos_ref, sin_ref):
    ang = pos_ref[...] * inv_ref[...]
    cos_ref[...] = jnp.cos(ang)
    sin_ref[...] = jnp.sin(ang) * sgn_ref[...]


def _rope_tables(positions):
    bsz, s = positions.shape
    half = B_HEAD_DIM // 2
    inv = jnp.power(jnp.float32(ROPE_THETA), -jnp.arange(half, dtype=F32) * 2.0 / B_HEAD_DIM)
    inv_row = jnp.tile(inv, LANES // half).reshape(1, LANES)
    sgn_row = jnp.tile(jnp.concatenate([-jnp.ones((half,), F32), jnp.ones((half,), F32)]),
                       LANES // B_HEAD_DIM).reshape(1, LANES)
    pos = jnp.broadcast_to(positions.astype(F32)[:, :, None], (bsz, s, LANES))
    tok = pl.BlockSpec((None, ROW_T, LANES), lambda b, i: (b, i, 0))
    row = pl.BlockSpec((1, LANES), lambda b, i: (0, 0))
    return pl.pallas_call(
        _rope_table_kernel,
        grid=(bsz, s // ROW_T),
        in_specs=[tok, row, row],
        out_specs=[tok, tok],
        out_shape=[jax.ShapeDtypeStruct((bsz, s, LANES), F32)] * 2,
        compiler_params=_cparams(("parallel", "parallel")),
        name="rope_tables",
    )(pos, inv_row, sgn_row)


def _rope(t, cosf, sinf, first_half):
    n = t.shape[1]
    half = B_HEAD_DIM // 2
    fwd = pltpu.roll(t, n - half, 1)
    bwd = pltpu.roll(t, half, 1)
    return t * cosf + jnp.where(first_half, fwd, bwd) * sinf


def _rope_operands(cos_ref, sin_ref, rows):
    reps = B_WIDTH // LANES
    cosf = jnp.concatenate([cos_ref[...]] * reps, axis=1)
    sinf = jnp.concatenate([sin_ref[...]] * reps, axis=1)
    lane = lax.broadcasted_iota(I32, (rows, B_WIDTH), 1)
    first_half = (lane & (B_HEAD_DIM - 1)) < (B_HEAD_DIM // 2)
    return cosf, sinf, first_half


def _store_by_residue(val, out_ref, stage_ref, dil):
    rows, width = val.shape
    if dil == 1:
        out_ref[0] = val.astype(out_ref.dtype)
        return
    for c in range(width // LANES):
        stage_ref[c] = val[:, c * LANES:(c + 1) * LANES]
    for r in range(dil):
        for c in range(width // LANES):
            out_ref[r, :, c * LANES:(c + 1) * LANES] = (
                stage_ref[c, pl.ds(r, rows // dil, stride=dil), :].astype(out_ref.dtype))


def _load_by_residue(in_ref, stage_ref, dil):
    _, sub, width = in_ref.shape
    if dil == 1:
        return in_ref[0].astype(F32)
    for r in range(dil):
        for c in range(width // LANES):
            stage_ref[c, pl.ds(r, sub, stride=dil), :] = in_ref[r, :, c * LANES:(c + 1) * LANES].astype(F32)
    return jnp.concatenate([stage_ref[c] for c in range(width // LANES)], axis=1)


def _class_spec(dil, width):
    return pl.BlockSpec((None, dil, ROW_T // dil, width), lambda b, i: (b, 0, i, 0))


def _class_shape(bsz, s, dil, width, dtype):
    return jax.ShapeDtypeStruct((bsz, dil, s // dil, width), dtype)


def _kv_kernel(x_ref, w_ref, cos_ref, sin_ref, k0, k1, k2, v0, v1, v2, stage_ref):
    xb = x_ref[...].astype(BF16)
    cosf, sinf, first_half = _rope_operands(cos_ref, sin_ref, ROW_T)
    for g, (k_ref, v_ref) in enumerate(((k0, v0), (k1, v1), (k2, v2))):
        kg = _dot(xb, w_ref[:, g * B_WIDTH:(g + 1) * B_WIDTH])
        _store_by_residue(_rope(kg, cosf, sinf, first_half), k_ref, stage_ref, B_DILATIONS[g])
        vo = (B_GROUPS + g) * B_WIDTH
        _store_by_residue(_dot(xb, w_ref[:, vo:vo + B_WIDTH]), v_ref, stage_ref, B_DILATIONS[g])


def _kv_project(x, wkv, cos_t, sin_t):
    bsz, s, d = x.shape
    tok = lambda w: pl.BlockSpec((None, ROW_T, w), lambda b, i: (b, i, 0))
    dils = B_DILATIONS * 2
    return pl.pallas_call(
        _kv_kernel,
        grid=(bsz, s // ROW_T),
        in_specs=[tok(d), pl.BlockSpec(wkv.shape, lambda b, i: (0, 0)), tok(LANES), tok(LANES)],
        out_specs=[_class_spec(dl, B_WIDTH) for dl in dils],
        out_shape=[_class_shape(bsz, s, dl, B_WIDTH, BF16) for dl in dils],
        scratch_shapes=[pltpu.VMEM((B_WIDTH // LANES, ROW_T, LANES), F32)],
        compiler_params=_cparams(("parallel", "parallel")),
        name="kv_project",
    )(x, wkv, cos_t, sin_t)


def _q_kernel(x_ref, mod_ref, w_ref, cos_ref, sin_ref, q0, q1, q2, stage_ref):
    md = mod_ref[...]
    hb = (x_ref[...] * (1.0 + md[1:2, :]) + md[0:1, :]).astype(BF16)
    cosf, sinf, first_half = _rope_operands(cos_ref, sin_ref, ROW_T)
    for g, q_ref in enumerate((q0, q1, q2)):
        qg = _dot(hb, w_ref[:, g * B_WIDTH:(g + 1) * B_WIDTH])
        _store_by_residue(_rope(qg, cosf, sinf, first_half) * (B_HEAD_DIM ** -0.5), q_ref, stage_ref,
                          B_DILATIONS[g])


def _q_project(x, mod_l, wq, cos_t, sin_t):
    bsz, s, d = x.shape
    tok = lambda w: pl.BlockSpec((None, ROW_T, w), lambda b, i: (b, i, 0))
    return pl.pallas_call(
        _q_kernel,
        grid=(bsz, s // ROW_T),
        in_specs=[tok(d), pl.BlockSpec((None, 6, d), lambda b, i: (b, 0, 0)),
                  pl.BlockSpec(wq.shape, lambda b, i: (0, 0)), tok(LANES), tok(LANES)],
        out_specs=[_class_spec(dl, B_WIDTH) for dl in B_DILATIONS],
        out_shape=[_class_shape(bsz, s, dl, B_WIDTH, BF16) for dl in B_DILATIONS],
        scratch_shapes=[pltpu.VMEM((B_WIDTH // LANES, ROW_T, LANES), F32)],
        compiler_params=_cparams(("parallel", "parallel")),
        name="q_project",
    )(x, mod_l, wq, cos_t, sin_t)


def _attn_kernel(q_ref, kc_ref, kp_ref, vc_ref, vp_ref, o_ref, lse_ref, kw_ref, vw_ref, *, nsub):
    wb = B_WB
    first_step = pl.program_id(2) == 0
    kw_ref[0:wb, :] = kp_ref[...]
    kw_ref[wb:, :] = kc_ref[...]
    vw_ref[0:wb, :] = vp_ref[...]
    vw_ref[wb:, :] = vc_ref[...]
    qi = lax.broadcasted_iota(I32, (wb, 2 * wb), 0)
    kj = lax.broadcasted_iota(I32, (wb, 2 * wb), 1)
    band = (kj >= qi) & (kj <= qi + wb)
    head_lane = lax.broadcasted_iota(I32, (wb, LANES), 1)

    def sub_block(j, carry):
        r0 = pl.multiple_of(j * wb, wb)
        valid = band & ((kj >= wb) | (j > 0) | jnp.logical_not(first_step))
        q = q_ref[pl.ds(r0, wb), :]
        kwin = kw_ref[pl.ds(r0, 2 * wb), :]
        vwin = vw_ref[pl.ds(r0, 2 * wb), :]
        outs = []
        lse = jnp.zeros((wb, LANES), F32)
        for h in range(B_HEADS):
            sl = slice(h * B_HEAD_DIM, (h + 1) * B_HEAD_DIM)
            sc = jnp.where(valid, _dot_nt(q[:, sl], kwin[:, sl]), -jnp.inf)
            m = jnp.max(sc, axis=-1, keepdims=True)
            p = jnp.exp(sc - m)
            l = jnp.sum(p, axis=-1, keepdims=True)
            outs.append(_dot(p.astype(BF16), vwin[:, sl]) * (1.0 / l))
            lse = jnp.where(head_lane == h, m + jnp.log(l), lse)
        o_ref[pl.ds(r0, wb), :] = jnp.concatenate(outs, axis=1).astype(BF16)
        lse_ref[pl.ds(r0, wb), :] = lse
        return carry

    lax.fori_loop(0, nsub, sub_block, 0)


def _dilated_attention(q, k, v, nsub):
    bsz, dil, rows, w = q.shape
    blk = B_WB * nsub
    cur = lambda width: pl.BlockSpec((None, None, blk, width), lambda b, r, n: (b, r, n, 0))
    prev = pl.BlockSpec((None, None, B_WB, w), lambda b, r, n: (b, r, jnp.maximum(n * nsub - 1, 0), 0))
    return pl.pallas_call(
        functools.partial(_attn_kernel, nsub=nsub),
        grid=(bsz, dil, rows // blk),
        in_specs=[cur(w), cur(w), prev, cur(w), prev],
        out_specs=[cur(w), cur(LANES)],
        out_shape=[jax.ShapeDtypeStruct((bsz, dil, rows, w), BF16),
                   jax.ShapeDtypeStruct((bsz, dil, rows, LANES), F32)],
        scratch_shapes=[pltpu.VMEM((blk + B_WB, w), BF16), pltpu.VMEM((blk + B_WB, w), BF16)],
        compiler_params=_cparams(("parallel", "parallel", "arbitrary")),
        name=f"dilated_attention_d{dil}",
    )(q, k, k, v, v)


def _attn_out_kernel(x_ref, mod_ref, o0, o1, o2, l0, l1, l2, wo_ref, lng_ref, lnb_ref, wr_ref, br_ref,
                     x1_ref, h2_ref, route_ref, stage_ref):
    md = mod_ref[...]
    g1, sh2, sc2 = md[2:3, :], md[3:4, :], md[4:5, :]
    lses = [_load_by_residue(l_ref, stage_ref, dl) for l_ref, dl in zip((l0, l1, l2), B_DILATIONS)]
    mx = jnp.maximum(jnp.maximum(lses[0], lses[1]), lses[2])
    es = [jnp.exp(l - mx) for l in lses]
    inv = 1.0 / (es[0] + es[1] + es[2])

    def per_head(wc):
        return jnp.concatenate([jnp.broadcast_to(wc[:, h:h + 1], (ROW_T, B_HEAD_DIM)) for h in range(B_HEADS)],
                               axis=1)

    o = jnp.zeros((ROW_T, B_WIDTH), F32)
    for o_ref, e, dl in zip((o0, o1, o2), es, B_DILATIONS):
        o = o + per_head(e * inv) * _load_by_residue(o_ref, stage_ref, dl)
    y = _dot(o.astype(BF16), wo_ref[...])
    x1 = _layer_norm(ALPHA * x_ref[...] + (1.0 + g1) * y, lng_ref[...], lnb_ref[...])
    x1_ref[...] = x1
    h2 = x1 * (1.0 + sc2) + sh2
    h2_ref[...] = h2
    route_ref[...] = _route_tile(h2, wr_ref, br_ref)


def _attn_out_layer(x, mod_l, outs, lses, wo, lng, lnb, wr, br):
    bsz, s, d = x.shape
    tok = lambda w: pl.BlockSpec((None, ROW_T, w), lambda b, i: (b, i, 0))
    const = lambda b, i: (0, 0)
    return pl.pallas_call(
        _attn_out_kernel,
        grid=(bsz, s // ROW_T),
        in_specs=[tok(d), pl.BlockSpec((None, 6, d), lambda b, i: (b, 0, 0))]
                 + [_class_spec(dl, B_WIDTH) for dl in B_DILATIONS]
                 + [_class_spec(dl, LANES) for dl in B_DILATIONS]
                 + [pl.BlockSpec((B_WIDTH, d), const), pl.BlockSpec((1, d), const), pl.BlockSpec((1, d), const),
                    pl.BlockSpec((d, LANES), const), pl.BlockSpec((1, LANES), const)],
        out_specs=[tok(d), tok(d), tok(LANES)],
        out_shape=[jax.ShapeDtypeStruct((bsz, s, d), F32),
                   jax.ShapeDtypeStruct((bsz, s, d), F32),
                   jax.ShapeDtypeStruct((bsz, s, LANES), F32)],
        scratch_shapes=[pltpu.VMEM((B_WIDTH // LANES, ROW_T, LANES), F32)],
        compiler_params=_cparams(("parallel", "parallel")),
        name="attn_out_layer",
    )(x, mod_l, *outs, *lses, wo, lng, lnb, wr, br)


def _router_weights(w_r1, b_r1, w_r2, b_r2):
    d = w_r1.shape[0]
    n = MOE_GROUPS + MOE_EXPERTS
    wr = jnp.zeros((d, LANES), F32).at[:, :MOE_GROUPS].set(w_r1).at[:, MOE_GROUPS:n].set(w_r2)
    br = jnp.zeros((1, LANES), F32).at[0, :MOE_GROUPS].set(b_r1).at[0, MOE_GROUPS:n].set(b_r2)
    return wr, br


def kernel(x, c, positions, ada_w, ada_b, ln_g, ln_b, a_w_in, a_b_gate, a_norm_g, a_w_out, b_w_kv, b_w_q, b_w_o,
           moe_w_r1, moe_b_r1, moe_w_r2, moe_b_r2, moe_w_gate, moe_w_up, moe_w_down):
    bsz, s, d = x.shape
    assert d == D_MODEL and s % (B_WB * B_DILATIONS[-1]) == 0 and s % ROW_T == 0 and s % A_CHUNK == 0
    assert (bsz * s) % SORT_T == 0 and s % SORT_T == 0
    assert 2 * SORT_T + MOE_EXPERTS * (SEG_Q - 1) <= SORT_R and SORT_R % SEG_Q == 0
    mod = _adaln_mod(c, ada_w, ada_b)
    cos_t, sin_t = _rope_tables(positions)
    kv = None
    for l in range(DEPTH):
        lng1, lnb1 = ln_g[l, 0].reshape(1, d), ln_b[l, 0].reshape(1, d)
        lng2, lnb2 = ln_g[l, 1].reshape(1, d), ln_b[l, 1].reshape(1, d)
        wr, br = _router_weights(moe_w_r1[l], moe_b_r1[l], moe_w_r2[l], moe_b_r2[l])
        if l < N_A:
            n_main = 2 * A_QK + 2 * A_V
            win = jnp.zeros((d, n_main + LANES), BF16)
            win = win.at[:, :n_main + 2 * A_HEADS].set(a_w_in[l].astype(BF16))
            bg = jnp.zeros((1, LANES), F32).at[0, :2 * A_HEADS].set(a_b_gate[l])
            x1, h2, route = _mlstm_layer(x, mod[l], win, bg, a_norm_g[l].reshape(1, A_V),
                                         a_w_out[l].astype(BF16), lng1, lnb1, wr, br)
        else:
            lb = l - N_A
            qs = _q_project(x, mod[l], b_w_q[lb].astype(BF16), cos_t, sin_t)
            outs, lses = [], []
            for g in range(B_GROUPS):
                nsub = min(ATT_J[g], s // (B_WB * B_DILATIONS[g]))
                o_g, l_g = _dilated_attention(qs[g], kv[g], kv[B_GROUPS + g], nsub)
                outs.append(o_g)
                lses.append(l_g)
            x1, h2, route = _attn_out_layer(x, mod[l], outs, lses, b_w_o[lb].astype(BF16), lng1, lnb1, wr, br)
        x = _moe_and_norm(x1, h2, route, mod[l], lng2, lnb2, moe_w_gate, moe_w_up, moe_w_down, l)
        if l == N_A - 1:
            kv = _kv_project(x, b_w_kv.astype(BF16), cos_t, sin_t)
    return x
```

```python
import functools

import jax
import jax.numpy as jnp
from jax import lax
from jax.experimental import pallas as pl
from jax.experimental.pallas import tpu as pltpu

F32 = jnp.float32
BF16 = jnp.bfloat16
I32 = jnp.int32
HI = lax.Precision.HIGHEST

D_MODEL = 1024
DEPTH = 4
N_A = DEPTH // 2
ALPHA = (2.0 * DEPTH) ** 0.25
LN_EPS = 1e-5
A_HEADS = 4
A_DQK = D_MODEL // 8
A_DV = D_MODEL // 4
A_NORM_EPS = 1e-6
A_QK = A_HEADS * A_DQK
A_V = A_HEADS * A_DV
B_WINDOWS = (128, 512, 2048)
B_DILATIONS = (1, 4, 16)
B_GROUPS = 3
B_HEADS = 8
B_HEAD_DIM = 64
B_WIDTH = B_HEADS * B_HEAD_DIM
B_WB = 128
ROPE_THETA = 10000.0
MOE_GROUPS = 4
MOE_EPG = 8
MOE_EXPERTS = MOE_GROUPS * MOE_EPG
MOE_HIDDEN = D_MODEL // 4

LANES = 128
SUBLANES = 8
VMEM_LIMIT = 56 * 1024 * 1024

A_CHUNK = 256
SORT_T = 512
SEG_Q = 16
SORT_R = 1536
EXP_BM = 512
EXP_CPB = EXP_BM // SEG_Q
ROW_T = 512
ATT_J = (8, 8, 4)
MOD_TN = 1536


def _cparams(sem):
    return pltpu.CompilerParams(dimension_semantics=sem, vmem_limit_bytes=VMEM_LIMIT)


def _dot(a, b):
    return jnp.dot(a, b, preferred_element_type=F32)


def _dot_nt(a, b):
    return lax.dot_general(a, b, (((1,), (1,)), ((), ())), preferred_element_type=F32)


def _dot_tn(a, b):
    return lax.dot_general(a, b, (((0,), (0,)), ((), ())), preferred_element_type=F32)


def _layer_norm(v, g, b):
    mu = jnp.mean(v, axis=-1, keepdims=True)
    vc = v - mu
    var = jnp.mean(vc * vc, axis=-1, keepdims=True)
    return vc * lax.rsqrt(var + LN_EPS) * g + b


def _sigmoid(v):
    return 1.0 / (1.0 + jnp.exp(-v))


def _split_hi_lo(v):
    hi = v.astype(BF16)
    return hi, (v - hi.astype(F32)).astype(BF16)


def _route_tile(h2, wr_ref, br_ref):
    h_hi, h_lo = _split_hi_lo(h2)
    lg = _dot(h_hi, wr_ref[0]) + (_dot(h_hi, wr_ref[1]) + _dot(h_lo, wr_ref[0])) + br_ref[...]
    lane = lax.broadcasted_iota(I32, lg.shape, 1).astype(F32)
    neg = -jnp.inf
    big = 1000.0
    m1 = jnp.where(lane < MOE_GROUPS, lg, neg)
    mx = jnp.max(m1, axis=-1, keepdims=True)
    pg = 1.0 / jnp.sum(jnp.exp(m1 - mx), axis=-1, keepdims=True)
    gi = jnp.min(jnp.where(m1 == mx, lane, big), axis=-1, keepdims=True)
    lo = MOE_GROUPS + gi * MOE_EPG
    m2 = jnp.where((lane >= lo) & (lane < lo + MOE_EPG), lg, neg)
    v0 = jnp.max(m2, axis=-1, keepdims=True)
    j0 = jnp.min(jnp.where(m2 == v0, lane, big), axis=-1, keepdims=True)
    m3 = jnp.where(lane == j0, neg, m2)
    v1 = jnp.max(m3, axis=-1, keepdims=True)
    j1 = jnp.min(jnp.where(m3 == v1, lane, big), axis=-1, keepdims=True)
    t = jnp.exp(v1 - v0)
    wa = 1.0 / (1.0 + t)
    wb = t * wa
    out = jnp.where(lane == 0, j0 - MOE_GROUPS,
          jnp.where(lane == 1, j1 - MOE_GROUPS,
          jnp.where(lane == 2, pg * wa,
          jnp.where(lane == 3, pg * wb, 0.0))))
    return out


def _mod_kernel(c_ref, w_ref, b_ref, o_ref):
    c = c_ref[...]
    cond = c * _sigmoid(c)
    o_ref[0] = jnp.dot(cond, w_ref[0], preferred_element_type=F32, precision=HI) + b_ref[0]


def _adaln_mod(c, ada_w, ada_b):
    bsz, d = c.shape
    depth, _, n6 = ada_w.shape
    rows = 8
    c_pad = jnp.zeros((rows, d), F32).at[:bsz].set(c)
    out = pl.pallas_call(
        _mod_kernel,
        grid=(depth, n6 // MOD_TN),
        in_specs=[pl.BlockSpec((rows, d), lambda l, j: (0, 0)),
                  pl.BlockSpec((1, d, MOD_TN), lambda l, j: (l, 0, j)),
                  pl.BlockSpec((1, 1, MOD_TN), lambda l, j: (l, 0, j))],
        out_specs=pl.BlockSpec((1, rows, MOD_TN), lambda l, j: (l, 0, j)),
        out_shape=jax.ShapeDtypeStruct((depth, rows, n6), F32),
        compiler_params=_cparams(("parallel", "parallel")),
        name="adaln_mod",
    )(c_pad, ada_w, ada_b.reshape(depth, 1, n6))
    return out[:, :bsz].reshape(depth, bsz, 6, d)


def _mlstm_layer_kernel(x_ref, mod_ref, win_ref, bg_ref, ng_ref, wout_ref, lng_ref, lnb_ref,
                        wr_ref, br_ref, x1_ref, h2_ref, route_ref, c_ref, n_ref, m_ref):
    L = A_CHUNK

    @pl.when(pl.program_id(1) == 0)
    def _():
        c_ref[...] = jnp.zeros_like(c_ref)
        n_ref[...] = jnp.zeros_like(n_ref)
        m_ref[...] = jnp.zeros_like(m_ref)

    x = x_ref[...]
    md = mod_ref[...]
    sh1, sc1, g1, sh2, sc2, g2 = [md[j:j + 1, :] for j in range(6)]
    hb = (x * (1.0 + sc1) + sh1).astype(BF16)

    o_q, o_k, o_v, o_o, o_g = 0, A_QK, 2 * A_QK, 2 * A_QK + A_V, 2 * A_QK + 2 * A_V
    q_all = _dot(hb, win_ref[:, o_q:o_k])
    k_all = _dot(hb, win_ref[:, o_k:o_v]) * (A_DQK ** -0.5)
    v_all = _dot(hb, win_ref[:, o_v:o_o])
    og_all = _dot(hb, win_ref[:, o_o:o_g])
    gates = _dot(hb, win_ref[:, o_g:o_g + LANES]) + bg_ref[...]

    lf = jnp.minimum(gates, 0.0) - jnp.log(1.0 + jnp.exp(-jnp.abs(gates)))
    row = lax.broadcasted_iota(I32, (L, L), 0)
    col = lax.broadcasted_iota(I32, (L, L), 1)
    causal = row >= col
    tri = jnp.where(causal, 1.0, 0.0).astype(BF16)
    lf_hi, lf_lo = _split_hi_lo(lf)
    lf_lo2 = (lf - lf_hi.astype(F32) - lf_lo.astype(F32)).astype(BF16)
    b_col = _dot(tri, lf_hi) + (_dot(tri, lf_lo) + _dot(tri, lf_lo2))
    g_t = gates.T
    b_t = b_col.T

    acc = jnp.zeros((L, D_MODEL), F32)
    for h in range(A_HEADS):
        qh = q_all[:, h * A_DQK:(h + 1) * A_DQK]
        kh = k_all[:, h * A_DQK:(h + 1) * A_DQK]
        vh = v_all[:, h * A_DV:(h + 1) * A_DV]
        qb, kb, vb = qh.astype(BF16), kh.astype(BF16), vh.astype(BF16)
        fl = A_HEADS + h
        b_c = b_col[:, fl:fl + 1]
        i_c = gates[:, h:h + 1]
        b_r = b_t[fl:fl + 1, :]
        i_r = g_t[h:h + 1, :]
        m_st = m_ref[h:h + 1, 0:1]
        c_st = c_ref[h]
        n_st = n_ref[h:h + 1, :]

        dmat = jnp.where(causal, b_c - b_r + i_r, -jnp.inf)
        inter = b_c + m_st
        m_t = jnp.maximum(inter, jnp.max(dmat, axis=-1, keepdims=True))
        dw = jnp.exp(dmat - m_t)
        iw = jnp.exp(inter - m_t)
        sc = _dot_nt(qb, kb) * dw
        num = _dot(sc.astype(BF16), vb) + iw * _dot(qb, c_st.astype(BF16))
        den = jnp.sum(sc, axis=-1, keepdims=True) + iw * jnp.sum(qh * n_st, axis=-1, keepdims=True)
        hh = num * (1.0 / jnp.maximum(jnp.abs(den), jnp.exp(-m_t)))

        b_last = b_col[L - 1:L, fl:fl + 1]
        ws_log = b_last - b_c + i_c
        m_new = jnp.maximum(b_last + m_st, jnp.max(ws_log, axis=0, keepdims=True))
        ws = jnp.exp(ws_log - m_new)
        decay = jnp.exp(b_last + m_st - m_new)
        kw = kh * ws
        c_ref[h] = decay * c_st + _dot_tn(kw.astype(BF16), vb)
        n_ref[h:h + 1, :] = decay * n_st + jnp.sum(kw, axis=0, keepdims=True)
        m_ref[h:h + 1, :] = jnp.broadcast_to(m_new, (1, LANES))

        mu = jnp.mean(hh, axis=-1, keepdims=True)
        hc = hh - mu
        var = jnp.mean(hc * hc, axis=-1, keepdims=True)
        hn = hc * lax.rsqrt(var + A_NORM_EPS)
        og = og_all[:, h * A_DV:(h + 1) * A_DV]
        yh = hn * ng_ref[:, h * A_DV:(h + 1) * A_DV] * _sigmoid(og)
        acc = acc + _dot(yh.astype(BF16), wout_ref[h * A_DV:(h + 1) * A_DV, :])

    x1 = _layer_norm(ALPHA * x + (1.0 + g1) * acc, lng_ref[...], lnb_ref[...])
    x1_ref[...] = x1
    h2 = x1 * (1.0 + sc2) + sh2
    h2_ref[...] = h2
    route_ref[...] = _route_tile(h2, wr_ref, br_ref)


def _mlstm_layer(x, mod_l, win, bg, ng, wout, lng, lnb, wr, br):
    bsz, s, d = x.shape
    L = A_CHUNK
    nw = win.shape[1]
    const = lambda b, i: (0, 0)
    tok = pl.BlockSpec((None, L, d), lambda b, i: (b, i, 0))
    return pl.pallas_call(
        _mlstm_layer_kernel,
        grid=(bsz, s // L),
        in_specs=[tok,
                  pl.BlockSpec((None, 6, d), lambda b, i: (b, 0, 0)),
                  pl.BlockSpec((d, nw), const),
                  pl.BlockSpec((1, LANES), const),
                  pl.BlockSpec((1, A_V), const),
                  pl.BlockSpec((A_V, d), const),
                  pl.BlockSpec((1, d), const),
                  pl.BlockSpec((1, d), const),
                  pl.BlockSpec((2, d, LANES), lambda b, i: (0, 0, 0)),
                  pl.BlockSpec((1, LANES), const)],
        out_specs=[tok, tok, pl.BlockSpec((None, L, LANES), lambda b, i: (b, i, 0))],
        out_shape=[jax.ShapeDtypeStruct((bsz, s, d), F32),
                   jax.ShapeDtypeStruct((bsz, s, d), F32),
                   jax.ShapeDtypeStruct((bsz, s, LANES), F32)],
        scratch_shapes=[pltpu.VMEM((A_HEADS, A_DQK, A_DV), F32),
                        pltpu.VMEM((8, A_DQK), F32),
                        pltpu.VMEM((8, LANES), F32)],
        compiler_params=_cparams(("parallel", "arbitrary")),
        name="mlstm_layer",
    )(x, mod_l, win, bg, ng, wout, lng, lnb, wr, br)


def _moe_sort_kernel(h2_ref, route_ref, xs_ref, ws_ref, pos_ref, cnt_ref):
    n, nr = SORT_T, SORT_R
    r = route_ref[...]
    lane = lax.broadcasted_iota(I32, (n, LANES), 1).astype(F32)
    oh0 = lane == r[:, 0:1]
    oh1 = lane == r[:, 1:2]
    oh = jnp.where(oh0 | oh1, 1.0, 0.0)
    row = lax.broadcasted_iota(I32, (n, n), 0)
    col = lax.broadcasted_iota(I32, (n, n), 1)
    before = jnp.where(row > col, 1.0, 0.0).astype(BF16)
    excl = _dot(before, oh.astype(BF16))
    cnt = jnp.sum(oh, axis=0, keepdims=True)
    pcnt = jnp.ceil(cnt * (1.0 / SEG_Q)) * SEG_Q
    ej = lax.broadcasted_iota(I32, (LANES, LANES), 0)
    el = lax.broadcasted_iota(I32, (LANES, LANES), 1)
    upper = jnp.where(ej < el, 1.0, 0.0).astype(F32)
    seg = jnp.dot(jnp.broadcast_to(pcnt, (SUBLANES, LANES)), upper,
                  preferred_element_type=F32, precision=HI)[0:1, :]
    base = excl + seg
    p0 = jnp.sum(jnp.where(oh0, base, 0.0), axis=-1, keepdims=True)
    p1 = jnp.sum(jnp.where(oh1, base, 0.0), axis=-1, keepdims=True)
    pos = jnp.where(lane == 0, p0, jnp.where(lane == 1, p1, 0.0))
    pos_ref[...] = pos
    cnt_ref[...] = cnt
    pos_t = pos.T
    ri = lax.broadcasted_iota(I32, (nr, n), 0).astype(F32)
    sel0 = ri == pos_t[0:1, :]
    sel1 = ri == pos_t[1:2, :]
    perm = jnp.where(sel0 | sel1, 1.0, 0.0).astype(BF16)
    xs_ref[...] = _dot(perm, h2_ref[...].astype(BF16)).astype(BF16)

    def split3(w):
        hi = w.astype(BF16)
        r1 = w - hi.astype(F32)
        mid = r1.astype(BF16)
        lo = (r1 - mid.astype(F32)).astype(BF16)
        return jnp.where(lane == 0, hi.astype(F32),
               jnp.where(lane == 1, mid.astype(F32),
               jnp.where(lane == 2, lo.astype(F32), 0.0))).astype(BF16)

    wsum = (_dot(jnp.where(sel0, 1.0, 0.0).astype(BF16), split3(r[:, 2:3]))
            + _dot(jnp.where(sel1, 1.0, 0.0).astype(BF16), split3(r[:, 3:4])))
    ws = wsum[:, 0:1] + wsum[:, 1:2] + wsum[:, 2:3]
    ws_ref[...] = jnp.broadcast_to(ws, (nr, LANES))


def _chunk_list_kernel(seg_ref, nch_ref, src_ref, be_ref, bc_ref, nb_ref, *, n_tiles):
    n_src = src_ref.shape[0]
    n_blk = be_ref.shape[0]

    def clear_src(k, c):
        src_ref[k] = 0
        return c

    lax.fori_loop(0, n_src, clear_src, 0)

    def per_expert(e, carry):
        k0, nb0 = carry

        def per_tile(i, k):
            s = seg_ref[i * MOE_EXPERTS + e]
            n = nch_ref[i * MOE_EXPERTS + e]

            def per_chunk(c, kk):
                src_ref[kk] = i * SORT_R + s + c * SEG_Q
                return kk + 1

            return lax.fori_loop(0, n, per_chunk, k)

        k1 = lax.fori_loop(0, n_tiles, per_tile, k0)
        tot = k1 - k0
        nblk = (tot + EXP_CPB - 1) // EXP_CPB

        def per_block(j, c):
            be_ref[nb0 + j] = e
            bc_ref[nb0 + j] = jnp.minimum(EXP_CPB, tot - j * EXP_CPB)
            return c

        lax.fori_loop(0, nblk, per_block, 0)
        return k0 + nblk * EXP_CPB, nb0 + nblk

    _, nb = lax.fori_loop(0, MOE_EXPERTS, per_expert, (jnp.int32(0), jnp.int32(0)))
    nb_ref[0] = nb

    def spare_block(j, c):
        be_ref[j] = MOE_EXPERTS - 1
        bc_ref[j] = 0
        return c

    lax.fori_loop(nb, n_blk, spare_block, 0)


def _chunk_copy(src_ref, src_row, dst_ref, dst_row, sem):
    return pltpu.make_async_copy(src_ref.at[pl.ds(pl.multiple_of(src_row, SEG_Q), SEG_Q), :],
                                 dst_ref.at[pl.ds(pl.multiple_of(dst_row, SEG_Q), SEG_Q), :], sem)


def _expert_kernel(src_ref, be_ref, bc_ref, nb_ref, wg_ref, wu_ref, wd_ref, xs_ref, ys_ref,
                   wgu_s, wd_s, xg, yg, sem_in, sem_out):
    b = pl.program_id(0)
    nb = nb_ref[0]

    def for_chunk_pairs(blk, fn):
        n = bc_ref[blk]

        def body(c2, carry):
            for par in range(2):
                @pl.when(2 * c2 + par < n)
                def _(par=par):
                    fn(2 * c2 + par, par)
            return carry
        lax.fori_loop(0, (n + 1) // 2, body, 0)

    def gather(blk, slot):
        for_chunk_pairs(blk, lambda c, pr: _chunk_copy(
            xs_ref, src_ref[blk * EXP_CPB + c], xg.at[slot], c * SEG_Q, sem_in.at[slot]).start(priority=pr))

    def wait_gather(blk, slot):
        def body(c, carry):
            _chunk_copy(xs_ref, 0, xg.at[slot], 0, sem_in.at[slot]).wait()
            return carry
        lax.fori_loop(0, bc_ref[blk], body, 0)

    def scatter(blk, slot):
        for_chunk_pairs(blk, lambda c, pr: _chunk_copy(
            yg.at[slot], c * SEG_Q, ys_ref, src_ref[blk * EXP_CPB + c], sem_out.at[slot]).start(priority=pr))

    def wait_scatter(blk, slot):
        def body(c, carry):
            _chunk_copy(yg.at[slot], 0, ys_ref, 0, sem_out.at[slot]).wait()
            return carry
        lax.fori_loop(0, bc_ref[blk], body, 0)

    @pl.when(b == 0)
    def _():
        xg[...] = jnp.zeros_like(xg)
        gather(0, 0)

    @pl.when(b + 1 < nb)
    def _():
        gather(b + 1, (b + 1) & 1)

    @pl.when((b == 0) | (be_ref[b] != be_ref[jnp.maximum(b - 1, 0)]))
    def _():
        wgu_s[:, 0:MOE_HIDDEN] = wg_ref[...].astype(BF16)
        wgu_s[:, MOE_HIDDEN:2 * MOE_HIDDEN] = wu_ref[...].astype(BF16)
        wd_s[...] = wd_ref[...].astype(BF16)

    @pl.when(b < nb)
    def _():
        slot = b & 1
        wait_gather(b, slot)
        gu = _dot(xg[slot], wgu_s[...])
        g = gu[:, 0:MOE_HIDDEN]
        u = gu[:, MOE_HIDDEN:2 * MOE_HIDDEN]
        y = _dot((g * _sigmoid(g) * u).astype(BF16), wd_s[...])

        @pl.when(b >= 2)
        def _():
            wait_scatter(b - 2, slot)

        yg[slot] = y.astype(BF16)
        scatter(b, slot)

    @pl.when(b == pl.num_programs(0) - 1)
    def _():
        @pl.when(nb >= 2)
        def _():
            wait_scatter(nb - 2, nb & 1)
        wait_scatter(nb - 1, (nb - 1) & 1)


def _combine_kernel(ys_ref, ws_ref, pos_ref, x1_ref, mod_ref, lng_ref, lnb_ref, x2_ref):
    n, nr = SORT_T, SORT_R
    ysw = (ys_ref[...].astype(F32) * ws_ref[:, 0:1]).astype(BF16)
    pos = pos_ref[...]
    ci = lax.broadcasted_iota(I32, (n, nr), 1).astype(F32)
    unperm = jnp.where((ci == pos[:, 0:1]) | (ci == pos[:, 1:2]), 1.0, 0.0).astype(BF16)
    y = _dot(unperm, ysw)
    g2 = mod_ref[5:6, :]
    x2_ref[...] = _layer_norm(ALPHA * x1_ref[...] + (1.0 + g2) * y, lng_ref[...], lnb_ref[...])


def _moe_and_norm(x1, h2, route, mod_l, lng, lnb, wg, wu, wd, layer):
    bsz, s, d = x1.shape
    n_tok = bsz * s
    nt = n_tok // SORT_T
    n_rows = nt * SORT_R
    x1f = x1.reshape(n_tok, d)
    h2f = h2.reshape(n_tok, d)
    rt = route.reshape(n_tok, LANES)

    xs, ws, pos, cnt = pl.pallas_call(
        _moe_sort_kernel,
        grid=(nt,),
        in_specs=[pl.BlockSpec((SORT_T, d), lambda i: (i, 0)),
                  pl.BlockSpec((SORT_T, LANES), lambda i: (i, 0))],
        out_specs=[pl.BlockSpec((SORT_R, d), lambda i: (i, 0)),
                   pl.BlockSpec((SORT_R, LANES), lambda i: (i, 0)),
                   pl.BlockSpec((SORT_T, LANES), lambda i: (i, 0)),
                   pl.BlockSpec((None, 1, LANES), lambda i: (i, 0, 0))],
        out_shape=[jax.ShapeDtypeStruct((n_rows, d), BF16),
                   jax.ShapeDtypeStruct((n_rows, LANES), F32),
                   jax.ShapeDtypeStruct((n_tok, LANES), F32),
                   jax.ShapeDtypeStruct((nt, 1, LANES), F32)],
        compiler_params=_cparams(("parallel",)),
        name="moe_sort",
    )(h2f, rt)

    cnt_i = cnt[:, 0, :MOE_EXPERTS].astype(I32)
    pcnt = (cnt_i + SEG_Q - 1) // SEG_Q * SEG_Q
    seg = (jnp.cumsum(pcnt, axis=1) - pcnt).reshape(-1).astype(I32)
    nch = (pcnt // SEG_Q).reshape(-1).astype(I32)

    n_src = n_rows // SEG_Q + MOE_EXPERTS * EXP_CPB
    n_blk = n_src // EXP_CPB
    smem = pl.BlockSpec(memory_space=pltpu.SMEM)
    src, block_e, block_n, n_used = pl.pallas_call(
        functools.partial(_chunk_list_kernel, n_tiles=nt),
        in_specs=[smem, smem],
        out_specs=[smem, smem, smem, smem],
        out_shape=[jax.ShapeDtypeStruct((n_src,), I32),
                   jax.ShapeDtypeStruct((n_blk,), I32),
                   jax.ShapeDtypeStruct((n_blk,), I32),
                   jax.ShapeDtypeStruct((1,), I32)],
        name="moe_chunk_list",
    )(seg, nch)

    ys = pl.pallas_call(
        _expert_kernel,
        grid_spec=pltpu.PrefetchScalarGridSpec(
            num_scalar_prefetch=4,
            grid=(n_blk,),
            in_specs=[pl.BlockSpec((None, None, d, MOE_HIDDEN), lambda b, src, be, bc, nb: (layer, be[b], 0, 0)),
                      pl.BlockSpec((None, None, d, MOE_HIDDEN), lambda b, src, be, bc, nb: (layer, be[b], 0, 0)),
                      pl.BlockSpec((None, None, MOE_HIDDEN, d), lambda b, src, be, bc, nb: (layer, be[b], 0, 0)),
                      pl.BlockSpec(memory_space=pl.ANY)],
            out_specs=pl.BlockSpec(memory_space=pl.ANY),
            scratch_shapes=[pltpu.VMEM((d, 2 * MOE_HIDDEN), BF16),
                            pltpu.VMEM((MOE_HIDDEN, d), BF16),
                            pltpu.VMEM((2, EXP_BM, d), BF16),
                            pltpu.VMEM((2, EXP_BM, d), BF16),
                            pltpu.SemaphoreType.DMA((2,)),
                            pltpu.SemaphoreType.DMA((2,))]),
        out_shape=jax.ShapeDtypeStruct((n_rows, d), BF16),
        input_output_aliases={7: 0},
        compiler_params=_cparams(("arbitrary",)),
        name="moe_experts",
    )(src, block_e, block_n, n_used, wg, wu, wd, xs)

    per_b = s // SORT_T
    x2 = pl.pallas_call(
        _combine_kernel,
        grid=(nt,),
        in_specs=[pl.BlockSpec((SORT_R, d), lambda i: (i, 0)),
                  pl.BlockSpec((SORT_R, LANES), lambda i: (i, 0)),
                  pl.BlockSpec((SORT_T, LANES), lambda i: (i, 0)),
                  pl.BlockSpec((SORT_T, d), lambda i: (i, 0)),
                  pl.BlockSpec((None, 6, d), lambda i: (i // per_b, 0, 0)),
                  pl.BlockSpec((1, d), lambda i: (0, 0)),
                  pl.BlockSpec((1, d), lambda i: (0, 0))],
        out_specs=pl.BlockSpec((SORT_T, d), lambda i: (i, 0)),
        out_shape=jax.ShapeDtypeStruct((n_tok, d), F32),
        compiler_params=_cparams(("parallel",)),
        name="moe_combine_norm",
    )(ys, ws, pos, x1f, mod_l, lng, lnb)
    return x2.reshape(bsz, s, d)


def _rope_table_kernel(pos_ref, inv_ref, sgn_ref, cos_ref, sin_ref):
    ang = pos_ref[...] * inv_ref[...]
    cos_ref[...] = jnp.cos(ang)
    sin_ref[...] = jnp.sin(ang) * sgn_ref[...]


def _rope_tables(positions):
    bsz, s = positions.shape
    half = B_HEAD_DIM // 2
    inv = jnp.power(jnp.float32(ROPE_THETA), -jnp.arange(half, dtype=F32) * 2.0 / B_HEAD_DIM)
    inv_row = jnp.tile(inv, LANES // half).reshape(1, LANES)
    sgn_row = jnp.tile(jnp.concatenate([-jnp.ones((half,), F32), jnp.ones((half,), F32)]),
                       LANES // B_HEAD_DIM).reshape(1, LANES)
    pos = jnp.broadcast_to(positions.astype(F32)[:, :, None], (bsz, s, LANES))
    tok = pl.BlockSpec((None, ROW_T, LANES), lambda b, i: (b, i, 0))
    row = pl.BlockSpec((1, LANES), lambda b, i: (0, 0))
    return pl.pallas_call(
        _rope_table_kernel,
        grid=(bsz, s // ROW_T),
        in_specs=[tok, row, row],
        out_specs=[tok, tok],
        out_shape=[jax.ShapeDtypeStruct((bsz, s, LANES), F32)] * 2,
        compiler_params=_cparams(("parallel", "parallel")),
        name="rope_tables",
    )(pos, inv_row, sgn_row)


def _rope(t, cosf, sinf, first_half):
    n = t.shape[1]
    half = B_HEAD_DIM // 2
    fwd = pltpu.roll(t, n - half, 1)
    bwd = pltpu.roll(t, half, 1)
    return t * cosf + jnp.where(first_half, fwd, bwd) * sinf


def _rope_operands(cos_ref, sin_ref, rows):
    reps = B_WIDTH // LANES
    cosf = jnp.concatenate([cos_ref[...]] * reps, axis=1)
    sinf = jnp.concatenate([sin_ref[...]] * reps, axis=1)
    lane = lax.broadcasted_iota(I32, (rows, B_WIDTH), 1)
    first_half = (lane & (B_HEAD_DIM - 1)) < (B_HEAD_DIM // 2)
    return cosf, sinf, first_half


def _store_by_residue(val, out_ref, stage_ref, dil):
    rows, width = val.shape
    if dil == 1:
        out_ref[0] = val.astype(out_ref.dtype)
        return
    for c in range(width // LANES):
        stage_ref[c] = val[:, c * LANES:(c + 1) * LANES]
    for r in range(dil):
        for c in range(width // LANES):
            out_ref[r, :, c * LANES:(c + 1) * LANES] = (
                stage_ref[c, pl.ds(r, rows // dil, stride=dil), :].astype(out_ref.dtype))


def _load_by_residue(in_ref, stage_ref, dil):
    _, sub, width = in_ref.shape
    if dil == 1:
        return in_ref[0].astype(F32)
    for r in range(dil):
        for c in range(width // LANES):
            stage_ref[c, pl.ds(r, sub, stride=dil), :] = in_ref[r, :, c * LANES:(c + 1) * LANES].astype(F32)
    return jnp.concatenate([stage_ref[c] for c in range(width // LANES)], axis=1)


def _class_spec(dil, width):
    return pl.BlockSpec((None, dil, ROW_T // dil, width), lambda b, i: (b, 0, i, 0))


def _class_shape(bsz, s, dil, width, dtype):
    return jax.ShapeDtypeStruct((bsz, dil, s // dil, width), dtype)


def _kv_kernel(x_ref, w_ref, cos_ref, sin_ref, k0, k1, k2, v0, v1, v2, stage_ref):
    xb = x_ref[...].astype(BF16)
    cosf, sinf, first_half = _rope_operands(cos_ref, sin_ref, ROW_T)
    for g, (k_ref, v_ref) in enumerate(((k0, v0), (k1, v1), (k2, v2))):
        kg = _dot(xb, w_ref[:, g * B_WIDTH:(g + 1) * B_WIDTH])
        _store_by_residue(_rope(kg, cosf, sinf, first_half), k_ref, stage_ref, B_DILATIONS[g])
        vo = (B_GROUPS + g) * B_WIDTH
        _store_by_residue(_dot(xb, w_ref[:, vo:vo + B_WIDTH]), v_ref, stage_ref, B_DILATIONS[g])


def _kv_project(x, wkv, cos_t, sin_t):
    bsz, s, d = x.shape
    tok = lambda w: pl.BlockSpec((None, ROW_T, w), lambda b, i: (b, i, 0))
    dils = B_DILATIONS * 2
    return pl.pallas_call(
        _kv_kernel,
        grid=(bsz, s // ROW_T),
        in_specs=[tok(d), pl.BlockSpec(wkv.shape, lambda b, i: (0, 0)), tok(LANES), tok(LANES)],
        out_specs=[_class_spec(dl, B_WIDTH) for dl in dils],
        out_shape=[_class_shape(bsz, s, dl, B_WIDTH, BF16) for dl in dils],
        scratch_shapes=[pltpu.VMEM((B_WIDTH // LANES, ROW_T, LANES), F32)],
        compiler_params=_cparams(("parallel", "parallel")),
        name="kv_project",
    )(x, wkv, cos_t, sin_t)


def _q_kernel(x_ref, mod_ref, w_ref, cos_ref, sin_ref, q0, q1, q2, stage_ref):
    md = mod_ref[...]
    hb = (x_ref[...] * (1.0 + md[1:2, :]) + md[0:1, :]).astype(BF16)
    cosf, sinf, first_half = _rope_operands(cos_ref, sin_ref, ROW_T)
    for g, q_ref in enumerate((q0, q1, q2)):
        qg = _dot(hb, w_ref[:, g * B_WIDTH:(g + 1) * B_WIDTH])
        _store_by_residue(_rope(qg, cosf, sinf, first_half) * (B_HEAD_DIM ** -0.5), q_ref, stage_ref,
                          B_DILATIONS[g])


def _q_project(x, mod_l, wq, cos_t, sin_t):
    bsz, s, d = x.shape
    tok = lambda w: pl.BlockSpec((None, ROW_T, w), lambda b, i: (b, i, 0))
    return pl.pallas_call(
        _q_kernel,
        grid=(bsz, s // ROW_T),
        in_specs=[tok(d), pl.BlockSpec((None, 6, d), lambda b, i: (b, 0, 0)),
                  pl.BlockSpec(wq.shape, lambda b, i: (0, 0)), tok(LANES), tok(LANES)],
        out_specs=[_class_spec(dl, B_WIDTH) for dl in B_DILATIONS],
        out_shape=[_class_shape(bsz, s, dl, B_WIDTH, BF16) for dl in B_DILATIONS],
        scratch_shapes=[pltpu.VMEM((B_WIDTH // LANES, ROW_T, LANES), F32)],
        compiler_params=_cparams(("parallel", "parallel")),
        name="q_project",
    )(x, mod_l, wq, cos_t, sin_t)


def _attn_kernel(q_ref, kc_ref, kp_ref, vc_ref, vp_ref, o_ref, lse_ref, kw_ref, vx_ref, *, nsub):
    wb = B_WB
    npair = B_WIDTH // LANES
    first_step = pl.program_id(2) == 0
    kw_ref[0:wb, :] = kp_ref[...]
    kw_ref[wb:, :] = kc_ref[...]
    for p in range(npair):
        vx_ref[p, 0:wb, 0:LANES] = vp_ref[:, p * LANES:(p + 1) * LANES]
        vx_ref[p, wb:, 0:LANES] = vc_ref[:, p * LANES:(p + 1) * LANES]
        vx_ref[p, :, LANES:2 * LANES] = jnp.ones((vx_ref.shape[1], LANES), BF16)
    qi = lax.broadcasted_iota(I32, (wb, 2 * wb), 0)
    kj = lax.broadcasted_iota(I32, (wb, 2 * wb), 1)
    band = (kj >= qi) & (kj <= qi + wb)
    lane = lax.broadcasted_iota(I32, (wb, LANES), 1)
    low = lane < B_HEAD_DIM

    def sub_block(j, carry):
        r0 = pl.multiple_of(j * wb, wb)
        valid = band & ((kj >= wb) | (j > 0) | jnp.logical_not(first_step))
        q = q_ref[pl.ds(r0, wb), :]
        kwin = kw_ref[pl.ds(r0, 2 * wb), :]
        outs = []
        m_all = jnp.zeros((wb, LANES), F32)
        l_all = jnp.ones((wb, LANES), F32)
        for p in range(npair):
            qp = q[:, p * LANES:(p + 1) * LANES]
            kpair = kwin[:, p * LANES:(p + 1) * LANES]
            vx = vx_ref[p, pl.ds(r0, 2 * wb), :]
            acc = lsum = None
            for hh in range(2):
                mine = low if hh == 0 else jnp.logical_not(low)
                qm = jnp.where(mine, qp, jnp.zeros_like(qp))
                sc = jnp.where(valid, _dot_nt(qm, kpair), -jnp.inf)
                m = jnp.max(sc, axis=-1, keepdims=True)
                pv = _dot(jnp.exp(sc - m).astype(BF16), vx)
                o_h, l_h = pv[:, 0:LANES], pv[:, LANES:2 * LANES]
                m_all = jnp.where(lane == 2 * p + hh, m, m_all)
                l_all = jnp.where(lane == 2 * p + hh, l_h, l_all)
                acc = o_h if hh == 0 else jnp.where(low, acc, o_h)
                lsum = l_h if hh == 0 else jnp.where(low, lsum, l_h)
            outs.append(acc * (1.0 / lsum))
        o_ref[pl.ds(r0, wb), :] = jnp.concatenate(outs, axis=1).astype(BF16)
        lse_ref[pl.ds(r0, wb), :] = m_all + jnp.log(l_all)
        return carry

    lax.fori_loop(0, nsub, sub_block, 0)


def _dilated_attention(q, k, v, nsub):
    bsz, dil, rows, w = q.shape
    blk = B_WB * nsub
    cur = lambda width: pl.BlockSpec((None, None, blk, width), lambda b, r, n: (b, r, n, 0))
    prev = pl.BlockSpec((None, None, B_WB, w), lambda b, r, n: (b, r, jnp.maximum(n * nsub - 1, 0), 0))
    return pl.pallas_call(
        functools.partial(_attn_kernel, nsub=nsub),
        grid=(bsz, dil, rows // blk),
        in_specs=[cur(w), cur(w), prev, cur(w), prev],
        out_specs=[cur(w), cur(LANES)],
        out_shape=[jax.ShapeDtypeStruct((bsz, dil, rows, w), BF16),
                   jax.ShapeDtypeStruct((bsz, dil, rows, LANES), F32)],
        scratch_shapes=[pltpu.VMEM((blk + B_WB, w), BF16),
                        pltpu.VMEM((w // LANES, blk + B_WB, 2 * LANES), BF16)],
        compiler_params=_cparams(("parallel", "parallel", "arbitrary")),
        name=f"dilated_attention_d{dil}",
    )(q, k, k, v, v)


def _attn_out_kernel(x_ref, mod_ref, o0, o1, o2, l0, l1, l2, wo_ref, lng_ref, lnb_ref, wr_ref, br_ref,
                     x1_ref, h2_ref, route_ref, stage_ref):
    md = mod_ref[...]
    g1, sh2, sc2 = md[2:3, :], md[3:4, :], md[4:5, :]
    lses = [_load_by_residue(l_ref, stage_ref, dl) for l_ref, dl in zip((l0, l1, l2), B_DILATIONS)]
    mx = jnp.maximum(jnp.maximum(lses[0], lses[1]), lses[2])
    es = [jnp.exp(l - mx) for l in lses]
    inv = 1.0 / (es[0] + es[1] + es[2])

    def per_head(wc):
        return jnp.concatenate([jnp.broadcast_to(wc[:, h:h + 1], (ROW_T, B_HEAD_DIM)) for h in range(B_HEADS)],
                               axis=1)

    o = jnp.zeros((ROW_T, B_WIDTH), F32)
    for o_ref, e, dl in zip((o0, o1, o2), es, B_DILATIONS):
        o = o + per_head(e * inv) * _load_by_residue(o_ref, stage_ref, dl)
    y = _dot(o.astype(BF16), wo_ref[...])
    x1 = _layer_norm(ALPHA * x_ref[...] + (1.0 + g1) * y, lng_ref[...], lnb_ref[...])
    x1_ref[...] = x1
    h2 = x1 * (1.0 + sc2) + sh2
    h2_ref[...] = h2
    route_ref[...] = _route_tile(h2, wr_ref, br_ref)


def _attn_out_layer(x, mod_l, outs, lses, wo, lng, lnb, wr, br):
    bsz, s, d = x.shape
    tok = lambda w: pl.BlockSpec((None, ROW_T, w), lambda b, i: (b, i, 0))
    const = lambda b, i: (0, 0)
    return pl.pallas_call(
        _attn_out_kernel,
        grid=(bsz, s // ROW_T),
        in_specs=[tok(d), pl.BlockSpec((None, 6, d), lambda b, i: (b, 0, 0))]
                 + [_class_spec(dl, B_WIDTH) for dl in B_DILATIONS]
                 + [_class_spec(dl, LANES) for dl in B_DILATIONS]
                 + [pl.BlockSpec((B_WIDTH, d), const), pl.BlockSpec((1, d), const), pl.BlockSpec((1, d), const),
                    pl.BlockSpec((2, d, LANES), lambda b, i: (0, 0, 0)), pl.BlockSpec((1, LANES), const)],
        out_specs=[tok(d), tok(d), tok(LANES)],
        out_shape=[jax.ShapeDtypeStruct((bsz, s, d), F32),
                   jax.ShapeDtypeStruct((bsz, s, d), F32),
                   jax.ShapeDtypeStruct((bsz, s, LANES), F32)],
        scratch_shapes=[pltpu.VMEM((B_WIDTH // LANES, ROW_T, LANES), F32)],
        compiler_params=_cparams(("parallel", "parallel")),
        name="attn_out_layer",
    )(x, mod_l, *outs, *lses, wo, lng, lnb, wr, br)


def _router_weights(w_r1, b_r1, w_r2, b_r2):
    d = w_r1.shape[0]
    n = MOE_GROUPS + MOE_EXPERTS
    wr = jnp.zeros((d, LANES), F32).at[:, :MOE_GROUPS].set(w_r1).at[:, MOE_GROUPS:n].set(w_r2)
    br = jnp.zeros((1, LANES), F32).at[0, :MOE_GROUPS].set(b_r1).at[0, MOE_GROUPS:n].set(b_r2)
    wr_hi = wr.astype(BF16)
    wr_lo = (wr - wr_hi.astype(F32)).astype(BF16)
    return jnp.stack([wr_hi, wr_lo]), br


def kernel(x, c, positions, ada_w, ada_b, ln_g, ln_b, a_w_in, a_b_gate, a_norm_g, a_w_out, b_w_kv, b_w_q, b_w_o,
           moe_w_r1, moe_b_r1, moe_w_r2, moe_b_r2, moe_w_gate, moe_w_up, moe_w_down):
    bsz, s, d = x.shape
    assert d == D_MODEL and s % (B_WB * B_DILATIONS[-1]) == 0 and s % ROW_T == 0 and s % A_CHUNK == 0
    assert (bsz * s) % SORT_T == 0 and s % SORT_T == 0
    assert 2 * SORT_T + MOE_EXPERTS * (SEG_Q - 1) <= SORT_R and SORT_R % SEG_Q == 0
    mod = _adaln_mod(c, ada_w, ada_b)
    cos_t, sin_t = _rope_tables(positions)
    kv = None
    for l in range(DEPTH):
        lng1, lnb1 = ln_g[l, 0].reshape(1, d), ln_b[l, 0].reshape(1, d)
        lng2, lnb2 = ln_g[l, 1].reshape(1, d), ln_b[l, 1].reshape(1, d)
        wr, br = _router_weights(moe_w_r1[l], moe_b_r1[l], moe_w_r2[l], moe_b_r2[l])
        if l < N_A:
            n_main = 2 * A_QK + 2 * A_V
            win = jnp.zeros((d, n_main + LANES), BF16)
            win = win.at[:, :n_main + 2 * A_HEADS].set(a_w_in[l].astype(BF16))
            bg = jnp.zeros((1, LANES), F32).at[0, :2 * A_HEADS].set(a_b_gate[l])
            x1, h2, route = _mlstm_layer(x, mod[l], win, bg, a_norm_g[l].reshape(1, A_V),
                                         a_w_out[l].astype(BF16), lng1, lnb1, wr, br)
        else:
            lb = l - N_A
            qs = _q_project(x, mod[l], b_w_q[lb].astype(BF16), cos_t, sin_t)
            outs, lses = [], []
            for g in range(B_GROUPS):
                nsub = min(ATT_J[g], s // (B_WB * B_DILATIONS[g]))
                o_g, l_g = _dilated_attention(qs[g], kv[g], kv[B_GROUPS + g], nsub)
                outs.append(o_g)
                lses.append(l_g)
            x1, h2, route = _attn_out_layer(x, mod[l], outs, lses, b_w_o[lb].astype(BF16), lng1, lnb1, wr, br)
        x = _moe_and_norm(x1, h2, route, mod[l], lng2, lnb2, moe_w_gate, moe_w_up, moe_w_down, l)
        if l == N_A - 1:
            kv = _kv_project(x, b_w_kv.astype(BF16), cos_t, sin_t)
    return x
```

```python
import functools

import jax
import jax.numpy as jnp
from jax import lax
from jax.experimental import pallas as pl
from jax.experimental.pallas import tpu as pltpu

F32 = jnp.float32
BF16 = jnp.bfloat16
I32 = jnp.int32
HI = lax.Precision.HIGHEST

D_MODEL = 1024
DEPTH = 4
N_A = DEPTH // 2
ALPHA = (2.0 * DEPTH) ** 0.25
LN_EPS = 1e-5
A_HEADS = 4
A_DQK = D_MODEL // 8
A_DV = D_MODEL // 4
A_NORM_EPS = 1e-6
A_QK = A_HEADS * A_DQK
A_V = A_HEADS * A_DV
B_WINDOWS = (128, 512, 2048)
B_DILATIONS = (1, 4, 16)
B_GROUPS = 3
B_HEADS = 8
B_HEAD_DIM = 64
B_WIDTH = B_HEADS * B_HEAD_DIM
B_WB = 128
ROPE_THETA = 10000.0
MOE_GROUPS = 4
MOE_EPG = 8
MOE_EXPERTS = MOE_GROUPS * MOE_EPG
MOE_HIDDEN = D_MODEL // 4

LANES = 128
SUBLANES = 8
VMEM_LIMIT = 56 * 1024 * 1024

A_CHUNK = 256
A_STEP = 512
SORT_T = 512
SEG_Q = 16
SORT_R = 1536
EXP_BM = 512
EXP_CPB = EXP_BM // SEG_Q
LIST_UNROLL = 4
ROW_T = 512
ATT_J = (8, 8, 4)
MOD_TN = 1536


def _cparams(sem):
    return pltpu.CompilerParams(dimension_semantics=sem, vmem_limit_bytes=VMEM_LIMIT)


def _dot(a, b):
    return jnp.dot(a, b, preferred_element_type=F32)


def _dot_nt(a, b):
    return lax.dot_general(a, b, (((1,), (1,)), ((), ())), preferred_element_type=F32)


def _dot_tn(a, b):
    return lax.dot_general(a, b, (((0,), (0,)), ((), ())), preferred_element_type=F32)


def _layer_norm(v, g, b):
    mu = jnp.mean(v, axis=-1, keepdims=True)
    vc = v - mu
    var = jnp.mean(vc * vc, axis=-1, keepdims=True)
    return vc * lax.rsqrt(var + LN_EPS) * g + b


def _sigmoid(v):
    return 1.0 / (1.0 + jnp.exp(-v))


def _split_hi_lo(v):
    hi = v.astype(BF16)
    return hi, (v - hi.astype(F32)).astype(BF16)


def _route_tile(h2, wr_ref, br_ref):
    h_hi, h_lo = _split_hi_lo(h2)
    lg = _dot(h_hi, wr_ref[0]) + (_dot(h_hi, wr_ref[1]) + _dot(h_lo, wr_ref[0])) + br_ref[...]
    lane = lax.broadcasted_iota(I32, lg.shape, 1).astype(F32)
    neg = -jnp.inf
    big = 1000.0
    m1 = jnp.where(lane < MOE_GROUPS, lg, neg)
    mx = jnp.max(m1, axis=-1, keepdims=True)
    pg = 1.0 / jnp.sum(jnp.exp(m1 - mx), axis=-1, keepdims=True)
    gi = jnp.min(jnp.where(m1 == mx, lane, big), axis=-1, keepdims=True)
    lo = MOE_GROUPS + gi * MOE_EPG
    m2 = jnp.where((lane >= lo) & (lane < lo + MOE_EPG), lg, neg)
    v0 = jnp.max(m2, axis=-1, keepdims=True)
    j0 = jnp.min(jnp.where(m2 == v0, lane, big), axis=-1, keepdims=True)
    m3 = jnp.where(lane == j0, neg, m2)
    v1 = jnp.max(m3, axis=-1, keepdims=True)
    j1 = jnp.min(jnp.where(m3 == v1, lane, big), axis=-1, keepdims=True)
    t = jnp.exp(v1 - v0)
    wa = 1.0 / (1.0 + t)
    wb = t * wa
    out = jnp.where(lane == 0, j0 - MOE_GROUPS,
          jnp.where(lane == 1, j1 - MOE_GROUPS,
          jnp.where(lane == 2, pg * wa,
          jnp.where(lane == 3, pg * wb, 0.0))))
    return out


def _mod_kernel(c_ref, w_ref, b_ref, o_ref):
    c = c_ref[...]
    cond = c * _sigmoid(c)
    o_ref[0] = jnp.dot(cond, w_ref[0], preferred_element_type=F32, precision=HI) + b_ref[0]


def _adaln_mod(c, ada_w, ada_b):
    bsz, d = c.shape
    depth, _, n6 = ada_w.shape
    rows = 8
    c_pad = jnp.zeros((rows, d), F32).at[:bsz].set(c)
    out = pl.pallas_call(
        _mod_kernel,
        grid=(depth, n6 // MOD_TN),
        in_specs=[pl.BlockSpec((rows, d), lambda l, j: (0, 0)),
                  pl.BlockSpec((1, d, MOD_TN), lambda l, j: (l, 0, j)),
                  pl.BlockSpec((1, 1, MOD_TN), lambda l, j: (l, 0, j))],
        out_specs=pl.BlockSpec((1, rows, MOD_TN), lambda l, j: (l, 0, j)),
        out_shape=jax.ShapeDtypeStruct((depth, rows, n6), F32),
        compiler_params=_cparams(("parallel", "parallel")),
        name="adaln_mod",
    )(c_pad, ada_w, ada_b.reshape(depth, 1, n6))
    return out[:, :bsz].reshape(depth, bsz, 6, d)


def _mlstm_layer_kernel(x_ref, mod_ref, win_ref, bg_ref, ng_ref, wout_ref, lng_ref, lnb_ref,
                        wr_ref, br_ref, x1_ref, h2_ref, route_ref, c_ref, n_ref, m_ref):
    L = A_CHUNK

    @pl.when(pl.program_id(1) == 0)
    def _():
        c_ref[...] = jnp.zeros_like(c_ref)
        n_ref[...] = jnp.zeros_like(n_ref)
        m_ref[...] = jnp.zeros_like(m_ref)

    x = x_ref[...]
    md = mod_ref[...]
    sh1, sc1, g1, sh2, sc2, g2 = [md[j:j + 1, :] for j in range(6)]
    hb = (x * (1.0 + sc1) + sh1).astype(BF16)

    o_q, o_k, o_v, o_o, o_g = 0, A_QK, 2 * A_QK, 2 * A_QK + A_V, 2 * A_QK + 2 * A_V
    q_all = _dot(hb, win_ref[:, o_q:o_k])
    k_all = _dot(hb, win_ref[:, o_k:o_v]) * (A_DQK ** -0.5)
    v_all = _dot(hb, win_ref[:, o_v:o_o])
    og_all = _dot(hb, win_ref[:, o_o:o_g])
    gates_all = _dot(hb, win_ref[:, o_g:o_g + LANES]) + bg_ref[...]

    row = lax.broadcasted_iota(I32, (L, L), 0)
    col = lax.broadcasted_iota(I32, (L, L), 1)
    causal = row >= col
    tri = jnp.where(causal, 1.0, 0.0).astype(BF16)

    def chunk(r0):
        rows = slice(r0, r0 + L)
        gates = gates_all[rows]
        lf = jnp.minimum(gates, 0.0) - jnp.log(1.0 + jnp.exp(-jnp.abs(gates)))
        lf_hi, lf_lo = _split_hi_lo(lf)
        lf_lo2 = (lf - lf_hi.astype(F32) - lf_lo.astype(F32)).astype(BF16)
        b_col = _dot(tri, lf_hi) + (_dot(tri, lf_lo) + _dot(tri, lf_lo2))
        g_t = gates.T
        b_t = b_col.T
        ys = []
        for h in range(A_HEADS):
            qh = q_all[rows, h * A_DQK:(h + 1) * A_DQK]
            kh = k_all[rows, h * A_DQK:(h + 1) * A_DQK]
            vh = v_all[rows, h * A_DV:(h + 1) * A_DV]
            qb, kb, vb = qh.astype(BF16), kh.astype(BF16), vh.astype(BF16)
            fl = A_HEADS + h
            b_c = b_col[:, fl:fl + 1]
            i_c = gates[:, h:h + 1]
            b_r = b_t[fl:fl + 1, :]
            i_r = g_t[h:h + 1, :]
            m_st = m_ref[h:h + 1, 0:1]
            c_st = c_ref[h]
            n_st = n_ref[h:h + 1, :]

            dmat = jnp.where(causal, b_c - b_r + i_r, -jnp.inf)
            inter = b_c + m_st
            m_t = jnp.maximum(inter, jnp.max(dmat, axis=-1, keepdims=True))
            dw = jnp.exp(dmat - m_t)
            iw = jnp.exp(inter - m_t)
            sc = _dot_nt(qb, kb) * dw
            num = _dot(sc.astype(BF16), vb) + iw * _dot(qb, c_st.astype(BF16))
            den = jnp.sum(sc, axis=-1, keepdims=True) + iw * jnp.sum(qh * n_st, axis=-1, keepdims=True)
            hh = num * (1.0 / jnp.maximum(jnp.abs(den), jnp.exp(-m_t)))

            b_last = b_col[L - 1:L, fl:fl + 1]
            ws_log = b_last - b_c + i_c
            m_new = jnp.maximum(b_last + m_st, jnp.max(ws_log, axis=0, keepdims=True))
            ws = jnp.exp(ws_log - m_new)
            decay = jnp.exp(b_last + m_st - m_new)
            kw = kh * ws
            c_ref[h] = decay * c_st + _dot_tn(kw.astype(BF16), vb)
            n_ref[h:h + 1, :] = decay * n_st + jnp.sum(kw, axis=0, keepdims=True)
            m_ref[h:h + 1, :] = jnp.broadcast_to(m_new, (1, LANES))

            mu = jnp.mean(hh, axis=-1, keepdims=True)
            hc = hh - mu
            var = jnp.mean(hc * hc, axis=-1, keepdims=True)
            hn = hc * lax.rsqrt(var + A_NORM_EPS)
            og = og_all[rows, h * A_DV:(h + 1) * A_DV]
            ys.append((hn * ng_ref[:, h * A_DV:(h + 1) * A_DV] * _sigmoid(og)).astype(BF16))
        return jnp.concatenate(ys, axis=1)

    yb = jnp.concatenate([chunk(r0) for r0 in range(0, A_STEP, L)], axis=0)
    x1 = _layer_norm(ALPHA * x + (1.0 + g1) * _dot(yb, wout_ref[...]), lng_ref[...], lnb_ref[...])
    x1_ref[...] = x1
    h2 = x1 * (1.0 + sc2) + sh2
    h2_ref[...] = h2
    route_ref[...] = _route_tile(h2, wr_ref, br_ref)


def _mlstm_layer(x, mod_l, win, bg, ng, wout, lng, lnb, wr, br):
    bsz, s, d = x.shape
    nw = win.shape[1]
    const = lambda b, i: (0, 0)
    tok = pl.BlockSpec((None, A_STEP, d), lambda b, i: (b, i, 0))
    return pl.pallas_call(
        _mlstm_layer_kernel,
        grid=(bsz, s // A_STEP),
        in_specs=[tok,
                  pl.BlockSpec((None, 6, d), lambda b, i: (b, 0, 0)),
                  pl.BlockSpec((d, nw), const),
                  pl.BlockSpec((1, LANES), const),
                  pl.BlockSpec((1, A_V), const),
                  pl.BlockSpec((A_V, d), const),
                  pl.BlockSpec((1, d), const),
                  pl.BlockSpec((1, d), const),
                  pl.BlockSpec((2, d, LANES), lambda b, i: (0, 0, 0)),
                  pl.BlockSpec((1, LANES), const)],
        out_specs=[tok, tok, pl.BlockSpec((None, A_STEP, LANES), lambda b, i: (b, i, 0))],
        out_shape=[jax.ShapeDtypeStruct((bsz, s, d), F32),
                   jax.ShapeDtypeStruct((bsz, s, d), F32),
                   jax.ShapeDtypeStruct((bsz, s, LANES), F32)],
        scratch_shapes=[pltpu.VMEM((A_HEADS, A_DQK, A_DV), F32),
                        pltpu.VMEM((8, A_DQK), F32),
                        pltpu.VMEM((8, LANES), F32)],
        compiler_params=_cparams(("parallel", "arbitrary")),
        name="mlstm_layer",
    )(x, mod_l, win, bg, ng, wout, lng, lnb, wr, br)


def _moe_sort_kernel(h2_ref, route_ref, xs_ref, ws_ref, pos_ref, cnt_ref):
    n, nr = SORT_T, SORT_R
    r = route_ref[...]
    lane = lax.broadcasted_iota(I32, (n, LANES), 1).astype(F32)
    oh0 = lane == r[:, 0:1]
    oh1 = lane == r[:, 1:2]
    oh = jnp.where(oh0 | oh1, 1.0, 0.0)
    row = lax.broadcasted_iota(I32, (n, n), 0)
    col = lax.broadcasted_iota(I32, (n, n), 1)
    before = jnp.where(row > col, 1.0, 0.0).astype(BF16)
    excl = _dot(before, oh.astype(BF16))
    cnt = jnp.sum(oh, axis=0, keepdims=True)
    pcnt = jnp.ceil(cnt * (1.0 / SEG_Q)) * SEG_Q
    ej = lax.broadcasted_iota(I32, (LANES, LANES), 0)
    el = lax.broadcasted_iota(I32, (LANES, LANES), 1)
    upper = jnp.where(ej < el, 1.0, 0.0).astype(F32)
    seg = jnp.dot(jnp.broadcast_to(pcnt, (SUBLANES, LANES)), upper,
                  preferred_element_type=F32, precision=HI)[0:1, :]
    base = excl + seg
    p0 = jnp.sum(jnp.where(oh0, base, 0.0), axis=-1, keepdims=True)
    p1 = jnp.sum(jnp.where(oh1, base, 0.0), axis=-1, keepdims=True)
    pos = jnp.where(lane == 0, p0, jnp.where(lane == 1, p1, 0.0))
    pos_ref[...] = pos
    cnt_ref[...] = cnt
    pos_t = pos.T
    ri = lax.broadcasted_iota(I32, (nr, n), 0).astype(F32)
    sel0 = ri == pos_t[0:1, :]
    sel1 = ri == pos_t[1:2, :]
    perm = jnp.where(sel0 | sel1, 1.0, 0.0).astype(BF16)
    xs_ref[...] = _dot(perm, h2_ref[...].astype(BF16)).astype(BF16)

    def split3(w):
        hi = w.astype(BF16)
        r1 = w - hi.astype(F32)
        mid = r1.astype(BF16)
        lo = (r1 - mid.astype(F32)).astype(BF16)
        return jnp.where(lane == 0, hi.astype(F32),
               jnp.where(lane == 1, mid.astype(F32),
               jnp.where(lane == 2, lo.astype(F32), 0.0))).astype(BF16)

    wsum = (_dot(jnp.where(sel0, 1.0, 0.0).astype(BF16), split3(r[:, 2:3]))
            + _dot(jnp.where(sel1, 1.0, 0.0).astype(BF16), split3(r[:, 3:4])))
    ws = wsum[:, 0:1] + wsum[:, 1:2] + wsum[:, 2:3]
    ws_ref[...] = jnp.broadcast_to(ws, (nr, LANES))


def _chunk_list_kernel(seg_ref, nch_ref, src_ref, be_ref, bc_ref, nb_ref, *, n_tiles):
    n_src = src_ref.shape[0]
    n_blk = be_ref.shape[0]

    def clear_src(k, c):
        src_ref[k] = 0
        return c

    lax.fori_loop(0, n_src, clear_src, 0, unroll=16)

    def per_expert(e, carry):
        k0, nb0 = carry

        def per_tile(i, k):
            s = seg_ref[i * MOE_EXPERTS + e]
            n = nch_ref[i * MOE_EXPERTS + e]

            row0 = i * SORT_R + s
            for c in range(LIST_UNROLL):
                @pl.when(c < n)
                def _(c=c):
                    src_ref[k + c] = row0 + c * SEG_Q

            def per_chunk(c, carry):
                src_ref[k + c] = row0 + c * SEG_Q
                return carry

            lax.fori_loop(LIST_UNROLL, n, per_chunk, 0)
            return k + n

        k1 = lax.fori_loop(0, n_tiles, per_tile, k0)
        tot = k1 - k0
        nblk = (tot + EXP_CPB - 1) // EXP_CPB

        def per_block(j, c):
            be_ref[nb0 + j] = e
            bc_ref[nb0 + j] = jnp.minimum(EXP_CPB, tot - j * EXP_CPB)
            return c

        lax.fori_loop(0, nblk, per_block, 0)
        return k0 + nblk * EXP_CPB, nb0 + nblk

    _, nb = lax.fori_loop(0, MOE_EXPERTS, per_expert, (jnp.int32(0), jnp.int32(0)))
    nb_ref[0] = nb

    def spare_block(j, c):
        be_ref[j] = MOE_EXPERTS - 1
        bc_ref[j] = 0
        return c

    lax.fori_loop(nb, n_blk, spare_block, 0)


def _chunk_copy(src_ref, src_row, dst_ref, dst_row, sem):
    return pltpu.make_async_copy(src_ref.at[pl.ds(pl.multiple_of(src_row, SEG_Q), SEG_Q), :],
                                 dst_ref.at[pl.ds(pl.multiple_of(dst_row, SEG_Q), SEG_Q), :], sem)


def _expert_kernel(src_ref, be_ref, bc_ref, nb_ref, wg_ref, wu_ref, wd_ref, xs_ref, ys_ref,
                   wgu_s, wd_s, xg, yg, sem_in, sem_out):
    b = pl.program_id(0)
    last = pl.num_programs(0) - 1
    nb = nb_ref[0]

    def clamp(blk):
        return jnp.clip(blk, 0, last)

    def count(blk):
        return jnp.where((blk >= 0) & (blk <= last), bc_ref[clamp(blk)], 0)

    def per_chunk(n, fn):
        for c in range(EXP_CPB):
            @pl.when(c < n)
            def _(c=c):
                fn(c)

    def gather(blk, slot):
        per_chunk(count(blk), lambda c: _chunk_copy(
            xs_ref, src_ref[clamp(blk) * EXP_CPB + c], xg.at[slot], c * SEG_Q, sem_in.at[slot]).start())

    def wait_gather(blk, slot):
        per_chunk(count(blk), lambda c: _chunk_copy(xs_ref, 0, xg.at[slot], 0, sem_in.at[slot]).wait())

    def scatter(blk, slot):
        per_chunk(count(blk), lambda c: _chunk_copy(
            yg.at[slot], c * SEG_Q, ys_ref, src_ref[clamp(blk) * EXP_CPB + c], sem_out.at[slot]).start())

    def wait_scatter(blk, slot):
        per_chunk(count(blk), lambda c: _chunk_copy(yg.at[slot], 0, ys_ref, 0, sem_out.at[slot]).wait())

    @pl.when(b == 0)
    def _():
        xg[...] = jnp.zeros_like(xg)
        gather(0, 0)

    @pl.when((b == 0) | (be_ref[b] != be_ref[jnp.maximum(b - 1, 0)]))
    def _():
        wgu_s[:, 0:MOE_HIDDEN] = wg_ref[...].astype(BF16)
        wgu_s[:, MOE_HIDDEN:2 * MOE_HIDDEN] = wu_ref[...].astype(BF16)
        wd_s[...] = wd_ref[...].astype(BF16)

    @pl.when(b < nb)
    def _():
        slot = b & 1
        gather(b + 1, 1 - slot)
        wait_gather(b, slot)
        gu = _dot(xg[slot], wgu_s[...])
        g = gu[:, 0:MOE_HIDDEN]
        u = gu[:, MOE_HIDDEN:2 * MOE_HIDDEN]
        y = _dot((g * _sigmoid(g) * u).astype(BF16), wd_s[...])
        wait_scatter(b - 2, slot)
        yg[slot] = y.astype(BF16)
        scatter(b, slot)

    @pl.when(b == last)
    def _():
        wait_scatter(nb - 2, nb & 1)
        wait_scatter(nb - 1, (nb - 1) & 1)


def _combine_kernel(ys_ref, ws_ref, pos_ref, x1_ref, mod_ref, lng_ref, lnb_ref, x2_ref):
    n, nr = SORT_T, SORT_R
    ysw = (ys_ref[...].astype(F32) * ws_ref[:, 0:1]).astype(BF16)
    pos = pos_ref[...]
    ci = lax.broadcasted_iota(I32, (n, nr), 1).astype(F32)
    unperm = jnp.where((ci == pos[:, 0:1]) | (ci == pos[:, 1:2]), 1.0, 0.0).astype(BF16)
    y = _dot(unperm, ysw)
    g2 = mod_ref[5:6, :]
    x2_ref[...] = _layer_norm(ALPHA * x1_ref[...] + (1.0 + g2) * y, lng_ref[...], lnb_ref[...])


def _moe_and_norm(x1, h2, route, mod_l, lng, lnb, wg, wu, wd, layer):
    bsz, s, d = x1.shape
    n_tok = bsz * s
    nt = n_tok // SORT_T
    n_rows = nt * SORT_R
    x1f = x1.reshape(n_tok, d)
    h2f = h2.reshape(n_tok, d)
    rt = route.reshape(n_tok, LANES)

    xs, ws, pos, cnt = pl.pallas_call(
        _moe_sort_kernel,
        grid=(nt,),
        in_specs=[pl.BlockSpec((SORT_T, d), lambda i: (i, 0)),
                  pl.BlockSpec((SORT_T, LANES), lambda i: (i, 0))],
        out_specs=[pl.BlockSpec((SORT_R, d), lambda i: (i, 0)),
                   pl.BlockSpec((SORT_R, LANES), lambda i: (i, 0)),
                   pl.BlockSpec((SORT_T, LANES), lambda i: (i, 0)),
                   pl.BlockSpec((None, 1, LANES), lambda i: (i, 0, 0))],
        out_shape=[jax.ShapeDtypeStruct((n_rows, d), BF16),
                   jax.ShapeDtypeStruct((n_rows, LANES), F32),
                   jax.ShapeDtypeStruct((n_tok, LANES), F32),
                   jax.ShapeDtypeStruct((nt, 1, LANES), F32)],
        compiler_params=_cparams(("parallel",)),
        name="moe_sort",
    )(h2f, rt)

    cnt_i = cnt[:, 0, :MOE_EXPERTS].astype(I32)
    pcnt = (cnt_i + SEG_Q - 1) // SEG_Q * SEG_Q
    seg = (jnp.cumsum(pcnt, axis=1) - pcnt).reshape(-1).astype(I32)
    nch = (pcnt // SEG_Q).reshape(-1).astype(I32)

    n_src = n_rows // SEG_Q + MOE_EXPERTS * EXP_CPB
    n_blk = n_src // EXP_CPB
    smem = pl.BlockSpec(memory_space=pltpu.SMEM)
    src, block_e, block_n, n_used = pl.pallas_call(
        functools.partial(_chunk_list_kernel, n_tiles=nt),
        in_specs=[smem, smem],
        out_specs=[smem, smem, smem, smem],
        out_shape=[jax.ShapeDtypeStruct((n_src,), I32),
                   jax.ShapeDtypeStruct((n_blk,), I32),
                   jax.ShapeDtypeStruct((n_blk,), I32),
                   jax.ShapeDtypeStruct((1,), I32)],
        name="moe_chunk_list",
    )(seg, nch)

    ys = pl.pallas_call(
        _expert_kernel,
        grid_spec=pltpu.PrefetchScalarGridSpec(
            num_scalar_prefetch=4,
            grid=(n_blk,),
            in_specs=[pl.BlockSpec((None, None, d, MOE_HIDDEN), lambda b, src, be, bc, nb: (layer, be[b], 0, 0)),
                      pl.BlockSpec((None, None, d, MOE_HIDDEN), lambda b, src, be, bc, nb: (layer, be[b], 0, 0)),
                      pl.BlockSpec((None, None, MOE_HIDDEN, d), lambda b, src, be, bc, nb: (layer, be[b], 0, 0)),
                      pl.BlockSpec(memory_space=pl.ANY)],
            out_specs=pl.BlockSpec(memory_space=pl.ANY),
            scratch_shapes=[pltpu.VMEM((d, 2 * MOE_HIDDEN), BF16),
                            pltpu.VMEM((MOE_HIDDEN, d), BF16),
                            pltpu.VMEM((2, EXP_BM, d), BF16),
                            pltpu.VMEM((2, EXP_BM, d), BF16),
                            pltpu.SemaphoreType.DMA((2,)),
                            pltpu.SemaphoreType.DMA((2,))]),
        out_shape=jax.ShapeDtypeStruct((n_rows, d), BF16),
        input_output_aliases={7: 0},
        compiler_params=_cparams(("arbitrary",)),
        name="moe_experts",
    )(src, block_e, block_n, n_used, wg, wu, wd, xs)

    per_b = s // SORT_T
    x2 = pl.pallas_call(
        _combine_kernel,
        grid=(nt,),
        in_specs=[pl.BlockSpec((SORT_R, d), lambda i: (i, 0)),
                  pl.BlockSpec((SORT_R, LANES), lambda i: (i, 0)),
                  pl.BlockSpec((SORT_T, LANES), lambda i: (i, 0)),
                  pl.BlockSpec((SORT_T, d), lambda i: (i, 0)),
                  pl.BlockSpec((None, 6, d), lambda i: (i // per_b, 0, 0)),
                  pl.BlockSpec((1, d), lambda i: (0, 0)),
                  pl.BlockSpec((1, d), lambda i: (0, 0))],
        out_specs=pl.BlockSpec((SORT_T, d), lambda i: (i, 0)),
        out_shape=jax.ShapeDtypeStruct((n_tok, d), F32),
        compiler_params=_cparams(("parallel",)),
        name="moe_combine_norm",
    )(ys, ws, pos, x1f, mod_l, lng, lnb)
    return x2.reshape(bsz, s, d)


def _rope_table_kernel(pos_ref, inv_ref, sgn_ref, cos_ref, sin_ref):
    ang = pos_ref[...] * inv_ref[...]
    cos_ref[...] = jnp.cos(ang)
    sin_ref[...] = jnp.sin(ang) * sgn_ref[...]


def _rope_tables(positions):
    bsz, s = positions.shape
    half = B_HEAD_DIM // 2
    inv = jnp.power(jnp.float32(ROPE_THETA), -jnp.arange(half, dtype=F32) * 2.0 / B_HEAD_DIM)
    inv_row = jnp.tile(inv, LANES // half).reshape(1, LANES)
    sgn_row = jnp.tile(jnp.concatenate([-jnp.ones((half,), F32), jnp.ones((half,), F32)]),
                       LANES // B_HEAD_DIM).reshape(1, LANES)
    pos = jnp.broadcast_to(positions.astype(F32)[:, :, None], (bsz, s, LANES))
    tok = pl.BlockSpec((None, ROW_T, LANES), lambda b, i: (b, i, 0))
    row = pl.BlockSpec((1, LANES), lambda b, i: (0, 0))
    return pl.pallas_call(
        _rope_table_kernel,
        grid=(bsz, s // ROW_T),
        in_specs=[tok, row, row],
        out_specs=[tok, tok],
        out_shape=[jax.ShapeDtypeStruct((bsz, s, LANES), F32)] * 2,
        compiler_params=_cparams(("parallel", "parallel")),
        name="rope_tables",
    )(pos, inv_row, sgn_row)


def _rope(t, cosf, sinf, first_half):
    n = t.shape[1]
    half = B_HEAD_DIM // 2
    fwd = pltpu.roll(t, n - half, 1)
    bwd = pltpu.roll(t, half, 1)
    return t * cosf + jnp.where(first_half, fwd, bwd) * sinf


def _rope_operands(cos_ref, sin_ref, rows):
    reps = B_WIDTH // LANES
    cosf = jnp.concatenate([cos_ref[...]] * reps, axis=1)
    sinf = jnp.concatenate([sin_ref[...]] * reps, axis=1)
    lane = lax.broadcasted_iota(I32, (rows, B_WIDTH), 1)
    first_half = (lane & (B_HEAD_DIM - 1)) < (B_HEAD_DIM // 2)
    return cosf, sinf, first_half


def _store_by_residue(val, out_ref, stage_ref, dil):
    rows, width = val.shape
    if dil == 1:
        out_ref[0] = val.astype(out_ref.dtype)
        return
    for c in range(width // LANES):
        stage_ref[c] = val[:, c * LANES:(c + 1) * LANES]
    for r in range(dil):
        for c in range(width // LANES):
            out_ref[r, :, c * LANES:(c + 1) * LANES] = (
                stage_ref[c, pl.ds(r, rows // dil, stride=dil), :].astype(out_ref.dtype))


def _load_by_residue(in_ref, stage_ref, dil):
    _, sub, width = in_ref.shape
    if dil == 1:
        return in_ref[0].astype(F32)
    for r in range(dil):
        for c in range(width // LANES):
            stage_ref[c, pl.ds(r, sub, stride=dil), :] = in_ref[r, :, c * LANES:(c + 1) * LANES].astype(F32)
    return jnp.concatenate([stage_ref[c] for c in range(width // LANES)], axis=1)


def _class_spec(dil, width):
    return pl.BlockSpec((None, dil, ROW_T // dil, width), lambda b, i: (b, 0, i, 0))


def _class_shape(bsz, s, dil, width, dtype):
    return jax.ShapeDtypeStruct((bsz, dil, s // dil, width), dtype)


def _kv_kernel(x_ref, w_ref, cos_ref, sin_ref, k0, k1, k2, v0, v1, v2, stage_ref):
    xb = x_ref[...].astype(BF16)
    cosf, sinf, first_half = _rope_operands(cos_ref, sin_ref, ROW_T)
    for g, (k_ref, v_ref) in enumerate(((k0, v0), (k1, v1), (k2, v2))):
        kg = _dot(xb, w_ref[:, g * B_WIDTH:(g + 1) * B_WIDTH])
        _store_by_residue(_rope(kg, cosf, sinf, first_half), k_ref, stage_ref, B_DILATIONS[g])
        vo = (B_GROUPS + g) * B_WIDTH
        _store_by_residue(_dot(xb, w_ref[:, vo:vo + B_WIDTH]), v_ref, stage_ref, B_DILATIONS[g])


def _kv_project(x, wkv, cos_t, sin_t):
    bsz, s, d = x.shape
    tok = lambda w: pl.BlockSpec((None, ROW_T, w), lambda b, i: (b, i, 0))
    dils = B_DILATIONS * 2
    return pl.pallas_call(
        _kv_kernel,
        grid=(bsz, s // ROW_T),
        in_specs=[tok(d), pl.BlockSpec(wkv.shape, lambda b, i: (0, 0)), tok(LANES), tok(LANES)],
        out_specs=[_class_spec(dl, B_WIDTH) for dl in dils],
        out_shape=[_class_shape(bsz, s, dl, B_WIDTH, BF16) for dl in dils],
        scratch_shapes=[pltpu.VMEM((B_WIDTH // LANES, ROW_T, LANES), F32)],
        compiler_params=_cparams(("parallel", "parallel")),
        name="kv_project",
    )(x, wkv, cos_t, sin_t)


def _q_kernel(x_ref, mod_ref, w_ref, cos_ref, sin_ref, q0, q1, q2, stage_ref):
    md = mod_ref[...]
    hb = (x_ref[...] * (1.0 + md[1:2, :]) + md[0:1, :]).astype(BF16)
    cosf, sinf, first_half = _rope_operands(cos_ref, sin_ref, ROW_T)
    for g, q_ref in enumerate((q0, q1, q2)):
        qg = _dot(hb, w_ref[:, g * B_WIDTH:(g + 1) * B_WIDTH])
        _store_by_residue(_rope(qg, cosf, sinf, first_half) * (B_HEAD_DIM ** -0.5), q_ref, stage_ref,
                          B_DILATIONS[g])


def _q_project(x, mod_l, wq, cos_t, sin_t):
    bsz, s, d = x.shape
    tok = lambda w: pl.BlockSpec((None, ROW_T, w), lambda b, i: (b, i, 0))
    return pl.pallas_call(
        _q_kernel,
        grid=(bsz, s // ROW_T),
        in_specs=[tok(d), pl.BlockSpec((None, 6, d), lambda b, i: (b, 0, 0)),
                  pl.BlockSpec(wq.shape, lambda b, i: (0, 0)), tok(LANES), tok(LANES)],
        out_specs=[_class_spec(dl, B_WIDTH) for dl in B_DILATIONS],
        out_shape=[_class_shape(bsz, s, dl, B_WIDTH, BF16) for dl in B_DILATIONS],
        scratch_shapes=[pltpu.VMEM((B_WIDTH // LANES, ROW_T, LANES), F32)],
        compiler_params=_cparams(("parallel", "parallel")),
        name="q_project",
    )(x, mod_l, wq, cos_t, sin_t)


def _attn_kernel(q_ref, kc_ref, kp_ref, vc_ref, vp_ref, o_ref, lse_ref, kw_ref, vx_ref, *, nsub):
    wb = B_WB
    npair = B_WIDTH // LANES
    first_step = pl.program_id(2) == 0
    kw_ref[0:wb, :] = kp_ref[...]
    kw_ref[wb:, :] = kc_ref[...]
    for p in range(npair):
        vx_ref[p, 0:wb, 0:LANES] = vp_ref[:, p * LANES:(p + 1) * LANES]
        vx_ref[p, wb:, 0:LANES] = vc_ref[:, p * LANES:(p + 1) * LANES]
        vx_ref[p, :, LANES:2 * LANES] = jnp.ones((vx_ref.shape[1], LANES), BF16)
    qi = lax.broadcasted_iota(I32, (wb, 2 * wb), 0)
    kj = lax.broadcasted_iota(I32, (wb, 2 * wb), 1)
    band = (kj >= qi) & (kj <= qi + wb)
    lane = lax.broadcasted_iota(I32, (wb, LANES), 1)
    low = lane < B_HEAD_DIM

    def sub_block(j, carry):
        r0 = pl.multiple_of(j * wb, wb)
        valid = band & ((kj >= wb) | (j > 0) | jnp.logical_not(first_step))
        q = q_ref[pl.ds(r0, wb), :]
        kwin = kw_ref[pl.ds(r0, 2 * wb), :]
        outs = []
        m_all = jnp.zeros((wb, LANES), F32)
        l_all = jnp.ones((wb, LANES), F32)
        for p in range(npair):
            qp = q[:, p * LANES:(p + 1) * LANES]
            kpair = kwin[:, p * LANES:(p + 1) * LANES]
            vx = vx_ref[p, pl.ds(r0, 2 * wb), :]
            acc = lsum = None
            for hh in range(2):
                mine = low if hh == 0 else jnp.logical_not(low)
                qm = jnp.where(mine, qp, jnp.zeros_like(qp))
                sc = jnp.where(valid, _dot_nt(qm, kpair), -jnp.inf)
                m = jnp.max(sc, axis=-1, keepdims=True)
                pv = _dot(jnp.exp(sc - m).astype(BF16), vx)
                o_h, l_h = pv[:, 0:LANES], pv[:, LANES:2 * LANES]
                m_all = jnp.where(lane == 2 * p + hh, m, m_all)
                l_all = jnp.where(lane == 2 * p + hh, l_h, l_all)
                acc = o_h if hh == 0 else jnp.where(low, acc, o_h)
                lsum = l_h if hh == 0 else jnp.where(low, lsum, l_h)
            outs.append(acc * (1.0 / lsum))
        o_ref[pl.ds(r0, wb), :] = jnp.concatenate(outs, axis=1).astype(BF16)
        lse_ref[pl.ds(r0, wb), :] = m_all + jnp.log(l_all)
        return carry

    lax.fori_loop(0, nsub, sub_block, 0)


def _dilated_attention(q, k, v, nsub):
    bsz, dil, rows, w = q.shape
    blk = B_WB * nsub
    cur = lambda width: pl.BlockSpec((None, None, blk, width), lambda b, r, n: (b, r, n, 0))
    prev = pl.BlockSpec((None, None, B_WB, w), lambda b, r, n: (b, r, jnp.maximum(n * nsub - 1, 0), 0))
    return pl.pallas_call(
        functools.partial(_attn_kernel, nsub=nsub),
        grid=(bsz, dil, rows // blk),
        in_specs=[cur(w), cur(w), prev, cur(w), prev],
        out_specs=[cur(w), cur(LANES)],
        out_shape=[jax.ShapeDtypeStruct((bsz, dil, rows, w), BF16),
                   jax.ShapeDtypeStruct((bsz, dil, rows, LANES), F32)],
        scratch_shapes=[pltpu.VMEM((blk + B_WB, w), BF16),
                        pltpu.VMEM((w // LANES, blk + B_WB, 2 * LANES), BF16)],
        compiler_params=_cparams(("parallel", "parallel", "arbitrary")),
        name=f"dilated_attention_d{dil}",
    )(q, k, k, v, v)


def _attn_out_kernel(x_ref, mod_ref, o0, o1, o2, l0, l1, l2, wo_ref, lng_ref, lnb_ref, wr_ref, br_ref,
                     x1_ref, h2_ref, route_ref, stage_ref):
    md = mod_ref[...]
    g1, sh2, sc2 = md[2:3, :], md[3:4, :], md[4:5, :]
    lses = [_load_by_residue(l_ref, stage_ref, dl) for l_ref, dl in zip((l0, l1, l2), B_DILATIONS)]
    mx = jnp.maximum(jnp.maximum(lses[0], lses[1]), lses[2])
    es = [jnp.exp(l - mx) for l in lses]
    inv = 1.0 / (es[0] + es[1] + es[2])

    def per_head(wc):
        return jnp.concatenate([jnp.broadcast_to(wc[:, h:h + 1], (ROW_T, B_HEAD_DIM)) for h in range(B_HEADS)],
                               axis=1)

    o = jnp.zeros((ROW_T, B_WIDTH), F32)
    for o_ref, e, dl in zip((o0, o1, o2), es, B_DILATIONS):
        o = o + per_head(e * inv) * _load_by_residue(o_ref, stage_ref, dl)
    y = _dot(o.astype(BF16), wo_ref[...])
    x1 = _layer_norm(ALPHA * x_ref[...] + (1.0 + g1) * y, lng_ref[...], lnb_ref[...])
    x1_ref[...] = x1
    h2 = x1 * (1.0 + sc2) + sh2
    h2_ref[...] = h2
    route_ref[...] = _route_tile(h2, wr_ref, br_ref)


def _attn_out_layer(x, mod_l, outs, lses, wo, lng, lnb, wr, br):
    bsz, s, d = x.shape
    tok = lambda w: pl.BlockSpec((None, ROW_T, w), lambda b, i: (b, i, 0))
    const = lambda b, i: (0, 0)
    return pl.pallas_call(
        _attn_out_kernel,
        grid=(bsz, s // ROW_T),
        in_specs=[tok(d), pl.BlockSpec((None, 6, d), lambda b, i: (b, 0, 0))]
                 + [_class_spec(dl, B_WIDTH) for dl in B_DILATIONS]
                 + [_class_spec(dl, LANES) for dl in B_DILATIONS]
                 + [pl.BlockSpec((B_WIDTH, d), const), pl.BlockSpec((1, d), const), pl.BlockSpec((1, d), const),
                    pl.BlockSpec((2, d, LANES), lambda b, i: (0, 0, 0)), pl.BlockSpec((1, LANES), const)],
        out_specs=[tok(d), tok(d), tok(LANES)],
        out_shape=[jax.ShapeDtypeStruct((bsz, s, d), F32),
                   jax.ShapeDtypeStruct((bsz, s, d), F32),
                   jax.ShapeDtypeStruct((bsz, s, LANES), F32)],
        scratch_shapes=[pltpu.VMEM((B_WIDTH // LANES, ROW_T, LANES), F32)],
        compiler_params=_cparams(("parallel", "parallel")),
        name="attn_out_layer",
    )(x, mod_l, *outs, *lses, wo, lng, lnb, wr, br)


def _router_weights(w_r1, b_r1, w_r2, b_r2):
    d = w_r1.shape[0]
    n = MOE_GROUPS + MOE_EXPERTS
    wr = jnp.zeros((d, LANES), F32).at[:, :MOE_GROUPS].set(w_r1).at[:, MOE_GROUPS:n].set(w_r2)
    br = jnp.zeros((1, LANES), F32).at[0, :MOE_GROUPS].set(b_r1).at[0, MOE_GROUPS:n].set(b_r2)
    wr_hi = wr.astype(BF16)
    wr_lo = (wr - wr_hi.astype(F32)).astype(BF16)
    return jnp.stack([wr_hi, wr_lo]), br


def kernel(x, c, positions, ada_w, ada_b, ln_g, ln_b, a_w_in, a_b_gate, a_norm_g, a_w_out, b_w_kv, b_w_q, b_w_o,
           moe_w_r1, moe_b_r1, moe_w_r2, moe_b_r2, moe_w_gate, moe_w_up, moe_w_down):
    bsz, s, d = x.shape
    assert d == D_MODEL and s % (B_WB * B_DILATIONS[-1]) == 0 and s % ROW_T == 0 and s % A_STEP == 0 and A_STEP % A_CHUNK == 0
    assert (bsz * s) % SORT_T == 0 and s % SORT_T == 0
    assert 2 * SORT_T + MOE_EXPERTS * (SEG_Q - 1) <= SORT_R and SORT_R % SEG_Q == 0
    mod = _adaln_mod(c, ada_w, ada_b)
    cos_t, sin_t = _rope_tables(positions)
    kv = None
    for l in range(DEPTH):
        lng1, lnb1 = ln_g[l, 0].reshape(1, d), ln_b[l, 0].reshape(1, d)
        lng2, lnb2 = ln_g[l, 1].reshape(1, d), ln_b[l, 1].reshape(1, d)
        wr, br = _router_weights(moe_w_r1[l], moe_b_r1[l], moe_w_r2[l], moe_b_r2[l])
        if l < N_A:
            n_main = 2 * A_QK + 2 * A_V
            win = jnp.zeros((d, n_main + LANES), BF16)
            win = win.at[:, :n_main + 2 * A_HEADS].set(a_w_in[l].astype(BF16))
            bg = jnp.zeros((1, LANES), F32).at[0, :2 * A_HEADS].set(a_b_gate[l])
            x1, h2, route = _mlstm_layer(x, mod[l], win, bg, a_norm_g[l].reshape(1, A_V),
                                         a_w_out[l].astype(BF16), lng1, lnb1, wr, br)
        else:
            lb = l - N_A
            qs = _q_project(x, mod[l], b_w_q[lb].astype(BF16), cos_t, sin_t)
            outs, lses = [], []
            for g in range(B_GROUPS):
                nsub = min(ATT_J[g], s // (B_WB * B_DILATIONS[g]))
                o_g, l_g = _dilated_attention(qs[g], kv[g], kv[B_GROUPS + g], nsub)
                outs.append(o_g)
                lses.append(l_g)
            x1, h2, route = _attn_out_layer(x, mod[l], outs, lses, b_w_o[lb].astype(BF16), lng1, lnb1, wr, br)
        x = _moe_and_norm(x1, h2, route, mod[l], lng2, lnb2, moe_w_gate, moe_w_up, moe_w_down, l)
        if l == N_A - 1:
            kv = _kv_project(x, b_w_kv.astype(BF16), cos_t, sin_t)
    return x
```

```python
import functools

import jax
import jax.numpy as jnp
from jax import lax
from jax.experimental import pallas as pl
from jax.experimental.pallas import tpu as pltpu

F32 = jnp.float32
BF16 = jnp.bfloat16
I32 = jnp.int32
HI = lax.Precision.HIGHEST

D_MODEL = 1024
DEPTH = 4
N_A = DEPTH // 2
ALPHA = (2.0 * DEPTH) ** 0.25
LN_EPS = 1e-5
A_HEADS = 4
A_DQK = D_MODEL // 8
A_DV = D_MODEL // 4
A_NORM_EPS = 1e-6
A_QK = A_HEADS * A_DQK
A_V = A_HEADS * A_DV
B_WINDOWS = (128, 512, 2048)
B_DILATIONS = (1, 4, 16)
B_GROUPS = 3
B_HEADS = 8
B_HEAD_DIM = 64
B_WIDTH = B_HEADS * B_HEAD_DIM
B_WB = 128
ROPE_THETA = 10000.0
MOE_GROUPS = 4
MOE_EPG = 8
MOE_EXPERTS = MOE_GROUPS * MOE_EPG
MOE_HIDDEN = D_MODEL // 4

LANES = 128
SUBLANES = 8
VMEM_LIMIT = 56 * 1024 * 1024

A_CHUNK = 256
A_STEP = 512
SORT_T = 512
SEG_Q = 16
SORT_R = 1536
EXP_BM = 512
EXP_CPB = EXP_BM // SEG_Q
LIST_UNROLL = 4
ROW_T = 512
ATT_J = (8, 8, 4)
MOD_TN = 1536


def _cparams(sem):
    return pltpu.CompilerParams(dimension_semantics=sem, vmem_limit_bytes=VMEM_LIMIT)


def _dot(a, b):
    return jnp.dot(a, b, preferred_element_type=F32)


def _dot_nt(a, b):
    return lax.dot_general(a, b, (((1,), (1,)), ((), ())), preferred_element_type=F32)


def _dot_tn(a, b):
    return lax.dot_general(a, b, (((0,), (0,)), ((), ())), preferred_element_type=F32)


def _layer_norm(v, g, b):
    mu = jnp.mean(v, axis=-1, keepdims=True)
    vc = v - mu
    var = jnp.mean(vc * vc, axis=-1, keepdims=True)
    return vc * lax.rsqrt(var + LN_EPS) * g + b


def _sigmoid(v):
    return 1.0 / (1.0 + jnp.exp(-v))


def _split_hi_lo(v):
    hi = v.astype(BF16)
    return hi, (v - hi.astype(F32)).astype(BF16)


def _route_tile(h2, wr_ref, br_ref):
    h_hi, h_lo = _split_hi_lo(h2)
    lg = _dot(h_hi, wr_ref[0]) + (_dot(h_hi, wr_ref[1]) + _dot(h_lo, wr_ref[0])) + br_ref[...]
    lane = lax.broadcasted_iota(I32, lg.shape, 1).astype(F32)
    neg = -jnp.inf
    big = 1000.0
    m1 = jnp.where(lane < MOE_GROUPS, lg, neg)
    mx = jnp.max(m1, axis=-1, keepdims=True)
    pg = 1.0 / jnp.sum(jnp.exp(m1 - mx), axis=-1, keepdims=True)
    gi = jnp.min(jnp.where(m1 == mx, lane, big), axis=-1, keepdims=True)
    lo = MOE_GROUPS + gi * MOE_EPG
    m2 = jnp.where((lane >= lo) & (lane < lo + MOE_EPG), lg, neg)
    v0 = jnp.max(m2, axis=-1, keepdims=True)
    j0 = jnp.min(jnp.where(m2 == v0, lane, big), axis=-1, keepdims=True)
    m3 = jnp.where(lane == j0, neg, m2)
    v1 = jnp.max(m3, axis=-1, keepdims=True)
    j1 = jnp.min(jnp.where(m3 == v1, lane, big), axis=-1, keepdims=True)
    t = jnp.exp(v1 - v0)
    wa = 1.0 / (1.0 + t)
    wb = t * wa
    out = jnp.where(lane == 0, j0 - MOE_GROUPS,
          jnp.where(lane == 1, j1 - MOE_GROUPS,
          jnp.where(lane == 2, pg * wa,
          jnp.where(lane == 3, pg * wb, 0.0))))
    return out


def _mod_kernel(c_ref, w_ref, b_ref, o_ref):
    c = c_ref[...]
    cond = c * _sigmoid(c)
    o_ref[0] = jnp.dot(cond, w_ref[0], preferred_element_type=F32, precision=HI) + b_ref[0]


def _adaln_mod(c, ada_w, ada_b):
    bsz, d = c.shape
    depth, _, n6 = ada_w.shape
    rows = 8
    c_pad = jnp.zeros((rows, d), F32).at[:bsz].set(c)
    out = pl.pallas_call(
        _mod_kernel,
        grid=(depth, n6 // MOD_TN),
        in_specs=[pl.BlockSpec((rows, d), lambda l, j: (0, 0)),
                  pl.BlockSpec((1, d, MOD_TN), lambda l, j: (l, 0, j)),
                  pl.BlockSpec((1, 1, MOD_TN), lambda l, j: (l, 0, j))],
        out_specs=pl.BlockSpec((1, rows, MOD_TN), lambda l, j: (l, 0, j)),
        out_shape=jax.ShapeDtypeStruct((depth, rows, n6), F32),
        compiler_params=_cparams(("parallel", "parallel")),
        name="adaln_mod",
    )(c_pad, ada_w, ada_b.reshape(depth, 1, n6))
    return out[:, :bsz].reshape(depth, bsz, 6, d)


def _mlstm_layer_kernel(x_ref, mod_ref, win_ref, bg_ref, ng_ref, wout_ref, lng_ref, lnb_ref,
                        wr_ref, br_ref, x1_ref, h2_ref, route_ref, c_ref, n_ref, m_ref):
    L = A_CHUNK
    n_chunks = A_STEP // L

    @pl.when(pl.program_id(1) == 0)
    def _():
        c_ref[...] = jnp.zeros_like(c_ref)
        n_ref[...] = jnp.zeros_like(n_ref)
        m_ref[...] = jnp.zeros_like(m_ref)

    md = mod_ref[...]
    sh1, sc1, g1, sh2, sc2, g2 = [md[j:j + 1, :] for j in range(6)]
    o_q, o_k, o_v, o_o, o_g = 0, A_QK, 2 * A_QK, 2 * A_QK + A_V, 2 * A_QK + 2 * A_V

    row = lax.broadcasted_iota(I32, (L, L), 0)
    col = lax.broadcasted_iota(I32, (L, L), 1)
    causal = row >= col
    tri = jnp.where(causal, 1.0, 0.0).astype(BF16)

    def proj_parts(ci):
        rows = slice(ci * L, (ci + 1) * L)
        out = {}

        def start():
            out["hb"] = (x_ref[rows, :] * (1.0 + sc1) + sh1).astype(BF16)
            out["q"] = _dot(out["hb"], win_ref[:, o_q:o_k])

        def kpart():
            out["k"] = _dot(out["hb"], win_ref[:, o_k:o_v]) * (A_DQK ** -0.5)

        def vpart():
            out["v"] = _dot(out["hb"], win_ref[:, o_v:o_o])

        def opart():
            out["og"] = _dot(out["hb"], win_ref[:, o_o:o_g])
            out["gates"] = _dot(out["hb"], win_ref[:, o_g:o_g + LANES]) + bg_ref[...]

        return out, [start, kpart, vpart, opart]

    def gate_terms(pr):
        gates = pr["gates"]
        lf = jnp.minimum(gates, 0.0) - jnp.log(1.0 + jnp.exp(-jnp.abs(gates)))
        lf_hi, lf_lo = _split_hi_lo(lf)
        lf_lo2 = (lf - lf_hi.astype(F32) - lf_lo.astype(F32)).astype(BF16)
        b_col = _dot(tri, lf_hi) + (_dot(tri, lf_lo) + _dot(tri, lf_lo2))
        return gates, b_col, gates.T, b_col.T

    def head(pr, gt, h):
        gates, b_col, g_t, b_t = gt
        qh = pr["q"][:, h * A_DQK:(h + 1) * A_DQK]
        kh = pr["k"][:, h * A_DQK:(h + 1) * A_DQK]
        vh = pr["v"][:, h * A_DV:(h + 1) * A_DV]
        qb, kb, vb = qh.astype(BF16), kh.astype(BF16), vh.astype(BF16)
        fl = A_HEADS + h
        b_c = b_col[:, fl:fl + 1]
        i_c = gates[:, h:h + 1]
        b_r = b_t[fl:fl + 1, :]
        i_r = g_t[h:h + 1, :]
        m_st = m_ref[h:h + 1, 0:1]
        c_st = c_ref[h]
        n_st = n_ref[h:h + 1, :]

        dmat = jnp.where(causal, b_c - b_r + i_r, -jnp.inf)
        inter = b_c + m_st
        m_t = jnp.maximum(inter, jnp.max(dmat, axis=-1, keepdims=True))
        dw = jnp.exp(dmat - m_t)
        iw = jnp.exp(inter - m_t)
        sc = _dot_nt(qb, kb) * dw
        num = _dot(sc.astype(BF16), vb) + iw * _dot(qb, c_st.astype(BF16))
        den = jnp.sum(sc, axis=-1, keepdims=True) + iw * jnp.sum(qh * n_st, axis=-1, keepdims=True)
        hh = num * (1.0 / jnp.maximum(jnp.abs(den), jnp.exp(-m_t)))

        b_last = b_col[L - 1:L, fl:fl + 1]
        ws_log = b_last - b_c + i_c
        m_new = jnp.maximum(b_last + m_st, jnp.max(ws_log, axis=0, keepdims=True))
        ws = jnp.exp(ws_log - m_new)
        decay = jnp.exp(b_last + m_st - m_new)
        kw = kh * ws
        c_ref[h] = decay * c_st + _dot_tn(kw.astype(BF16), vb)
        n_ref[h:h + 1, :] = decay * n_st + jnp.sum(kw, axis=0, keepdims=True)
        m_ref[h:h + 1, :] = jnp.broadcast_to(m_new, (1, LANES))

        mu = jnp.mean(hh, axis=-1, keepdims=True)
        hc = hh - mu
        var = jnp.mean(hc * hc, axis=-1, keepdims=True)
        hn = hc * lax.rsqrt(var + A_NORM_EPS)
        og = pr["og"][:, h * A_DV:(h + 1) * A_DV]
        return (hn * ng_ref[:, h * A_DV:(h + 1) * A_DV] * _sigmoid(og)).astype(BF16)

    def tail_parts(ci, ys):
        rows = slice(ci * L, (ci + 1) * L)
        st = {}

        def outproj():
            st["y"] = _dot(jnp.concatenate(ys, axis=1), wout_ref[...])

        def norm():
            x1 = _layer_norm(ALPHA * x_ref[rows, :] + (1.0 + g1) * st["y"], lng_ref[...], lnb_ref[...])
            x1_ref[rows, :] = x1
            st["h2"] = x1 * (1.0 + sc2) + sh2
            h2_ref[rows, :] = st["h2"].astype(h2_ref.dtype)

        def router():
            route_ref[rows, :] = _route_tile(st["h2"], wr_ref, br_ref)

        return [outproj, norm, router]

    pr, parts = proj_parts(0)
    for part in parts:
        part()
    pending_tail = []
    for ci in range(n_chunks):
        nxt, nxt_parts = proj_parts(ci + 1) if ci + 1 < n_chunks else (None, [])
        fill = nxt_parts + pending_tail
        gt = gate_terms(pr)
        ys = []
        for h in range(A_HEADS):
            ys.append(head(pr, gt, h))
            share = -(-len(fill) // (A_HEADS - h))
            for part in fill[:share]:
                part()
            fill = fill[share:]
        pending_tail = tail_parts(ci, ys)
        pr = nxt
    for part in pending_tail:
        part()


def _mlstm_layer(x, mod_l, win, bg, ng, wout, lng, lnb, wr, br):
    bsz, s, d = x.shape
    nw = win.shape[1]
    const = lambda b, i: (0, 0)
    tok = pl.BlockSpec((None, A_STEP, d), lambda b, i: (b, i, 0))
    return pl.pallas_call(
        _mlstm_layer_kernel,
        grid=(bsz, s // A_STEP),
        in_specs=[tok,
                  pl.BlockSpec((None, 6, d), lambda b, i: (b, 0, 0)),
                  pl.BlockSpec((d, nw), const),
                  pl.BlockSpec((1, LANES), const),
                  pl.BlockSpec((1, A_V), const),
                  pl.BlockSpec((A_V, d), const),
                  pl.BlockSpec((1, d), const),
                  pl.BlockSpec((1, d), const),
                  pl.BlockSpec((2, d, LANES), lambda b, i: (0, 0, 0)),
                  pl.BlockSpec((1, LANES), const)],
        out_specs=[tok, tok, pl.BlockSpec((None, A_STEP, LANES), lambda b, i: (b, i, 0))],
        out_shape=[jax.ShapeDtypeStruct((bsz, s, d), F32),
                   jax.ShapeDtypeStruct((bsz, s, d), BF16),
                   jax.ShapeDtypeStruct((bsz, s, LANES), F32)],
        scratch_shapes=[pltpu.VMEM((A_HEADS, A_DQK, A_DV), F32),
                        pltpu.VMEM((8, A_DQK), F32),
                        pltpu.VMEM((8, LANES), F32)],
        compiler_params=_cparams(("parallel", "arbitrary")),
        name="mlstm_layer",
    )(x, mod_l, win, bg, ng, wout, lng, lnb, wr, br)


def _moe_sort_kernel(h2_ref, route_ref, xs_ref, ws_ref, pos_ref, cnt_ref):
    n, nr = SORT_T, SORT_R
    r = route_ref[...]
    lane = lax.broadcasted_iota(I32, (n, LANES), 1).astype(F32)
    oh0 = lane == r[:, 0:1]
    oh1 = lane == r[:, 1:2]
    oh = jnp.where(oh0 | oh1, 1.0, 0.0)
    row = lax.broadcasted_iota(I32, (n, n), 0)
    col = lax.broadcasted_iota(I32, (n, n), 1)
    before = jnp.where(row > col, 1.0, 0.0).astype(BF16)
    excl = _dot(before, oh.astype(BF16))
    cnt = jnp.sum(oh, axis=0, keepdims=True)
    pcnt = jnp.ceil(cnt * (1.0 / SEG_Q)) * SEG_Q
    ej = lax.broadcasted_iota(I32, (LANES, LANES), 0)
    el = lax.broadcasted_iota(I32, (LANES, LANES), 1)
    upper = jnp.where(ej < el, 1.0, 0.0).astype(F32)
    seg = jnp.dot(jnp.broadcast_to(pcnt, (SUBLANES, LANES)), upper,
                  preferred_element_type=F32, precision=HI)[0:1, :]
    base = excl + seg
    p0 = jnp.sum(jnp.where(oh0, base, 0.0), axis=-1, keepdims=True)
    p1 = jnp.sum(jnp.where(oh1, base, 0.0), axis=-1, keepdims=True)
    pos = jnp.where(lane == 0, p0, jnp.where(lane == 1, p1, 0.0))
    pos_ref[...] = pos
    cnt_ref[...] = cnt
    pos_t = pos.T
    ri = lax.broadcasted_iota(I32, (nr, n), 0).astype(F32)
    sel0 = ri == pos_t[0:1, :]
    sel1 = ri == pos_t[1:2, :]
    perm = jnp.where(sel0 | sel1, 1.0, 0.0).astype(BF16)
    xs_ref[...] = _dot(perm, h2_ref[...]).astype(BF16)

    def split3(w):
        hi = w.astype(BF16)
        r1 = w - hi.astype(F32)
        mid = r1.astype(BF16)
        lo = (r1 - mid.astype(F32)).astype(BF16)
        return jnp.where(lane == 0, hi.astype(F32),
               jnp.where(lane == 1, mid.astype(F32),
               jnp.where(lane == 2, lo.astype(F32), 0.0))).astype(BF16)

    wsum = (_dot(jnp.where(sel0, 1.0, 0.0).astype(BF16), split3(r[:, 2:3]))
            + _dot(jnp.where(sel1, 1.0, 0.0).astype(BF16), split3(r[:, 3:4])))
    ws = wsum[:, 0:1] + wsum[:, 1:2] + wsum[:, 2:3]
    ws_ref[...] = jnp.broadcast_to(ws, (nr, LANES))


def _chunk_list_kernel(seg_ref, nch_ref, src_ref, be_ref, bc_ref, nb_ref, *, n_tiles):
    n_src = src_ref.shape[0]
    n_blk = be_ref.shape[0]

    def clear_src(k, c):
        src_ref[k] = 0
        return c

    lax.fori_loop(0, n_src, clear_src, 0, unroll=16)

    def per_expert(e, carry):
        k0, nb0 = carry

        def per_tile(i, k):
            s = seg_ref[i * MOE_EXPERTS + e]
            n = nch_ref[i * MOE_EXPERTS + e]

            row0 = i * SORT_R + s
            for c in range(LIST_UNROLL):
                @pl.when(c < n)
                def _(c=c):
                    src_ref[k + c] = row0 + c * SEG_Q

            def per_chunk(c, carry):
                src_ref[k + c] = row0 + c * SEG_Q
                return carry

            lax.fori_loop(LIST_UNROLL, n, per_chunk, 0)
            return k + n

        k1 = lax.fori_loop(0, n_tiles, per_tile, k0)
        tot = k1 - k0
        nblk = (tot + EXP_CPB - 1) // EXP_CPB

        def per_block(j, c):
            be_ref[nb0 + j] = e
            bc_ref[nb0 + j] = jnp.minimum(EXP_CPB, tot - j * EXP_CPB)
            return c

        lax.fori_loop(0, nblk, per_block, 0)
        return k0 + nblk * EXP_CPB, nb0 + nblk

    _, nb = lax.fori_loop(0, MOE_EXPERTS, per_expert, (jnp.int32(0), jnp.int32(0)))
    nb_ref[0] = nb

    def spare_block(j, c):
        be_ref[j] = MOE_EXPERTS - 1
        bc_ref[j] = 0
        return c

    lax.fori_loop(nb, n_blk, spare_block, 0)


def _chunk_copy(src_ref, src_row, dst_ref, dst_row, sem):
    return pltpu.make_async_copy(src_ref.at[pl.ds(pl.multiple_of(src_row, SEG_Q), SEG_Q), :],
                                 dst_ref.at[pl.ds(pl.multiple_of(dst_row, SEG_Q), SEG_Q), :], sem)


def _expert_kernel(src_ref, be_ref, bc_ref, nb_ref, wg_ref, wu_ref, wd_ref, xs_ref, ys_ref,
                   wgu_s, wd_s, xg, yg, sem_in, sem_out):
    b = pl.program_id(0)
    last = pl.num_programs(0) - 1
    nb = nb_ref[0]

    def clamp(blk):
        return jnp.clip(blk, 0, last)

    def count(blk):
        return jnp.where((blk >= 0) & (blk <= last), bc_ref[clamp(blk)], 0)

    def per_chunk(n, fn):
        for c in range(EXP_CPB):
            @pl.when(c < n)
            def _(c=c):
                fn(c)

    def gather(blk, slot):
        per_chunk(count(blk), lambda c: _chunk_copy(
            xs_ref, src_ref[clamp(blk) * EXP_CPB + c], xg.at[slot], c * SEG_Q, sem_in.at[slot]).start())

    def wait_gather(blk, slot):
        per_chunk(count(blk), lambda c: _chunk_copy(xs_ref, 0, xg.at[slot], 0, sem_in.at[slot]).wait())

    def scatter(blk, slot):
        per_chunk(count(blk), lambda c: _chunk_copy(
            yg.at[slot], c * SEG_Q, ys_ref, src_ref[clamp(blk) * EXP_CPB + c], sem_out.at[slot]).start())

    def wait_scatter(blk, slot):
        per_chunk(count(blk), lambda c: _chunk_copy(yg.at[slot], 0, ys_ref, 0, sem_out.at[slot]).wait())

    @pl.when(b == 0)
    def _():
        xg[...] = jnp.zeros_like(xg)
        gather(0, 0)

    @pl.when((b == 0) | (be_ref[b] != be_ref[jnp.maximum(b - 1, 0)]))
    def _():
        wgu_s[:, 0:MOE_HIDDEN] = wg_ref[...].astype(BF16)
        wgu_s[:, MOE_HIDDEN:2 * MOE_HIDDEN] = wu_ref[...].astype(BF16)
        wd_s[...] = wd_ref[...].astype(BF16)

    @pl.when(b < nb)
    def _():
        slot = b & 1
        gather(b + 1, 1 - slot)
        wait_gather(b, slot)
        gu = _dot(xg[slot], wgu_s[...])
        g = gu[:, 0:MOE_HIDDEN]
        u = gu[:, MOE_HIDDEN:2 * MOE_HIDDEN]
        y = _dot((g * _sigmoid(g) * u).astype(BF16), wd_s[...])
        wait_scatter(b - 2, slot)
        yg[slot] = y.astype(BF16)
        scatter(b, slot)

    @pl.when(b == last)
    def _():
        wait_scatter(nb - 2, nb & 1)
        wait_scatter(nb - 1, (nb - 1) & 1)


def _combine_kernel(ys_ref, ws_ref, pos_ref, x1_ref, mod_ref, lng_ref, lnb_ref, x2_ref):
    n, nr = SORT_T, SORT_R
    ysw = (ys_ref[...].astype(F32) * ws_ref[:, 0:1]).astype(BF16)
    pos = pos_ref[...]
    ci = lax.broadcasted_iota(I32, (n, nr), 1).astype(F32)
    unperm = jnp.where((ci == pos[:, 0:1]) | (ci == pos[:, 1:2]), 1.0, 0.0).astype(BF16)
    y = _dot(unperm, ysw)
    g2 = mod_ref[5:6, :]
    x2_ref[...] = _layer_norm(ALPHA * x1_ref[...] + (1.0 + g2) * y, lng_ref[...], lnb_ref[...])


def _moe_and_norm(x1, h2, route, mod_l, lng, lnb, wg, wu, wd, layer):
    bsz, s, d = x1.shape
    n_tok = bsz * s
    nt = n_tok // SORT_T
    n_rows = nt * SORT_R
    x1f = x1.reshape(n_tok, d)
    h2f = h2.reshape(n_tok, d)
    rt = route.reshape(n_tok, LANES)

    xs, ws, pos, cnt = pl.pallas_call(
        _moe_sort_kernel,
        grid=(nt,),
        in_specs=[pl.BlockSpec((SORT_T, d), lambda i: (i, 0)),
                  pl.BlockSpec((SORT_T, LANES), lambda i: (i, 0))],
        out_specs=[pl.BlockSpec((SORT_R, d), lambda i: (i, 0)),
                   pl.BlockSpec((SORT_R, LANES), lambda i: (i, 0)),
                   pl.BlockSpec((SORT_T, LANES), lambda i: (i, 0)),
                   pl.BlockSpec((None, 1, LANES), lambda i: (i, 0, 0))],
        out_shape=[jax.ShapeDtypeStruct((n_rows, d), BF16),
                   jax.ShapeDtypeStruct((n_rows, LANES), F32),
                   jax.ShapeDtypeStruct((n_tok, LANES), F32),
                   jax.ShapeDtypeStruct((nt, 1, LANES), F32)],
        compiler_params=_cparams(("parallel",)),
        name="moe_sort",
    )(h2f, rt)

    cnt_i = cnt[:, 0, :MOE_EXPERTS].astype(I32)
    pcnt = (cnt_i + SEG_Q - 1) // SEG_Q * SEG_Q
    seg = (jnp.cumsum(pcnt, axis=1) - pcnt).reshape(-1).astype(I32)
    nch = (pcnt // SEG_Q).reshape(-1).astype(I32)

    n_src = n_rows // SEG_Q + MOE_EXPERTS * EXP_CPB
    n_blk = n_src // EXP_CPB
    smem = pl.BlockSpec(memory_space=pltpu.SMEM)
    src, block_e, block_n, n_used = pl.pallas_call(
        functools.partial(_chunk_list_kernel, n_tiles=nt),
        in_specs=[smem, smem],
        out_specs=[smem, smem, smem, smem],
        out_shape=[jax.ShapeDtypeStruct((n_src,), I32),
                   jax.ShapeDtypeStruct((n_blk,), I32),
                   jax.ShapeDtypeStruct((n_blk,), I32),
                   jax.ShapeDtypeStruct((1,), I32)],
        name="moe_chunk_list",
    )(seg, nch)

    ys = pl.pallas_call(
        _expert_kernel,
        grid_spec=pltpu.PrefetchScalarGridSpec(
            num_scalar_prefetch=4,
            grid=(n_blk,),
            in_specs=[pl.BlockSpec((None, None, d, MOE_HIDDEN), lambda b, src, be, bc, nb: (layer, be[b], 0, 0)),
                      pl.BlockSpec((None, None, d, MOE_HIDDEN), lambda b, src, be, bc, nb: (layer, be[b], 0, 0)),
                      pl.BlockSpec((None, None, MOE_HIDDEN, d), lambda b, src, be, bc, nb: (layer, be[b], 0, 0)),
                      pl.BlockSpec(memory_space=pl.ANY)],
            out_specs=pl.BlockSpec(memory_space=pl.ANY),
            scratch_shapes=[pltpu.VMEM((d, 2 * MOE_HIDDEN), BF16),
                            pltpu.VMEM((MOE_HIDDEN, d), BF16),
                            pltpu.VMEM((2, EXP_BM, d), BF16),
                            pltpu.VMEM((2, EXP_BM, d), BF16),
                            pltpu.SemaphoreType.DMA((2,)),
                            pltpu.SemaphoreType.DMA((2,))]),
        out_shape=jax.ShapeDtypeStruct((n_rows, d), BF16),
        input_output_aliases={7: 0},
        compiler_params=_cparams(("arbitrary",)),
        name="moe_experts",
    )(src, block_e, block_n, n_used, wg, wu, wd, xs)

    per_b = s // SORT_T
    x2 = pl.pallas_call(
        _combine_kernel,
        grid=(nt,),
        in_specs=[pl.BlockSpec((SORT_R, d), lambda i: (i, 0)),
                  pl.BlockSpec((SORT_R, LANES), lambda i: (i, 0)),
                  pl.BlockSpec((SORT_T, LANES), lambda i: (i, 0)),
                  pl.BlockSpec((SORT_T, d), lambda i: (i, 0)),
                  pl.BlockSpec((None, 6, d), lambda i: (i // per_b, 0, 0)),
                  pl.BlockSpec((1, d), lambda i: (0, 0)),
                  pl.BlockSpec((1, d), lambda i: (0, 0))],
        out_specs=pl.BlockSpec((SORT_T, d), lambda i: (i, 0)),
        out_shape=jax.ShapeDtypeStruct((n_tok, d), F32),
        compiler_params=_cparams(("parallel",)),
        name="moe_combine_norm",
    )(ys, ws, pos, x1f, mod_l, lng, lnb)
    return x2.reshape(bsz, s, d)


def _rope_table_kernel(pos_ref, inv_ref, sgn_ref, cos_ref, sin_ref):
    ang = pos_ref[...] * inv_ref[...]
    cos_ref[...] = jnp.cos(ang)
    sin_ref[...] = jnp.sin(ang) * sgn_ref[...]


def _rope_tables(positions):
    bsz, s = positions.shape
    half = B_HEAD_DIM // 2
    inv = jnp.power(jnp.float32(ROPE_THETA), -jnp.arange(half, dtype=F32) * 2.0 / B_HEAD_DIM)
    inv_row = jnp.tile(inv, LANES // half).reshape(1, LANES)
    sgn_row = jnp.tile(jnp.concatenate([-jnp.ones((half,), F32), jnp.ones((half,), F32)]),
                       LANES // B_HEAD_DIM).reshape(1, LANES)
    pos = jnp.broadcast_to(positions.astype(F32)[:, :, None], (bsz, s, LANES))
    tok = pl.BlockSpec((None, ROW_T, LANES), lambda b, i: (b, i, 0))
    row = pl.BlockSpec((1, LANES), lambda b, i: (0, 0))
    return pl.pallas_call(
        _rope_table_kernel,
        grid=(bsz, s // ROW_T),
        in_specs=[tok, row, row],
        out_specs=[tok, tok],
        out_shape=[jax.ShapeDtypeStruct((bsz, s, LANES), F32)] * 2,
        compiler_params=_cparams(("parallel", "parallel")),
        name="rope_tables",
    )(pos, inv_row, sgn_row)


def _rope(t, cosf, sinf, first_half):
    n = t.shape[1]
    half = B_HEAD_DIM // 2
    fwd = pltpu.roll(t, n - half, 1)
    bwd = pltpu.roll(t, half, 1)
    return t * cosf + jnp.where(first_half, fwd, bwd) * sinf


def _rope_operands(cos_ref, sin_ref, rows):
    reps = B_WIDTH // LANES
    cosf = jnp.concatenate([cos_ref[...]] * reps, axis=1)
    sinf = jnp.concatenate([sin_ref[...]] * reps, axis=1)
    lane = lax.broadcasted_iota(I32, (rows, B_WIDTH), 1)
    first_half = (lane & (B_HEAD_DIM - 1)) < (B_HEAD_DIM // 2)
    return cosf, sinf, first_half


def _store_by_residue(val, out_ref, stage_ref, dil):
    rows, width = val.shape
    if dil == 1:
        out_ref[0] = val.astype(out_ref.dtype)
        return
    for c in range(width // LANES):
        stage_ref[c] = val[:, c * LANES:(c + 1) * LANES]
    for r in range(dil):
        for c in range(width // LANES):
            out_ref[r, :, c * LANES:(c + 1) * LANES] = (
                stage_ref[c, pl.ds(r, rows // dil, stride=dil), :].astype(out_ref.dtype))


def _load_by_residue(in_ref, stage_ref, dil):
    _, sub, width = in_ref.shape
    if dil == 1:
        return in_ref[0].astype(F32)
    for r in range(dil):
        for c in range(width // LANES):
            stage_ref[c, pl.ds(r, sub, stride=dil), :] = in_ref[r, :, c * LANES:(c + 1) * LANES].astype(F32)
    return jnp.concatenate([stage_ref[c] for c in range(width // LANES)], axis=1)


def _class_spec(dil, width):
    return pl.BlockSpec((None, dil, ROW_T // dil, width), lambda b, i: (b, 0, i, 0))


def _class_shape(bsz, s, dil, width, dtype):
    return jax.ShapeDtypeStruct((bsz, dil, s // dil, width), dtype)


def _kv_kernel(x_ref, w_ref, cos_ref, sin_ref, k0, k1, k2, v0, v1, v2, stage_ref):
    xb = x_ref[...].astype(BF16)
    cosf, sinf, first_half = _rope_operands(cos_ref, sin_ref, ROW_T)
    for g, (k_ref, v_ref) in enumerate(((k0, v0), (k1, v1), (k2, v2))):
        kg = _dot(xb, w_ref[:, g * B_WIDTH:(g + 1) * B_WIDTH])
        _store_by_residue(_rope(kg, cosf, sinf, first_half), k_ref, stage_ref, B_DILATIONS[g])
        vo = (B_GROUPS + g) * B_WIDTH
        _store_by_residue(_dot(xb, w_ref[:, vo:vo + B_WIDTH]), v_ref, stage_ref, B_DILATIONS[g])


def _kv_project(x, wkv, cos_t, sin_t):
    bsz, s, d = x.shape
    tok = lambda w: pl.BlockSpec((None, ROW_T, w), lambda b, i: (b, i, 0))
    dils = B_DILATIONS * 2
    return pl.pallas_call(
        _kv_kernel,
        grid=(bsz, s // ROW_T),
        in_specs=[tok(d), pl.BlockSpec(wkv.shape, lambda b, i: (0, 0)), tok(LANES), tok(LANES)],
        out_specs=[_class_spec(dl, B_WIDTH) for dl in dils],
        out_shape=[_class_shape(bsz, s, dl, B_WIDTH, BF16) for dl in dils],
        scratch_shapes=[pltpu.VMEM((B_WIDTH // LANES, ROW_T, LANES), F32)],
        compiler_params=_cparams(("parallel", "parallel")),
        name="kv_project",
    )(x, wkv, cos_t, sin_t)


def _q_kernel(x_ref, mod_ref, w_ref, cos_ref, sin_ref, q0, q1, q2, stage_ref):
    md = mod_ref[...]
    hb = (x_ref[...] * (1.0 + md[1:2, :]) + md[0:1, :]).astype(BF16)
    cosf, sinf, first_half = _rope_operands(cos_ref, sin_ref, ROW_T)
    for g, q_ref in enumerate((q0, q1, q2)):
        qg = _dot(hb, w_ref[:, g * B_WIDTH:(g + 1) * B_WIDTH])
        _store_by_residue(_rope(qg, cosf, sinf, first_half) * (B_HEAD_DIM ** -0.5), q_ref, stage_ref,
                          B_DILATIONS[g])


def _q_project(x, mod_l, wq, cos_t, sin_t):
    bsz, s, d = x.shape
    tok = lambda w: pl.BlockSpec((None, ROW_T, w), lambda b, i: (b, i, 0))
    return pl.pallas_call(
        _q_kernel,
        grid=(bsz, s // ROW_T),
        in_specs=[tok(d), pl.BlockSpec((None, 6, d), lambda b, i: (b, 0, 0)),
                  pl.BlockSpec(wq.shape, lambda b, i: (0, 0)), tok(LANES), tok(LANES)],
        out_specs=[_class_spec(dl, B_WIDTH) for dl in B_DILATIONS],
        out_shape=[_class_shape(bsz, s, dl, B_WIDTH, BF16) for dl in B_DILATIONS],
        scratch_shapes=[pltpu.VMEM((B_WIDTH // LANES, ROW_T, LANES), F32)],
        compiler_params=_cparams(("parallel", "parallel")),
        name="q_project",
    )(x, mod_l, wq, cos_t, sin_t)


def _attn_kernel(q_ref, kc_ref, kp_ref, vc_ref, vp_ref, o_ref, lse_ref, kw_ref, vx_ref, *, nsub):
    wb = B_WB
    npair = B_WIDTH // LANES
    first_step = pl.program_id(2) == 0
    kw_ref[0:wb, :] = kp_ref[...]
    kw_ref[wb:, :] = kc_ref[...]
    for p in range(npair):
        vx_ref[p, 0:wb, 0:LANES] = vp_ref[:, p * LANES:(p + 1) * LANES]
        vx_ref[p, wb:, 0:LANES] = vc_ref[:, p * LANES:(p + 1) * LANES]
        vx_ref[p, :, LANES:2 * LANES] = jnp.ones((vx_ref.shape[1], LANES), BF16)
    qi = lax.broadcasted_iota(I32, (wb, 2 * wb), 0)
    kj = lax.broadcasted_iota(I32, (wb, 2 * wb), 1)
    band = (kj >= qi) & (kj <= qi + wb)
    lane = lax.broadcasted_iota(I32, (wb, LANES), 1)
    low = lane < B_HEAD_DIM

    def sub_block(j, carry):
        r0 = pl.multiple_of(j * wb, wb)
        valid = band & ((kj >= wb) | (j > 0) | jnp.logical_not(first_step))
        q = q_ref[pl.ds(r0, wb), :]
        kwin = kw_ref[pl.ds(r0, 2 * wb), :]
        outs = []
        m_all = jnp.zeros((wb, LANES), F32)
        l_all = jnp.ones((wb, LANES), F32)
        for p in range(npair):
            qp = q[:, p * LANES:(p + 1) * LANES]
            kpair = kwin[:, p * LANES:(p + 1) * LANES]
            vx = vx_ref[p, pl.ds(r0, 2 * wb), :]
            acc = lsum = None
            for hh in range(2):
                mine = low if hh == 0 else jnp.logical_not(low)
                qm = jnp.where(mine, qp, jnp.zeros_like(qp))
                sc = jnp.where(valid, _dot_nt(qm, kpair), -jnp.inf)
                m = jnp.max(sc, axis=-1, keepdims=True)
                pv = _dot(jnp.exp(sc - m).astype(BF16), vx)
                o_h, l_h = pv[:, 0:LANES], pv[:, LANES:2 * LANES]
                m_all = jnp.where(lane == 2 * p + hh, m, m_all)
                l_all = jnp.where(lane == 2 * p + hh, l_h, l_all)
                acc = o_h if hh == 0 else jnp.where(low, acc, o_h)
                lsum = l_h if hh == 0 else jnp.where(low, lsum, l_h)
            outs.append(acc * (1.0 / lsum))
        o_ref[pl.ds(r0, wb), :] = jnp.concatenate(outs, axis=1).astype(BF16)
        lse_ref[pl.ds(r0, wb), :] = m_all + jnp.log(l_all)
        return carry

    lax.fori_loop(0, nsub, sub_block, 0)


def _dilated_attention(q, k, v, nsub):
    bsz, dil, rows, w = q.shape
    blk = B_WB * nsub
    cur = lambda width: pl.BlockSpec((None, None, blk, width), lambda b, r, n: (b, r, n, 0))
    prev = pl.BlockSpec((None, None, B_WB, w), lambda b, r, n: (b, r, jnp.maximum(n * nsub - 1, 0), 0))
    return pl.pallas_call(
        functools.partial(_attn_kernel, nsub=nsub),
        grid=(bsz, dil, rows // blk),
        in_specs=[cur(w), cur(w), prev, cur(w), prev],
        out_specs=[cur(w), cur(LANES)],
        out_shape=[jax.ShapeDtypeStruct((bsz, dil, rows, w), BF16),
                   jax.ShapeDtypeStruct((bsz, dil, rows, LANES), F32)],
        scratch_shapes=[pltpu.VMEM((blk + B_WB, w), BF16),
                        pltpu.VMEM((w // LANES, blk + B_WB, 2 * LANES), BF16)],
        compiler_params=_cparams(("parallel", "parallel", "arbitrary")),
        name=f"dilated_attention_d{dil}",
    )(q, k, k, v, v)


def _attn_out_kernel(x_ref, mod_ref, o0, o1, o2, l0, l1, l2, wo_ref, lng_ref, lnb_ref, wr_ref, br_ref,
                     x1_ref, h2_ref, route_ref, stage_ref):
    md = mod_ref[...]
    g1, sh2, sc2 = md[2:3, :], md[3:4, :], md[4:5, :]
    lses = [_load_by_residue(l_ref, stage_ref, dl) for l_ref, dl in zip((l0, l1, l2), B_DILATIONS)]
    mx = jnp.maximum(jnp.maximum(lses[0], lses[1]), lses[2])
    es = [jnp.exp(l - mx) for l in lses]
    inv = 1.0 / (es[0] + es[1] + es[2])

    def per_head(wc):
        return jnp.concatenate([jnp.broadcast_to(wc[:, h:h + 1], (ROW_T, B_HEAD_DIM)) for h in range(B_HEADS)],
                               axis=1)

    o = jnp.zeros((ROW_T, B_WIDTH), F32)
    for o_ref, e, dl in zip((o0, o1, o2), es, B_DILATIONS):
        o = o + per_head(e * inv) * _load_by_residue(o_ref, stage_ref, dl)
    y = _dot(o.astype(BF16), wo_ref[...])
    x1 = _layer_norm(ALPHA * x_ref[...] + (1.0 + g1) * y, lng_ref[...], lnb_ref[...])
    x1_ref[...] = x1
    h2 = x1 * (1.0 + sc2) + sh2
    h2_ref[...] = h2.astype(h2_ref.dtype)
    route_ref[...] = _route_tile(h2, wr_ref, br_ref)


def _attn_out_layer(x, mod_l, outs, lses, wo, lng, lnb, wr, br):
    bsz, s, d = x.shape
    tok = lambda w: pl.BlockSpec((None, ROW_T, w), lambda b, i: (b, i, 0))
    const = lambda b, i: (0, 0)
    return pl.pallas_call(
        _attn_out_kernel,
        grid=(bsz, s // ROW_T),
        in_specs=[tok(d), pl.BlockSpec((None, 6, d), lambda b, i: (b, 0, 0))]
                 + [_class_spec(dl, B_WIDTH) for dl in B_DILATIONS]
                 + [_class_spec(dl, LANES) for dl in B_DILATIONS]
                 + [pl.BlockSpec((B_WIDTH, d), const), pl.BlockSpec((1, d), const), pl.BlockSpec((1, d), const),
                    pl.BlockSpec((2, d, LANES), lambda b, i: (0, 0, 0)), pl.BlockSpec((1, LANES), const)],
        out_specs=[tok(d), tok(d), tok(LANES)],
        out_shape=[jax.ShapeDtypeStruct((bsz, s, d), F32),
                   jax.ShapeDtypeStruct((bsz, s, d), BF16),
                   jax.ShapeDtypeStruct((bsz, s, LANES), F32)],
        scratch_shapes=[pltpu.VMEM((B_WIDTH // LANES, ROW_T, LANES), F32)],
        compiler_params=_cparams(("parallel", "parallel")),
        name="attn_out_layer",
    )(x, mod_l, *outs, *lses, wo, lng, lnb, wr, br)


def _router_weights(w_r1, b_r1, w_r2, b_r2):
    d = w_r1.shape[0]
    n = MOE_GROUPS + MOE_EXPERTS
    wr = jnp.zeros((d, LANES), F32).at[:, :MOE_GROUPS].set(w_r1).at[:, MOE_GROUPS:n].set(w_r2)
    br = jnp.zeros((1, LANES), F32).at[0, :MOE_GROUPS].set(b_r1).at[0, MOE_GROUPS:n].set(b_r2)
    wr_hi = wr.astype(BF16)
    wr_lo = (wr - wr_hi.astype(F32)).astype(BF16)
    return jnp.stack([wr_hi, wr_lo]), br


def kernel(x, c, positions, ada_w, ada_b, ln_g, ln_b, a_w_in, a_b_gate, a_norm_g, a_w_out, b_w_kv, b_w_q, b_w_o,
           moe_w_r1, moe_b_r1, moe_w_r2, moe_b_r2, moe_w_gate, moe_w_up, moe_w_down):
    bsz, s, d = x.shape
    assert d == D_MODEL and s % (B_WB * B_DILATIONS[-1]) == 0 and s % ROW_T == 0 and s % A_STEP == 0 and A_STEP % A_CHUNK == 0
    assert (bsz * s) % SORT_T == 0 and s % SORT_T == 0
    assert 2 * SORT_T + MOE_EXPERTS * (SEG_Q - 1) <= SORT_R and SORT_R % SEG_Q == 0
    mod = _adaln_mod(c, ada_w, ada_b)
    cos_t, sin_t = _rope_tables(positions)
    kv = None
    for l in range(DEPTH):
        lng1, lnb1 = ln_g[l, 0].reshape(1, d), ln_b[l, 0].reshape(1, d)
        lng2, lnb2 = ln_g[l, 1].reshape(1, d), ln_b[l, 1].reshape(1, d)
        wr, br = _router_weights(moe_w_r1[l], moe_b_r1[l], moe_w_r2[l], moe_b_r2[l])
        if l < N_A:
            n_main = 2 * A_QK + 2 * A_V
            win = jnp.zeros((d, n_main + LANES), BF16)
            win = win.at[:, :n_main + 2 * A_HEADS].set(a_w_in[l].astype(BF16))
            bg = jnp.zeros((1, LANES), F32).at[0, :2 * A_HEADS].set(a_b_gate[l])
            x1, h2, route = _mlstm_layer(x, mod[l], win, bg, a_norm_g[l].reshape(1, A_V),
                                         a_w_out[l].astype(BF16), lng1, lnb1, wr, br)
        else:
            lb = l - N_A
            qs = _q_project(x, mod[l], b_w_q[lb].astype(BF16), cos_t, sin_t)
            outs, lses = [], []
            for g in range(B_GROUPS):
                nsub = min(ATT_J[g], s // (B_WB * B_DILATIONS[g]))
                o_g, l_g = _dilated_attention(qs[g], kv[g], kv[B_GROUPS + g], nsub)
                outs.append(o_g)
                lses.append(l_g)
            x1, h2, route = _attn_out_layer(x, mod[l], outs, lses, b_w_o[lb].astype(BF16), lng1, lnb1, wr, br)
        x = _moe_and_norm(x1, h2, route, mod[l], lng2, lnb2, moe_w_gate, moe_w_up, moe_w_down, l)
        if l == N_A - 1:
            kv = _kv_project(x, b_w_kv.astype(BF16), cos_t, sin_t)
    return x
```

```python
import functools

import jax
import jax.numpy as jnp
from jax import lax
from jax.experimental import pallas as pl
from jax.experimental.pallas import tpu as pltpu

F32 = jnp.float32
BF16 = jnp.bfloat16
I32 = jnp.int32
HI = lax.Precision.HIGHEST

D_MODEL = 1024
DEPTH = 4
N_A = DEPTH // 2
ALPHA = (2.0 * DEPTH) ** 0.25
LN_EPS = 1e-5
A_HEADS = 4
A_DQK = D_MODEL // 8
A_DV = D_MODEL // 4
A_NORM_EPS = 1e-6
A_QK = A_HEADS * A_DQK
A_V = A_HEADS * A_DV
B_WINDOWS = (128, 512, 2048)
B_DILATIONS = (1, 4, 16)
B_GROUPS = 3
B_HEADS = 8
B_HEAD_DIM = 64
B_WIDTH = B_HEADS * B_HEAD_DIM
B_WB = 128
ROPE_THETA = 10000.0
MOE_GROUPS = 4
MOE_EPG = 8
MOE_EXPERTS = MOE_GROUPS * MOE_EPG
MOE_HIDDEN = D_MODEL // 4

LANES = 128
SUBLANES = 8
VMEM_LIMIT = 56 * 1024 * 1024

A_CHUNK = 256
A_STEP = 512
SORT_T = 512
SEG_Q = 16
SORT_R = 1536
PERM_BLK = 256
EXP_BM = 512
EXP_CPB = EXP_BM // SEG_Q
LIST_UNROLL = 4
ROW_T = 512
ATT_J = (8, 8, 4)
MOD_TN = 1536


def _cparams(sem):
    return pltpu.CompilerParams(dimension_semantics=sem, vmem_limit_bytes=VMEM_LIMIT)


def _dot(a, b):
    return jnp.dot(a, b, preferred_element_type=F32)


def _dot_nt(a, b):
    return lax.dot_general(a, b, (((1,), (1,)), ((), ())), preferred_element_type=F32)


def _dot_tn(a, b):
    return lax.dot_general(a, b, (((0,), (0,)), ((), ())), preferred_element_type=F32)


def _layer_norm(v, g, b):
    mu = jnp.mean(v, axis=-1, keepdims=True)
    vc = v - mu
    var = jnp.mean(vc * vc, axis=-1, keepdims=True)
    return vc * lax.rsqrt(var + LN_EPS) * g + b


def _sigmoid(v):
    return 1.0 / (1.0 + jnp.exp(-v))


def _split_hi_lo(v):
    hi = v.astype(BF16)
    return hi, (v - hi.astype(F32)).astype(BF16)


def _route_tile(h2, wr_ref, br_ref):
    h_hi, h_lo = _split_hi_lo(h2)
    lg = _dot(h_hi, wr_ref[0]) + (_dot(h_hi, wr_ref[1]) + _dot(h_lo, wr_ref[0])) + br_ref[...]
    lane = lax.broadcasted_iota(I32, lg.shape, 1).astype(F32)
    neg = -jnp.inf
    big = 1000.0
    m1 = jnp.where(lane < MOE_GROUPS, lg, neg)
    mx = jnp.max(m1, axis=-1, keepdims=True)
    pg = 1.0 / jnp.sum(jnp.exp(m1 - mx), axis=-1, keepdims=True)
    gi = jnp.min(jnp.where(m1 == mx, lane, big), axis=-1, keepdims=True)
    lo = MOE_GROUPS + gi * MOE_EPG
    m2 = jnp.where((lane >= lo) & (lane < lo + MOE_EPG), lg, neg)
    v0 = jnp.max(m2, axis=-1, keepdims=True)
    j0 = jnp.min(jnp.where(m2 == v0, lane, big), axis=-1, keepdims=True)
    m3 = jnp.where(lane == j0, neg, m2)
    v1 = jnp.max(m3, axis=-1, keepdims=True)
    j1 = jnp.min(jnp.where(m3 == v1, lane, big), axis=-1, keepdims=True)
    t = jnp.exp(v1 - v0)
    wa = 1.0 / (1.0 + t)
    wb = t * wa
    out = jnp.where(lane == 0, j0 - MOE_GROUPS,
          jnp.where(lane == 1, j1 - MOE_GROUPS,
          jnp.where(lane == 2, pg * wa,
          jnp.where(lane == 3, pg * wb, 0.0))))
    return out


def _mod_kernel(c_ref, w_ref, b_ref, o_ref):
    c = c_ref[...]
    cond = c * _sigmoid(c)
    o_ref[0] = jnp.dot(cond, w_ref[0], preferred_element_type=F32, precision=HI) + b_ref[0]


def _adaln_mod(c, ada_w, ada_b):
    bsz, d = c.shape
    depth, _, n6 = ada_w.shape
    rows = 8
    c_pad = jnp.zeros((rows, d), F32).at[:bsz].set(c)
    out = pl.pallas_call(
        _mod_kernel,
        grid=(depth, n6 // MOD_TN),
        in_specs=[pl.BlockSpec((rows, d), lambda l, j: (0, 0)),
                  pl.BlockSpec((1, d, MOD_TN), lambda l, j: (l, 0, j)),
                  pl.BlockSpec((1, 1, MOD_TN), lambda l, j: (l, 0, j))],
        out_specs=pl.BlockSpec((1, rows, MOD_TN), lambda l, j: (l, 0, j)),
        out_shape=jax.ShapeDtypeStruct((depth, rows, n6), F32),
        compiler_params=_cparams(("parallel", "parallel")),
        name="adaln_mod",
    )(c_pad, ada_w, ada_b.reshape(depth, 1, n6))
    return out[:, :bsz].reshape(depth, bsz, 6, d)


def _mlstm_layer_kernel(x_ref, mod_ref, win_ref, bg_ref, ng_ref, wout_ref, lng_ref, lnb_ref,
                        wr_ref, br_ref, x1_ref, h2_ref, route_ref, c_ref, n_ref, m_ref):
    L = A_CHUNK
    n_chunks = A_STEP // L

    @pl.when(pl.program_id(1) == 0)
    def _():
        c_ref[...] = jnp.zeros_like(c_ref)
        n_ref[...] = jnp.zeros_like(n_ref)
        m_ref[...] = jnp.zeros_like(m_ref)

    md = mod_ref[...]
    sh1, sc1, g1, sh2, sc2, g2 = [md[j:j + 1, :] for j in range(6)]
    o_q, o_k, o_v, o_o, o_g = 0, A_QK, 2 * A_QK, 2 * A_QK + A_V, 2 * A_QK + 2 * A_V

    row = lax.broadcasted_iota(I32, (L, L), 0)
    col = lax.broadcasted_iota(I32, (L, L), 1)
    causal = row >= col
    tri = jnp.where(causal, 1.0, 0.0).astype(BF16)

    def proj_parts(ci):
        rows = slice(ci * L, (ci + 1) * L)
        out = {}

        def start():
            out["hb"] = (x_ref[rows, :] * (1.0 + sc1) + sh1).astype(BF16)
            out["q"] = _dot(out["hb"], win_ref[:, o_q:o_k])

        def kpart():
            out["k"] = _dot(out["hb"], win_ref[:, o_k:o_v]) * (A_DQK ** -0.5)

        def vpart():
            out["v"] = _dot(out["hb"], win_ref[:, o_v:o_o])

        def opart():
            out["og"] = _dot(out["hb"], win_ref[:, o_o:o_g])
            out["gates"] = _dot(out["hb"], win_ref[:, o_g:o_g + LANES]) + bg_ref[...]

        return out, [start, kpart, vpart, opart]

    def gate_terms(pr):
        gates = pr["gates"]
        lf = jnp.minimum(gates, 0.0) - jnp.log(1.0 + jnp.exp(-jnp.abs(gates)))
        lf_hi, lf_lo = _split_hi_lo(lf)
        lf_lo2 = (lf - lf_hi.astype(F32) - lf_lo.astype(F32)).astype(BF16)
        b_col = _dot(tri, lf_hi) + (_dot(tri, lf_lo) + _dot(tri, lf_lo2))
        return gates, b_col, gates.T, b_col.T

    def head(pr, gt, h):
        gates, b_col, g_t, b_t = gt
        qh = pr["q"][:, h * A_DQK:(h + 1) * A_DQK]
        kh = pr["k"][:, h * A_DQK:(h + 1) * A_DQK]
        vh = pr["v"][:, h * A_DV:(h + 1) * A_DV]
        qb, kb, vb = qh.astype(BF16), kh.astype(BF16), vh.astype(BF16)
        fl = A_HEADS + h
        b_c = b_col[:, fl:fl + 1]
        i_c = gates[:, h:h + 1]
        b_r = b_t[fl:fl + 1, :]
        i_r = g_t[h:h + 1, :]
        m_st = m_ref[h:h + 1, 0:1]
        c_st = c_ref[h]
        n_st = n_ref[h:h + 1, :]

        dmat = jnp.where(causal, b_c - b_r + i_r, -jnp.inf)
        inter = b_c + m_st
        m_t = jnp.maximum(inter, jnp.max(dmat, axis=-1, keepdims=True))
        dw = jnp.exp(dmat - m_t)
        iw = jnp.exp(inter - m_t)
        sc = _dot_nt(qb, kb) * dw
        num = _dot(sc.astype(BF16), vb) + iw * _dot(qb, c_st.astype(BF16))
        den = jnp.sum(sc, axis=-1, keepdims=True) + iw * jnp.sum(qh * n_st, axis=-1, keepdims=True)
        hh = num * (1.0 / jnp.maximum(jnp.abs(den), jnp.exp(-m_t)))

        b_last = b_col[L - 1:L, fl:fl + 1]
        ws_log = b_last - b_c + i_c
        m_new = jnp.maximum(b_last + m_st, jnp.max(ws_log, axis=0, keepdims=True))
        ws = jnp.exp(ws_log - m_new)
        decay = jnp.exp(b_last + m_st - m_new)
        kw = kh * ws
        c_ref[h] = decay * c_st + _dot_tn(kw.astype(BF16), vb)
        n_ref[h:h + 1, :] = decay * n_st + jnp.sum(kw, axis=0, keepdims=True)
        m_ref[h:h + 1, :] = jnp.broadcast_to(m_new, (1, LANES))

        mu = jnp.mean(hh, axis=-1, keepdims=True)
        hc = hh - mu
        var = jnp.mean(hc * hc, axis=-1, keepdims=True)
        hn = hc * lax.rsqrt(var + A_NORM_EPS)
        og = pr["og"][:, h * A_DV:(h + 1) * A_DV]
        return (hn * ng_ref[:, h * A_DV:(h + 1) * A_DV] * _sigmoid(og)).astype(BF16)

    def tail_parts(ci, ys):
        rows = slice(ci * L, (ci + 1) * L)
        st = {}

        def outproj():
            st["y"] = _dot(jnp.concatenate(ys, axis=1), wout_ref[...])

        def norm():
            x1 = _layer_norm(ALPHA * x_ref[rows, :] + (1.0 + g1) * st["y"], lng_ref[...], lnb_ref[...])
            x1_ref[rows, :] = x1
            st["h2"] = x1 * (1.0 + sc2) + sh2
            h2_ref[rows, :] = st["h2"].astype(h2_ref.dtype)

        def router():
            route_ref[rows, :] = _route_tile(st["h2"], wr_ref, br_ref)

        return [outproj, norm, router]

    pr, parts = proj_parts(0)
    for part in parts:
        part()
    pending_tail = []
    for ci in range(n_chunks):
        nxt, nxt_parts = proj_parts(ci + 1) if ci + 1 < n_chunks else (None, [])
        fill = nxt_parts + pending_tail
        gt = gate_terms(pr)
        ys = []
        for h in range(A_HEADS):
            ys.append(head(pr, gt, h))
            share = -(-len(fill) // (A_HEADS - h))
            for part in fill[:share]:
                part()
            fill = fill[share:]
        pending_tail = tail_parts(ci, ys)
        pr = nxt
    for part in pending_tail:
        part()


def _mlstm_layer(x, mod_l, win, bg, ng, wout, lng, lnb, wr, br):
    bsz, s, d = x.shape
    nw = win.shape[1]
    const = lambda b, i: (0, 0)
    tok = pl.BlockSpec((None, A_STEP, d), lambda b, i: (b, i, 0))
    return pl.pallas_call(
        _mlstm_layer_kernel,
        grid=(bsz, s // A_STEP),
        in_specs=[tok,
                  pl.BlockSpec((None, 6, d), lambda b, i: (b, 0, 0)),
                  pl.BlockSpec((d, nw), const),
                  pl.BlockSpec((1, LANES), const),
                  pl.BlockSpec((1, A_V), const),
                  pl.BlockSpec((A_V, d), const),
                  pl.BlockSpec((1, d), const),
                  pl.BlockSpec((1, d), const),
                  pl.BlockSpec((2, d, LANES), lambda b, i: (0, 0, 0)),
                  pl.BlockSpec((1, LANES), const)],
        out_specs=[tok, tok, pl.BlockSpec((None, A_STEP, LANES), lambda b, i: (b, i, 0))],
        out_shape=[jax.ShapeDtypeStruct((bsz, s, d), F32),
                   jax.ShapeDtypeStruct((bsz, s, d), BF16),
                   jax.ShapeDtypeStruct((bsz, s, LANES), F32)],
        scratch_shapes=[pltpu.VMEM((A_HEADS, A_DQK, A_DV), F32),
                        pltpu.VMEM((8, A_DQK), F32),
                        pltpu.VMEM((8, LANES), F32)],
        compiler_params=_cparams(("parallel", "arbitrary")),
        name="mlstm_layer",
    )(x, mod_l, win, bg, ng, wout, lng, lnb, wr, br)


def _moe_sort_kernel(h2_ref, route_ref, xs_ref, ws_ref, pos_ref, cnt_ref):
    n, nr = SORT_T, SORT_R
    r = route_ref[...]
    lane = lax.broadcasted_iota(I32, (n, LANES), 1).astype(F32)
    oh0 = lane == r[:, 0:1]
    oh1 = lane == r[:, 1:2]
    oh = jnp.where(oh0 | oh1, 1.0, 0.0)
    row = lax.broadcasted_iota(I32, (n, n), 0)
    col = lax.broadcasted_iota(I32, (n, n), 1)
    before = jnp.where(row > col, 1.0, 0.0).astype(BF16)
    excl = _dot(before, oh.astype(BF16))
    cnt = jnp.sum(oh, axis=0, keepdims=True)
    pcnt = jnp.ceil(cnt * (1.0 / SEG_Q)) * SEG_Q
    ej = lax.broadcasted_iota(I32, (LANES, LANES), 0)
    el = lax.broadcasted_iota(I32, (LANES, LANES), 1)
    upper = jnp.where(ej < el, 1.0, 0.0).astype(F32)
    seg = jnp.dot(jnp.broadcast_to(pcnt, (SUBLANES, LANES)), upper,
                  preferred_element_type=F32, precision=HI)[0:1, :]
    base = excl + seg
    p0 = jnp.sum(jnp.where(oh0, base, 0.0), axis=-1, keepdims=True)
    p1 = jnp.sum(jnp.where(oh1, base, 0.0), axis=-1, keepdims=True)
    pos = jnp.where(lane == 0, p0, jnp.where(lane == 1, p1, 0.0))
    pos_ref[...] = pos
    cnt_ref[...] = cnt
    pos_t = pos.T

    def split3(w):
        hi = w.astype(BF16)
        r1 = w - hi.astype(F32)
        mid = r1.astype(BF16)
        lo = (r1 - mid.astype(F32)).astype(BF16)
        return jnp.where(lane == 0, hi.astype(F32),
               jnp.where(lane == 1, mid.astype(F32),
               jnp.where(lane == 2, lo.astype(F32), 0.0))).astype(BF16)

    w0, w1 = split3(r[:, 2:3]), split3(r[:, 3:4])
    h2 = h2_ref[...]
    for kb in range(nr // PERM_BLK):
        rows = slice(kb * PERM_BLK, (kb + 1) * PERM_BLK)
        ri = (lax.broadcasted_iota(I32, (PERM_BLK, n), 0) + kb * PERM_BLK).astype(F32)
        sel0 = ri == pos_t[0:1, :]
        sel1 = ri == pos_t[1:2, :]
        perm = jnp.where(sel0 | sel1, 1.0, 0.0).astype(BF16)
        xs_ref[rows, :] = _dot(perm, h2).astype(BF16)
        wsum = (_dot(jnp.where(sel0, 1.0, 0.0).astype(BF16), w0)
                + _dot(jnp.where(sel1, 1.0, 0.0).astype(BF16), w1))
        ws = wsum[:, 0:1] + wsum[:, 1:2] + wsum[:, 2:3]
        ws_ref[rows, :] = jnp.broadcast_to(ws, (PERM_BLK, LANES))


def _chunk_list_kernel(seg_ref, nch_ref, src_ref, be_ref, bc_ref, nb_ref, *, n_tiles):
    n_src = src_ref.shape[0]
    n_blk = be_ref.shape[0]

    def clear_src(k, c):
        src_ref[k] = 0
        return c

    lax.fori_loop(0, n_src, clear_src, 0, unroll=16)

    def per_expert(e, carry):
        k0, nb0 = carry

        def per_tile(i, k):
            s = seg_ref[i * MOE_EXPERTS + e]
            n = nch_ref[i * MOE_EXPERTS + e]

            row0 = i * SORT_R + s
            for c in range(LIST_UNROLL):
                @pl.when(c < n)
                def _(c=c):
                    src_ref[k + c] = row0 + c * SEG_Q

            def per_chunk(c, carry):
                src_ref[k + c] = row0 + c * SEG_Q
                return carry

            lax.fori_loop(LIST_UNROLL, n, per_chunk, 0)
            return k + n

        k1 = lax.fori_loop(0, n_tiles, per_tile, k0)
        tot = k1 - k0
        nblk = (tot + EXP_CPB - 1) // EXP_CPB

        def per_block(j, c):
            be_ref[nb0 + j] = e
            bc_ref[nb0 + j] = jnp.minimum(EXP_CPB, tot - j * EXP_CPB)
            return c

        lax.fori_loop(0, nblk, per_block, 0)
        return k0 + nblk * EXP_CPB, nb0 + nblk

    _, nb = lax.fori_loop(0, MOE_EXPERTS, per_expert, (jnp.int32(0), jnp.int32(0)))
    nb_ref[0] = nb

    def spare_block(j, c):
        be_ref[j] = MOE_EXPERTS - 1
        bc_ref[j] = 0
        return c

    lax.fori_loop(nb, n_blk, spare_block, 0)


def _chunk_copy(src_ref, src_row, dst_ref, dst_row, sem):
    return pltpu.make_async_copy(src_ref.at[pl.ds(pl.multiple_of(src_row, SEG_Q), SEG_Q), :],
                                 dst_ref.at[pl.ds(pl.multiple_of(dst_row, SEG_Q), SEG_Q), :], sem)


def _expert_kernel(src_ref, be_ref, bc_ref, nb_ref, wg_ref, wu_ref, wd_ref, xs_ref, ys_ref,
                   wgu_s, wd_s, xg, yg, sem_in, sem_out):
    b = pl.program_id(0)
    last = pl.num_programs(0) - 1
    nb = nb_ref[0]

    def clamp(blk):
        return jnp.clip(blk, 0, last)

    def count(blk):
        return jnp.where((blk >= 0) & (blk <= last), bc_ref[clamp(blk)], 0)

    def per_chunk(n, fn):
        for c in range(EXP_CPB):
            @pl.when(c < n)
            def _(c=c):
                fn(c)

    def gather(blk, slot):
        per_chunk(count(blk), lambda c: _chunk_copy(
            xs_ref, src_ref[clamp(blk) * EXP_CPB + c], xg.at[slot], c * SEG_Q, sem_in.at[slot]).start())

    def wait_gather(blk, slot):
        per_chunk(count(blk), lambda c: _chunk_copy(xs_ref, 0, xg.at[slot], 0, sem_in.at[slot]).wait())

    def scatter(blk, slot):
        per_chunk(count(blk), lambda c: _chunk_copy(
            yg.at[slot], c * SEG_Q, ys_ref, src_ref[clamp(blk) * EXP_CPB + c], sem_out.at[slot]).start())

    def wait_scatter(blk, slot):
        per_chunk(count(blk), lambda c: _chunk_copy(yg.at[slot], 0, ys_ref, 0, sem_out.at[slot]).wait())

    @pl.when(b == 0)
    def _():
        xg[...] = jnp.zeros_like(xg)
        gather(0, 0)

    @pl.when((b == 0) | (be_ref[b] != be_ref[jnp.maximum(b - 1, 0)]))
    def _():
        wgu_s[:, 0:MOE_HIDDEN] = wg_ref[...].astype(BF16)
        wgu_s[:, MOE_HIDDEN:2 * MOE_HIDDEN] = wu_ref[...].astype(BF16)
        wd_s[...] = wd_ref[...].astype(BF16)

    @pl.when(b < nb)
    def _():
        slot = b & 1
        gather(b + 1, 1 - slot)
        wait_gather(b, slot)
        gu = _dot(xg[slot], wgu_s[...])
        g = gu[:, 0:MOE_HIDDEN]
        u = gu[:, MOE_HIDDEN:2 * MOE_HIDDEN]
        y = _dot((g * _sigmoid(g) * u).astype(BF16), wd_s[...])
        wait_scatter(b - 2, slot)
        yg[slot] = y.astype(BF16)
        scatter(b, slot)

    @pl.when(b == last)
    def _():
        wait_scatter(nb - 2, nb & 1)
        wait_scatter(nb - 1, (nb - 1) & 1)


def _combine_kernel(ys_ref, ws_ref, pos_ref, x1_ref, mod_ref, lng_ref, lnb_ref, x2_ref):
    n, nr = SORT_T, SORT_R
    pos = pos_ref[...]
    y = jnp.zeros((n, D_MODEL), F32)
    for kb in range(nr // PERM_BLK):
        rows = slice(kb * PERM_BLK, (kb + 1) * PERM_BLK)
        ysw = (ys_ref[rows, :].astype(F32) * ws_ref[rows, 0:1]).astype(BF16)
        ci = (lax.broadcasted_iota(I32, (n, PERM_BLK), 1) + kb * PERM_BLK).astype(F32)
        unperm = jnp.where((ci == pos[:, 0:1]) | (ci == pos[:, 1:2]), 1.0, 0.0).astype(BF16)
        y = y + _dot(unperm, ysw)
    g2 = mod_ref[5:6, :]
    x2_ref[...] = _layer_norm(ALPHA * x1_ref[...] + (1.0 + g2) * y, lng_ref[...], lnb_ref[...])


def _moe_and_norm(x1, h2, route, mod_l, lng, lnb, wg, wu, wd, layer):
    bsz, s, d = x1.shape
    n_tok = bsz * s
    nt = n_tok // SORT_T
    n_rows = nt * SORT_R
    x1f = x1.reshape(n_tok, d)
    h2f = h2.reshape(n_tok, d)
    rt = route.reshape(n_tok, LANES)

    xs, ws, pos, cnt = pl.pallas_call(
        _moe_sort_kernel,
        grid=(nt,),
        in_specs=[pl.BlockSpec((SORT_T, d), lambda i: (i, 0)),
                  pl.BlockSpec((SORT_T, LANES), lambda i: (i, 0))],
        out_specs=[pl.BlockSpec((SORT_R, d), lambda i: (i, 0)),
                   pl.BlockSpec((SORT_R, LANES), lambda i: (i, 0)),
                   pl.BlockSpec((SORT_T, LANES), lambda i: (i, 0)),
                   pl.BlockSpec((None, 1, LANES), lambda i: (i, 0, 0))],
        out_shape=[jax.ShapeDtypeStruct((n_rows, d), BF16),
                   jax.ShapeDtypeStruct((n_rows, LANES), F32),
                   jax.ShapeDtypeStruct((n_tok, LANES), F32),
                   jax.ShapeDtypeStruct((nt, 1, LANES), F32)],
        compiler_params=_cparams(("parallel",)),
        name="moe_sort",
    )(h2f, rt)

    cnt_i = cnt[:, 0, :MOE_EXPERTS].astype(I32)
    pcnt = (cnt_i + SEG_Q - 1) // SEG_Q * SEG_Q
    seg = (jnp.cumsum(pcnt, axis=1) - pcnt).reshape(-1).astype(I32)
    nch = (pcnt // SEG_Q).reshape(-1).astype(I32)

    n_src = n_rows // SEG_Q + MOE_EXPERTS * EXP_CPB
    n_blk = n_src // EXP_CPB
    smem = pl.BlockSpec(memory_space=pltpu.SMEM)
    src, block_e, block_n, n_used = pl.pallas_call(
        functools.partial(_chunk_list_kernel, n_tiles=nt),
        in_specs=[smem, smem],
        out_specs=[smem, smem, smem, smem],
        out_shape=[jax.ShapeDtypeStruct((n_src,), I32),
                   jax.ShapeDtypeStruct((n_blk,), I32),
                   jax.ShapeDtypeStruct((n_blk,), I32),
                   jax.ShapeDtypeStruct((1,), I32)],
        name="moe_chunk_list",
    )(seg, nch)

    ys = pl.pallas_call(
        _expert_kernel,
        grid_spec=pltpu.PrefetchScalarGridSpec(
            num_scalar_prefetch=4,
            grid=(n_blk,),
            in_specs=[pl.BlockSpec((None, None, d, MOE_HIDDEN), lambda b, src, be, bc, nb: (layer, be[b], 0, 0)),
                      pl.BlockSpec((None, None, d, MOE_HIDDEN), lambda b, src, be, bc, nb: (layer, be[b], 0, 0)),
                      pl.BlockSpec((None, None, MOE_HIDDEN, d), lambda b, src, be, bc, nb: (layer, be[b], 0, 0)),
                      pl.BlockSpec(memory_space=pl.ANY)],
            out_specs=pl.BlockSpec(memory_space=pl.ANY),
            scratch_shapes=[pltpu.VMEM((d, 2 * MOE_HIDDEN), BF16),
                            pltpu.VMEM((MOE_HIDDEN, d), BF16),
                            pltpu.VMEM((2, EXP_BM, d), BF16),
                            pltpu.VMEM((2, EXP_BM, d), BF16),
                            pltpu.SemaphoreType.DMA((2,)),
                            pltpu.SemaphoreType.DMA((2,))]),
        out_shape=jax.ShapeDtypeStruct((n_rows, d), BF16),
        input_output_aliases={7: 0},
        compiler_params=_cparams(("arbitrary",)),
        name="moe_experts",
    )(src, block_e, block_n, n_used, wg, wu, wd, xs)

    per_b = s // SORT_T
    x2 = pl.pallas_call(
        _combine_kernel,
        grid=(nt,),
        in_specs=[pl.BlockSpec((SORT_R, d), lambda i: (i, 0)),
                  pl.BlockSpec((SORT_R, LANES), lambda i: (i, 0)),
                  pl.BlockSpec((SORT_T, LANES), lambda i: (i, 0)),
                  pl.BlockSpec((SORT_T, d), lambda i: (i, 0)),
                  pl.BlockSpec((None, 6, d), lambda i: (i // per_b, 0, 0)),
                  pl.BlockSpec((1, d), lambda i: (0, 0)),
                  pl.BlockSpec((1, d), lambda i: (0, 0))],
        out_specs=pl.BlockSpec((SORT_T, d), lambda i: (i, 0)),
        out_shape=jax.ShapeDtypeStruct((n_tok, d), F32),
        compiler_params=_cparams(("parallel",)),
        name="moe_combine_norm",
    )(ys, ws, pos, x1f, mod_l, lng, lnb)
    return x2.reshape(bsz, s, d)


def _rope_table_kernel(pos_ref, inv_ref, sgn_ref, cos_ref, sin_ref):
    ang = pos_ref[...] * inv_ref[...]
    cos_ref[...] = jnp.cos(ang)
    sin_ref[...] = jnp.sin(ang) * sgn_ref[...]


def _rope_tables(positions):
    bsz, s = positions.shape
    half = B_HEAD_DIM // 2
    inv = jnp.power(jnp.float32(ROPE_THETA), -jnp.arange(half, dtype=F32) * 2.0 / B_HEAD_DIM)
    inv_row = jnp.tile(inv, LANES // half).reshape(1, LANES)
    sgn_row = jnp.tile(jnp.concatenate([-jnp.ones((half,), F32), jnp.ones((half,), F32)]),
                       LANES // B_HEAD_DIM).reshape(1, LANES)
    pos = jnp.broadcast_to(positions.astype(F32)[:, :, None], (bsz, s, LANES))
    tok = pl.BlockSpec((None, ROW_T, LANES), lambda b, i: (b, i, 0))
    row = pl.BlockSpec((1, LANES), lambda b, i: (0, 0))
    return pl.pallas_call(
        _rope_table_kernel,
        grid=(bsz, s // ROW_T),
        in_specs=[tok, row, row],
        out_specs=[tok, tok],
        out_shape=[jax.ShapeDtypeStruct((bsz, s, LANES), F32)] * 2,
        compiler_params=_cparams(("parallel", "parallel")),
        name="rope_tables",
    )(pos, inv_row, sgn_row)


def _rope(t, cosf, sinf, first_half):
    n = t.shape[1]
    half = B_HEAD_DIM // 2
    fwd = pltpu.roll(t, n - half, 1)
    bwd = pltpu.roll(t, half, 1)
    return t * cosf + jnp.where(first_half, fwd, bwd) * sinf


def _rope_operands(cos_ref, sin_ref, rows):
    reps = B_WIDTH // LANES
    cosf = jnp.concatenate([cos_ref[...]] * reps, axis=1)
    sinf = jnp.concatenate([sin_ref[...]] * reps, axis=1)
    lane = lax.broadcasted_iota(I32, (rows, B_WIDTH), 1)
    first_half = (lane & (B_HEAD_DIM - 1)) < (B_HEAD_DIM // 2)
    return cosf, sinf, first_half


def _store_by_residue(val, out_ref, stage_ref, dil):
    rows, width = val.shape
    if dil == 1:
        out_ref[0] = val.astype(out_ref.dtype)
        return
    for c in range(width // LANES):
        stage_ref[c] = val[:, c * LANES:(c + 1) * LANES]
    for r in range(dil):
        for c in range(width // LANES):
            out_ref[r, :, c * LANES:(c + 1) * LANES] = (
                stage_ref[c, pl.ds(r, rows // dil, stride=dil), :].astype(out_ref.dtype))


def _load_by_residue(in_ref, stage_ref, dil):
    _, sub, width = in_ref.shape
    if dil == 1:
        return in_ref[0].astype(F32)
    for r in range(dil):
        for c in range(width // LANES):
            stage_ref[c, pl.ds(r, sub, stride=dil), :] = in_ref[r, :, c * LANES:(c + 1) * LANES].astype(F32)
    return jnp.concatenate([stage_ref[c] for c in range(width // LANES)], axis=1)


def _class_spec(dil, width):
    return pl.BlockSpec((None, dil, ROW_T // dil, width), lambda b, i: (b, 0, i, 0))


def _class_shape(bsz, s, dil, width, dtype):
    return jax.ShapeDtypeStruct((bsz, dil, s // dil, width), dtype)


def _kv_kernel(x_ref, w_ref, cos_ref, sin_ref, k0, k1, k2, v0, v1, v2, stage_ref):
    xb = x_ref[...].astype(BF16)
    cosf, sinf, first_half = _rope_operands(cos_ref, sin_ref, ROW_T)
    for g, (k_ref, v_ref) in enumerate(((k0, v0), (k1, v1), (k2, v2))):
        kg = _dot(xb, w_ref[:, g * B_WIDTH:(g + 1) * B_WIDTH])
        _store_by_residue(_rope(kg, cosf, sinf, first_half), k_ref, stage_ref, B_DILATIONS[g])
        vo = (B_GROUPS + g) * B_WIDTH
        _store_by_residue(_dot(xb, w_ref[:, vo:vo + B_WIDTH]), v_ref, stage_ref, B_DILATIONS[g])


def _kv_project(x, wkv, cos_t, sin_t):
    bsz, s, d = x.shape
    tok = lambda w: pl.BlockSpec((None, ROW_T, w), lambda b, i: (b, i, 0))
    dils = B_DILATIONS * 2
    return pl.pallas_call(
        _kv_kernel,
        grid=(bsz, s // ROW_T),
        in_specs=[tok(d), pl.BlockSpec(wkv.shape, lambda b, i: (0, 0)), tok(LANES), tok(LANES)],
        out_specs=[_class_spec(dl, B_WIDTH) for dl in dils],
        out_shape=[_class_shape(bsz, s, dl, B_WIDTH, BF16) for dl in dils],
        scratch_shapes=[pltpu.VMEM((B_WIDTH // LANES, ROW_T, LANES), F32)],
        compiler_params=_cparams(("parallel", "parallel")),
        name="kv_project",
    )(x, wkv, cos_t, sin_t)


def _q_kernel(x_ref, mod_ref, w_ref, cos_ref, sin_ref, q0, q1, q2, stage_ref):
    md = mod_ref[...]
    hb = (x_ref[...] * (1.0 + md[1:2, :]) + md[0:1, :]).astype(BF16)
    cosf, sinf, first_half = _rope_operands(cos_ref, sin_ref, ROW_T)
    for g, q_ref in enumerate((q0, q1, q2)):
        qg = _dot(hb, w_ref[:, g * B_WIDTH:(g + 1) * B_WIDTH])
        _store_by_residue(_rope(qg, cosf, sinf, first_half) * (B_HEAD_DIM ** -0.5), q_ref, stage_ref,
                          B_DILATIONS[g])


def _q_project(x, mod_l, wq, cos_t, sin_t):
    bsz, s, d = x.shape
    tok = lambda w: pl.BlockSpec((None, ROW_T, w), lambda b, i: (b, i, 0))
    return pl.pallas_call(
        _q_kernel,
        grid=(bsz, s // ROW_T),
        in_specs=[tok(d), pl.BlockSpec((None, 6, d), lambda b, i: (b, 0, 0)),
                  pl.BlockSpec(wq.shape, lambda b, i: (0, 0)), tok(LANES), tok(LANES)],
        out_specs=[_class_spec(dl, B_WIDTH) for dl in B_DILATIONS],
        out_shape=[_class_shape(bsz, s, dl, B_WIDTH, BF16) for dl in B_DILATIONS],
        scratch_shapes=[pltpu.VMEM((B_WIDTH // LANES, ROW_T, LANES), F32)],
        compiler_params=_cparams(("parallel", "parallel")),
        name="q_project",
    )(x, mod_l, wq, cos_t, sin_t)


def _attn_kernel(q_ref, kc_ref, kp_ref, vc_ref, vp_ref, o_ref, lse_ref, kw_ref, vx_ref, *, nsub):
    wb = B_WB
    npair = B_WIDTH // LANES
    first_step = pl.program_id(2) == 0
    kw_ref[0:wb, :] = kp_ref[...]
    kw_ref[wb:, :] = kc_ref[...]
    for p in range(npair):
        vx_ref[p, 0:wb, 0:LANES] = vp_ref[:, p * LANES:(p + 1) * LANES]
        vx_ref[p, wb:, 0:LANES] = vc_ref[:, p * LANES:(p + 1) * LANES]
        vx_ref[p, :, LANES:2 * LANES] = jnp.ones((vx_ref.shape[1], LANES), BF16)
    qi = lax.broadcasted_iota(I32, (wb, 2 * wb), 0)
    kj = lax.broadcasted_iota(I32, (wb, 2 * wb), 1)
    band = (kj >= qi) & (kj <= qi + wb)
    lane = lax.broadcasted_iota(I32, (wb, LANES), 1)
    low = lane < B_HEAD_DIM

    for j in range(nsub):
        r0 = j * wb
        valid = band & ((kj >= wb) | jnp.logical_not(first_step)) if j == 0 else band
        q = q_ref[pl.ds(r0, wb), :]
        kwin = kw_ref[pl.ds(r0, 2 * wb), :]
        outs = []
        m_all = jnp.zeros((wb, LANES), F32)
        l_all = jnp.ones((wb, LANES), F32)
        for p in range(npair):
            qp = q[:, p * LANES:(p + 1) * LANES]
            kpair = kwin[:, p * LANES:(p + 1) * LANES]
            vx = vx_ref[p, pl.ds(r0, 2 * wb), :]
            acc = lsum = None
            for hh in range(2):
                mine = low if hh == 0 else jnp.logical_not(low)
                qm = jnp.where(mine, qp, jnp.zeros_like(qp))
                sc = jnp.where(valid, _dot_nt(qm, kpair), -jnp.inf)
                m = jnp.max(sc, axis=-1, keepdims=True)
                pv = _dot(jnp.exp(sc - m).astype(BF16), vx)
                o_h, l_h = pv[:, 0:LANES], pv[:, LANES:2 * LANES]
                m_all = jnp.where(lane == 2 * p + hh, m, m_all)
                l_all = jnp.where(lane == 2 * p + hh, l_h, l_all)
                acc = o_h if hh == 0 else jnp.where(low, acc, o_h)
                lsum = l_h if hh == 0 else jnp.where(low, lsum, l_h)
            outs.append(acc * (1.0 / lsum))
        o_ref[pl.ds(r0, wb), :] = jnp.concatenate(outs, axis=1).astype(BF16)
        lse_ref[pl.ds(r0, wb), :] = m_all + jnp.log(l_all)


def _dilated_attention(q, k, v, nsub):
    bsz, dil, rows, w = q.shape
    blk = B_WB * nsub
    cur = lambda width: pl.BlockSpec((None, None, blk, width), lambda b, r, n: (b, r, n, 0))
    prev = pl.BlockSpec((None, None, B_WB, w), lambda b, r, n: (b, r, jnp.maximum(n * nsub - 1, 0), 0))
    return pl.pallas_call(
        functools.partial(_attn_kernel, nsub=nsub),
        grid=(bsz, dil, rows // blk),
        in_specs=[cur(w), cur(w), prev, cur(w), prev],
        out_specs=[cur(w), cur(LANES)],
        out_shape=[jax.ShapeDtypeStruct((bsz, dil, rows, w), BF16),
                   jax.ShapeDtypeStruct((bsz, dil, rows, LANES), F32)],
        scratch_shapes=[pltpu.VMEM((blk + B_WB, w), BF16),
                        pltpu.VMEM((w // LANES, blk + B_WB, 2 * LANES), BF16)],
        compiler_params=_cparams(("parallel", "parallel", "arbitrary")),
        name=f"dilated_attention_d{dil}",
    )(q, k, k, v, v)


def _attn_out_kernel(x_ref, mod_ref, o0, o1, o2, l0, l1, l2, wo_ref, lng_ref, lnb_ref, wr_ref, br_ref,
                     x1_ref, h2_ref, route_ref, stage_ref):
    md = mod_ref[...]
    g1, sh2, sc2 = md[2:3, :], md[3:4, :], md[4:5, :]
    lses = [_load_by_residue(l_ref, stage_ref, dl) for l_ref, dl in zip((l0, l1, l2), B_DILATIONS)]
    mx = jnp.maximum(jnp.maximum(lses[0], lses[1]), lses[2])
    es = [jnp.exp(l - mx) for l in lses]
    inv = 1.0 / (es[0] + es[1] + es[2])

    def per_head(wc):
        return jnp.concatenate([jnp.broadcast_to(wc[:, h:h + 1], (ROW_T, B_HEAD_DIM)) for h in range(B_HEADS)],
                               axis=1)

    o = jnp.zeros((ROW_T, B_WIDTH), F32)
    for o_ref, e, dl in zip((o0, o1, o2), es, B_DILATIONS):
        o = o + per_head(e * inv) * _load_by_residue(o_ref, stage_ref, dl)
    y = _dot(o.astype(BF16), wo_ref[...])
    x1 = _layer_norm(ALPHA * x_ref[...] + (1.0 + g1) * y, lng_ref[...], lnb_ref[...])
    x1_ref[...] = x1
    h2 = x1 * (1.0 + sc2) + sh2
    h2_ref[...] = h2.astype(h2_ref.dtype)
    route_ref[...] = _route_tile(h2, wr_ref, br_ref)


def _attn_out_layer(x, mod_l, outs, lses, wo, lng, lnb, wr, br):
    bsz, s, d = x.shape
    tok = lambda w: pl.BlockSpec((None, ROW_T, w), lambda b, i: (b, i, 0))
    const = lambda b, i: (0, 0)
    return pl.pallas_call(
        _attn_out_kernel,
        grid=(bsz, s // ROW_T),
        in_specs=[tok(d), pl.BlockSpec((None, 6, d), lambda b, i: (b, 0, 0))]
                 + [_class_spec(dl, B_WIDTH) for dl in B_DILATIONS]
                 + [_class_spec(dl, LANES) for dl in B_DILATIONS]
                 + [pl.BlockSpec((B_WIDTH, d), const), pl.BlockSpec((1, d), const), pl.BlockSpec((1, d), const),
                    pl.BlockSpec((2, d, LANES), lambda b, i: (0, 0, 0)), pl.BlockSpec((1, LANES), const)],
        out_specs=[tok(d), tok(d), tok(LANES)],
        out_shape=[jax.ShapeDtypeStruct((bsz, s, d), F32),
                   jax.ShapeDtypeStruct((bsz, s, d), BF16),
                   jax.ShapeDtypeStruct((bsz, s, LANES), F32)],
        scratch_shapes=[pltpu.VMEM((B_WIDTH // LANES, ROW_T, LANES), F32)],
        compiler_params=_cparams(("parallel", "parallel")),
        name="attn_out_layer",
    )(x, mod_l, *outs, *lses, wo, lng, lnb, wr, br)


def _router_weights(w_r1, b_r1, w_r2, b_r2):
    d = w_r1.shape[0]
    n = MOE_GROUPS + MOE_EXPERTS
    wr = jnp.zeros((d, LANES), F32).at[:, :MOE_GROUPS].set(w_r1).at[:, MOE_GROUPS:n].set(w_r2)
    br = jnp.zeros((1, LANES), F32).at[0, :MOE_GROUPS].set(b_r1).at[0, MOE_GROUPS:n].set(b_r2)
    wr_hi = wr.astype(BF16)
    wr_lo = (wr - wr_hi.astype(F32)).astype(BF16)
    return jnp.stack([wr_hi, wr_lo]), br


def kernel(x, c, positions, ada_w, ada_b, ln_g, ln_b, a_w_in, a_b_gate, a_norm_g, a_w_out, b_w_kv, b_w_q, b_w_o,
           moe_w_r1, moe_b_r1, moe_w_r2, moe_b_r2, moe_w_gate, moe_w_up, moe_w_down):
    bsz, s, d = x.shape
    assert d == D_MODEL and s % (B_WB * B_DILATIONS[-1]) == 0 and s % ROW_T == 0 and s % A_STEP == 0 and A_STEP % A_CHUNK == 0
    assert (bsz * s) % SORT_T == 0 and s % SORT_T == 0
    assert 2 * SORT_T + MOE_EXPERTS * (SEG_Q - 1) <= SORT_R and SORT_R % SEG_Q == 0
    mod = _adaln_mod(c, ada_w, ada_b)
    cos_t, sin_t = _rope_tables(positions)
    kv = None
    for l in range(DEPTH):
        lng1, lnb1 = ln_g[l, 0].reshape(1, d), ln_b[l, 0].reshape(1, d)
        lng2, lnb2 = ln_g[l, 1].reshape(1, d), ln_b[l, 1].reshape(1, d)
        wr, br = _router_weights(moe_w_r1[l], moe_b_r1[l], moe_w_r2[l], moe_b_r2[l])
        if l < N_A:
            n_main = 2 * A_QK + 2 * A_V
            win = jnp.zeros((d, n_main + LANES), BF16)
            win = win.at[:, :n_main + 2 * A_HEADS].set(a_w_in[l].astype(BF16))
            bg = jnp.zeros((1, LANES), F32).at[0, :2 * A_HEADS].set(a_b_gate[l])
            x1, h2, route = _mlstm_layer(x, mod[l], win, bg, a_norm_g[l].reshape(1, A_V),
                                         a_w_out[l].astype(BF16), lng1, lnb1, wr, br)
        else:
            lb = l - N_A
            qs = _q_project(x, mod[l], b_w_q[lb].astype(BF16), cos_t, sin_t)
            outs, lses = [], []
            for g in range(B_GROUPS):
                nsub = min(ATT_J[g], s // (B_WB * B_DILATIONS[g]))
                o_g, l_g = _dilated_attention(qs[g], kv[g], kv[B_GROUPS + g], nsub)
                outs.append(o_g)
                lses.append(l_g)
            x1, h2, route = _attn_out_layer(x, mod[l], outs, lses, b_w_o[lb].astype(BF16), lng1, lnb1, wr, br)
        x = _moe_and_norm(x1, h2, route, mod[l], lng2, lnb2, moe_w_gate, moe_w_up, moe_w_down, l)
        if l == N_A - 1:
            kv = _kv_project(x, b_w_kv.astype(BF16), cos_t, sin_t)
    return x
```

```python
import functools

import jax
import jax.numpy as jnp
from jax import lax
from jax.experimental import pallas as pl
from jax.experimental.pallas import tpu as pltpu

F32 = jnp.float32
BF16 = jnp.bfloat16
I32 = jnp.int32
HI = lax.Precision.HIGHEST

D_MODEL = 1024
DEPTH = 4
N_A = DEPTH // 2
ALPHA = (2.0 * DEPTH) ** 0.25
LN_EPS = 1e-5
A_HEADS = 4
A_DQK = D_MODEL // 8
A_DV = D_MODEL // 4
A_NORM_EPS = 1e-6
A_QK = A_HEADS * A_DQK
A_V = A_HEADS * A_DV
B_WINDOWS = (128, 512, 2048)
B_DILATIONS = (1, 4, 16)
B_GROUPS = 3
B_HEADS = 8
B_HEAD_DIM = 64
B_WIDTH = B_HEADS * B_HEAD_DIM
B_WB = 128
ROPE_THETA = 10000.0
MOE_GROUPS = 4
MOE_EPG = 8
MOE_EXPERTS = MOE_GROUPS * MOE_EPG
MOE_HIDDEN = D_MODEL // 4

LANES = 128
SUBLANES = 8
VMEM_LIMIT = 56 * 1024 * 1024

A_CHUNK = 256
A_STEP = 512
SORT_T = 512
SEG_Q = 16
SORT_R = 1536
PERM_BLK = 256
EXP_BM = 512
EXP_CPB = EXP_BM // SEG_Q
LIST_UNROLL = 4
ROW_T = 1024
ATT_J = (8, 8, 4)
MOD_TN = 1536


def _cparams(sem):
    return pltpu.CompilerParams(dimension_semantics=sem, vmem_limit_bytes=VMEM_LIMIT)


def _dot(a, b):
    return jnp.dot(a, b, preferred_element_type=F32)


def _dot_nt(a, b):
    return lax.dot_general(a, b, (((1,), (1,)), ((), ())), preferred_element_type=F32)


def _dot_tn(a, b):
    return lax.dot_general(a, b, (((0,), (0,)), ((), ())), preferred_element_type=F32)


def _layer_norm(v, g, b):
    mu = jnp.mean(v, axis=-1, keepdims=True)
    vc = v - mu
    var = jnp.mean(vc * vc, axis=-1, keepdims=True)
    return vc * lax.rsqrt(var + LN_EPS) * g + b


def _sigmoid(v):
    return 1.0 / (1.0 + jnp.exp(-v))


def _split_hi_lo(v):
    hi = v.astype(BF16)
    return hi, (v - hi.astype(F32)).astype(BF16)


def _route_tile(h2, wr_ref, br_ref):
    h_hi, h_lo = _split_hi_lo(h2)
    lg = _dot(h_hi, wr_ref[0]) + (_dot(h_hi, wr_ref[1]) + _dot(h_lo, wr_ref[0])) + br_ref[...]
    lane = lax.broadcasted_iota(I32, lg.shape, 1).astype(F32)
    neg = -jnp.inf
    big = 1000.0
    m1 = jnp.where(lane < MOE_GROUPS, lg, neg)
    mx = jnp.max(m1, axis=-1, keepdims=True)
    pg = 1.0 / jnp.sum(jnp.exp(m1 - mx), axis=-1, keepdims=True)
    gi = jnp.min(jnp.where(m1 == mx, lane, big), axis=-1, keepdims=True)
    lo = MOE_GROUPS + gi * MOE_EPG
    m2 = jnp.where((lane >= lo) & (lane < lo + MOE_EPG), lg, neg)
    v0 = jnp.max(m2, axis=-1, keepdims=True)
    j0 = jnp.min(jnp.where(m2 == v0, lane, big), axis=-1, keepdims=True)
    m3 = jnp.where(lane == j0, neg, m2)
    v1 = jnp.max(m3, axis=-1, keepdims=True)
    j1 = jnp.min(jnp.where(m3 == v1, lane, big), axis=-1, keepdims=True)
    t = jnp.exp(v1 - v0)
    wa = 1.0 / (1.0 + t)
    wb = t * wa
    out = jnp.where(lane == 0, j0 - MOE_GROUPS,
          jnp.where(lane == 1, j1 - MOE_GROUPS,
          jnp.where(lane == 2, pg * wa,
          jnp.where(lane == 3, pg * wb, 0.0))))
    return out


def _mod_kernel(c_ref, w_ref, b_ref, o_ref):
    c = c_ref[...]
    c_hi, c_lo = _split_hi_lo(c * _sigmoid(c))
    w_hi, w_lo = _split_hi_lo(w_ref[0])
    o_ref[0] = _dot(c_hi, w_hi) + (_dot(c_hi, w_lo) + _dot(c_lo, w_hi)) + b_ref[0]


def _adaln_mod(c, ada_w, ada_b):
    bsz, d = c.shape
    depth, _, n6 = ada_w.shape
    rows = 8
    c_pad = jnp.zeros((rows, d), F32).at[:bsz].set(c)
    out = pl.pallas_call(
        _mod_kernel,
        grid=(depth, n6 // MOD_TN),
        in_specs=[pl.BlockSpec((rows, d), lambda l, j: (0, 0)),
                  pl.BlockSpec((1, d, MOD_TN), lambda l, j: (l, 0, j)),
                  pl.BlockSpec((1, 1, MOD_TN), lambda l, j: (l, 0, j))],
        out_specs=pl.BlockSpec((1, rows, MOD_TN), lambda l, j: (l, 0, j)),
        out_shape=jax.ShapeDtypeStruct((depth, rows, n6), F32),
        compiler_params=_cparams(("parallel", "parallel")),
        name="adaln_mod",
    )(c_pad, ada_w, ada_b.reshape(depth, 1, n6))
    return out[:, :bsz].reshape(depth, bsz, 6, d)


def _mlstm_layer_kernel(x_ref, mod_ref, win_ref, bg_ref, ng_ref, wout_ref, lng_ref, lnb_ref,
                        wr_ref, br_ref, x1_ref, h2_ref, route_ref, c_ref, n_ref, m_ref):
    L = A_CHUNK
    n_chunks = A_STEP // L

    @pl.when(pl.program_id(1) == 0)
    def _():
        c_ref[...] = jnp.zeros_like(c_ref)
        n_ref[...] = jnp.zeros_like(n_ref)
        m_ref[...] = jnp.zeros_like(m_ref)

    md = mod_ref[...]
    sh1, sc1, g1, sh2, sc2, g2 = [md[j:j + 1, :] for j in range(6)]
    o_q, o_k, o_v, o_o, o_g = 0, A_QK, 2 * A_QK, 2 * A_QK + A_V, 2 * A_QK + 2 * A_V

    row = lax.broadcasted_iota(I32, (L, L), 0)
    col = lax.broadcasted_iota(I32, (L, L), 1)
    causal = row >= col
    tri = jnp.where(causal, 1.0, 0.0).astype(BF16)

    def proj_parts(ci):
        rows = slice(ci * L, (ci + 1) * L)
        out = {}

        def start():
            out["hb"] = (x_ref[rows, :] * (1.0 + sc1) + sh1).astype(BF16)
            out["q"] = _dot(out["hb"], win_ref[:, o_q:o_k])

        def kpart():
            out["k"] = _dot(out["hb"], win_ref[:, o_k:o_v]) * (A_DQK ** -0.5)

        def vpart():
            out["v"] = _dot(out["hb"], win_ref[:, o_v:o_o])

        def opart():
            out["og"] = _dot(out["hb"], win_ref[:, o_o:o_g])
            out["gates"] = _dot(out["hb"], win_ref[:, o_g:o_g + LANES]) + bg_ref[...]

        return out, [start, kpart, vpart, opart]

    def gate_terms(pr):
        gates = pr["gates"]
        lf = jnp.minimum(gates, 0.0) - jnp.log(1.0 + jnp.exp(-jnp.abs(gates)))
        lf_hi, lf_lo = _split_hi_lo(lf)
        lf_lo2 = (lf - lf_hi.astype(F32) - lf_lo.astype(F32)).astype(BF16)
        b_col = _dot(tri, lf_hi) + (_dot(tri, lf_lo) + _dot(tri, lf_lo2))
        return gates, b_col, gates.T, b_col.T

    def head(pr, gt, h):
        gates, b_col, g_t, b_t = gt
        qh = pr["q"][:, h * A_DQK:(h + 1) * A_DQK]
        kh = pr["k"][:, h * A_DQK:(h + 1) * A_DQK]
        vh = pr["v"][:, h * A_DV:(h + 1) * A_DV]
        qb, kb, vb = qh.astype(BF16), kh.astype(BF16), vh.astype(BF16)
        fl = A_HEADS + h
        b_c = b_col[:, fl:fl + 1]
        i_c = gates[:, h:h + 1]
        b_r = b_t[fl:fl + 1, :]
        i_r = g_t[h:h + 1, :]
        m_st = m_ref[h:h + 1, 0:1]
        c_st = c_ref[h]
        n_st = n_ref[h:h + 1, :]

        dmat = jnp.where(causal, b_c - b_r + i_r, -jnp.inf)
        inter = b_c + m_st
        m_t = jnp.maximum(inter, jnp.max(dmat, axis=-1, keepdims=True))
        dw = jnp.exp(dmat - m_t)
        iw = jnp.exp(inter - m_t)
        sc = _dot_nt(qb, kb) * dw
        num = _dot(sc.astype(BF16), vb) + iw * _dot(qb, c_st.astype(BF16))
        den = jnp.sum(sc, axis=-1, keepdims=True) + iw * jnp.sum(qh * n_st, axis=-1, keepdims=True)
        hh = num * (1.0 / jnp.maximum(jnp.abs(den), jnp.exp(-m_t)))

        b_last = b_col[L - 1:L, fl:fl + 1]
        ws_log = b_last - b_c + i_c
        m_new = jnp.maximum(b_last + m_st, jnp.max(ws_log, axis=0, keepdims=True))
        ws = jnp.exp(ws_log - m_new)
        decay = jnp.exp(b_last + m_st - m_new)
        kw = kh * ws
        c_ref[h] = decay * c_st + _dot_tn(kw.astype(BF16), vb)
        n_ref[h:h + 1, :] = decay * n_st + jnp.sum(kw, axis=0, keepdims=True)
        m_ref[h:h + 1, :] = jnp.broadcast_to(m_new, (1, LANES))

        mu = jnp.mean(hh, axis=-1, keepdims=True)
        hc = hh - mu
        var = jnp.mean(hc * hc, axis=-1, keepdims=True)
        hn = hc * lax.rsqrt(var + A_NORM_EPS)
        og = pr["og"][:, h * A_DV:(h + 1) * A_DV]
        return (hn * ng_ref[:, h * A_DV:(h + 1) * A_DV] * _sigmoid(og)).astype(BF16)

    def tail_parts(ci, ys):
        rows = slice(ci * L, (ci + 1) * L)
        st = {}

        def outproj():
            st["y"] = _dot(jnp.concatenate(ys, axis=1), wout_ref[...])

        def norm():
            x1 = _layer_norm(ALPHA * x_ref[rows, :] + (1.0 + g1) * st["y"], lng_ref[...], lnb_ref[...])
            x1_ref[rows, :] = x1
            st["h2"] = x1 * (1.0 + sc2) + sh2
            h2_ref[rows, :] = st["h2"].astype(h2_ref.dtype)

        def router():
            route_ref[rows, :] = _route_tile(st["h2"], wr_ref, br_ref)

        return [outproj, norm, router]

    pr, parts = proj_parts(0)
    for part in parts:
        part()
    pending_tail = []
    for ci in range(n_chunks):
        nxt, nxt_parts = proj_parts(ci + 1) if ci + 1 < n_chunks else (None, [])
        fill = nxt_parts + pending_tail
        gt = gate_terms(pr)
        ys = []
        for h in range(A_HEADS):
            ys.append(head(pr, gt, h))
            share = -(-len(fill) // (A_HEADS - h))
            for part in fill[:share]:
                part()
            fill = fill[share:]
        pending_tail = tail_parts(ci, ys)
        pr = nxt
    for part in pending_tail:
        part()


def _mlstm_layer(x, mod_l, win, bg, ng, wout, lng, lnb, wr, br):
    bsz, s, d = x.shape
    nw = win.shape[1]
    const = lambda b, i: (0, 0)
    tok = pl.BlockSpec((None, A_STEP, d), lambda b, i: (b, i, 0))
    return pl.pallas_call(
        _mlstm_layer_kernel,
        grid=(bsz, s // A_STEP),
        in_specs=[tok,
                  pl.BlockSpec((None, 6, d), lambda b, i: (b, 0, 0)),
                  pl.BlockSpec((d, nw), const),
                  pl.BlockSpec((1, LANES), const),
                  pl.BlockSpec((1, A_V), const),
                  pl.BlockSpec((A_V, d), const),
                  pl.BlockSpec((1, d), const),
                  pl.BlockSpec((1, d), const),
                  pl.BlockSpec((2, d, LANES), lambda b, i: (0, 0, 0)),
                  pl.BlockSpec((1, LANES), const)],
        out_specs=[tok, tok, pl.BlockSpec((None, A_STEP, LANES), lambda b, i: (b, i, 0))],
        out_shape=[jax.ShapeDtypeStruct((bsz, s, d), F32),
                   jax.ShapeDtypeStruct((bsz, s, d), BF16),
                   jax.ShapeDtypeStruct((bsz, s, LANES), F32)],
        scratch_shapes=[pltpu.VMEM((A_HEADS, A_DQK, A_DV), F32),
                        pltpu.VMEM((8, A_DQK), F32),
                        pltpu.VMEM((8, LANES), F32)],
        compiler_params=_cparams(("parallel", "arbitrary")),
        name="mlstm_layer",
    )(x, mod_l, win, bg, ng, wout, lng, lnb, wr, br)


def _moe_sort_kernel(h2_ref, route_ref, xs_ref, ws_ref, pos_ref, cnt_ref):
    n, nr = SORT_T, SORT_R
    r = route_ref[...]
    lane = lax.broadcasted_iota(I32, (n, LANES), 1).astype(F32)
    oh0 = lane == r[:, 0:1]
    oh1 = lane == r[:, 1:2]
    oh = jnp.where(oh0 | oh1, 1.0, 0.0)
    row = lax.broadcasted_iota(I32, (n, n), 0)
    col = lax.broadcasted_iota(I32, (n, n), 1)
    before = jnp.where(row > col, 1.0, 0.0).astype(BF16)
    excl = _dot(before, oh.astype(BF16))
    cnt = jnp.sum(oh, axis=0, keepdims=True)
    pcnt = jnp.ceil(cnt * (1.0 / SEG_Q)) * SEG_Q
    ej = lax.broadcasted_iota(I32, (LANES, LANES), 0)
    el = lax.broadcasted_iota(I32, (LANES, LANES), 1)
    upper = jnp.where(ej < el, 1.0, 0.0).astype(F32)
    seg = jnp.dot(jnp.broadcast_to(pcnt, (SUBLANES, LANES)), upper,
                  preferred_element_type=F32, precision=HI)[0:1, :]
    base = excl + seg
    p0 = jnp.sum(jnp.where(oh0, base, 0.0), axis=-1, keepdims=True)
    p1 = jnp.sum(jnp.where(oh1, base, 0.0), axis=-1, keepdims=True)
    pos = jnp.where(lane == 0, p0, jnp.where(lane == 1, p1, 0.0))
    pos_ref[...] = pos
    cnt_ref[...] = cnt
    pos_t = pos.T

    def split3(w):
        hi = w.astype(BF16)
        r1 = w - hi.astype(F32)
        mid = r1.astype(BF16)
        lo = (r1 - mid.astype(F32)).astype(BF16)
        return jnp.where(lane == 0, hi.astype(F32),
               jnp.where(lane == 1, mid.astype(F32),
               jnp.where(lane == 2, lo.astype(F32), 0.0))).astype(BF16)

    w0, w1 = split3(r[:, 2:3]), split3(r[:, 3:4])
    h2 = h2_ref[...]
    for kb in range(nr // PERM_BLK):
        rows = slice(kb * PERM_BLK, (kb + 1) * PERM_BLK)
        ri = (lax.broadcasted_iota(I32, (PERM_BLK, n), 0) + kb * PERM_BLK).astype(F32)
        sel0 = ri == pos_t[0:1, :]
        sel1 = ri == pos_t[1:2, :]
        perm = jnp.where(sel0 | sel1, 1.0, 0.0).astype(BF16)
        xs_ref[rows, :] = _dot(perm, h2).astype(BF16)
        wsum = (_dot(jnp.where(sel0, 1.0, 0.0).astype(BF16), w0)
                + _dot(jnp.where(sel1, 1.0, 0.0).astype(BF16), w1))
        ws = wsum[:, 0:1] + wsum[:, 1:2] + wsum[:, 2:3]
        ws_ref[rows, :] = jnp.broadcast_to(ws, (PERM_BLK, LANES))


def _chunk_list_kernel(seg_ref, nch_ref, src_ref, be_ref, bc_ref, nb_ref, *, n_tiles):
    n_src = src_ref.shape[0]
    n_blk = be_ref.shape[0]

    def clear_src(k, c):
        src_ref[k] = 0
        return c

    lax.fori_loop(0, n_src, clear_src, 0, unroll=16)

    def per_expert(e, carry):
        k0, nb0 = carry

        def per_tile(i, k):
            s = seg_ref[i * MOE_EXPERTS + e]
            n = nch_ref[i * MOE_EXPERTS + e]

            row0 = i * SORT_R + s
            for c in range(LIST_UNROLL):
                @pl.when(c < n)
                def _(c=c):
                    src_ref[k + c] = row0 + c * SEG_Q

            def per_chunk(c, carry):
                src_ref[k + c] = row0 + c * SEG_Q
                return carry

            lax.fori_loop(LIST_UNROLL, n, per_chunk, 0)
            return k + n

        k1 = lax.fori_loop(0, n_tiles, per_tile, k0)
        tot = k1 - k0
        nblk = (tot + EXP_CPB - 1) // EXP_CPB

        def per_block(j, c):
            be_ref[nb0 + j] = e
            bc_ref[nb0 + j] = jnp.minimum(EXP_CPB, tot - j * EXP_CPB)
            return c

        lax.fori_loop(0, nblk, per_block, 0)
        return k0 + nblk * EXP_CPB, nb0 + nblk

    _, nb = lax.fori_loop(0, MOE_EXPERTS, per_expert, (jnp.int32(0), jnp.int32(0)))
    nb_ref[0] = nb

    def spare_block(j, c):
        be_ref[j] = MOE_EXPERTS - 1
        bc_ref[j] = 0
        return c

    lax.fori_loop(nb, n_blk, spare_block, 0)


def _chunk_copy(src_ref, src_row, dst_ref, dst_row, sem):
    return pltpu.make_async_copy(src_ref.at[pl.ds(pl.multiple_of(src_row, SEG_Q), SEG_Q), :],
                                 dst_ref.at[pl.ds(pl.multiple_of(dst_row, SEG_Q), SEG_Q), :], sem)


def _expert_kernel(src_ref, be_ref, bc_ref, nb_ref, wg_ref, wu_ref, wd_ref, xs_ref, ys_ref,
                   wgu_s, wd_s, xg, yg, sem_in, sem_out):
    b = pl.program_id(0)
    last = pl.num_programs(0) - 1
    nb = nb_ref[0]

    def clamp(blk):
        return jnp.clip(blk, 0, last)

    def count(blk):
        return jnp.where((blk >= 0) & (blk <= last), bc_ref[clamp(blk)], 0)

    def per_chunk(n, fn):
        for c in range(EXP_CPB):
            @pl.when(c < n)
            def _(c=c):
                fn(c)

    def gather(blk, slot):
        per_chunk(count(blk), lambda c: _chunk_copy(
            xs_ref, src_ref[clamp(blk) * EXP_CPB + c], xg.at[slot], c * SEG_Q, sem_in.at[slot]).start())

    def wait_gather(blk, slot):
        per_chunk(count(blk), lambda c: _chunk_copy(xs_ref, 0, xg.at[slot], 0, sem_in.at[slot]).wait())

    def scatter(blk, slot):
        per_chunk(count(blk), lambda c: _chunk_copy(
            yg.at[slot], c * SEG_Q, ys_ref, src_ref[clamp(blk) * EXP_CPB + c], sem_out.at[slot]).start())

    def wait_scatter(blk, slot):
        per_chunk(count(blk), lambda c: _chunk_copy(yg.at[slot], 0, ys_ref, 0, sem_out.at[slot]).wait())

    @pl.when(b == 0)
    def _():
        xg[...] = jnp.zeros_like(xg)
        gather(0, 0)

    @pl.when((b == 0) | (be_ref[b] != be_ref[jnp.maximum(b - 1, 0)]))
    def _():
        wgu_s[:, 0:MOE_HIDDEN] = wg_ref[...].astype(BF16)
        wgu_s[:, MOE_HIDDEN:2 * MOE_HIDDEN] = wu_ref[...].astype(BF16)
        wd_s[...] = wd_ref[...].astype(BF16)

    @pl.when(b < nb)
    def _():
        slot = b & 1
        gather(b + 1, 1 - slot)
        wait_gather(b, slot)
        gu = _dot(xg[slot], wgu_s[...])
        g = gu[:, 0:MOE_HIDDEN]
        u = gu[:, MOE_HIDDEN:2 * MOE_HIDDEN]
        y = _dot((g * _sigmoid(g) * u).astype(BF16), wd_s[...])
        wait_scatter(b - 2, slot)
        yg[slot] = y.astype(BF16)
        scatter(b, slot)

    @pl.when(b == last)
    def _():
        wait_scatter(nb - 2, nb & 1)
        wait_scatter(nb - 1, (nb - 1) & 1)


def _combine_kernel(ys_ref, ws_ref, pos_ref, x1_ref, mod_ref, lng_ref, lnb_ref, x2_ref):
    n, nr = SORT_T, SORT_R
    pos = pos_ref[...]
    y = jnp.zeros((n, D_MODEL), F32)
    for kb in range(nr // PERM_BLK):
        rows = slice(kb * PERM_BLK, (kb + 1) * PERM_BLK)
        ysw = (ys_ref[rows, :].astype(F32) * ws_ref[rows, 0:1]).astype(BF16)
        ci = (lax.broadcasted_iota(I32, (n, PERM_BLK), 1) + kb * PERM_BLK).astype(F32)
        unperm = jnp.where((ci == pos[:, 0:1]) | (ci == pos[:, 1:2]), 1.0, 0.0).astype(BF16)
        y = y + _dot(unperm, ysw)
    g2 = mod_ref[5:6, :]
    x2_ref[...] = _layer_norm(ALPHA * x1_ref[...] + (1.0 + g2) * y, lng_ref[...], lnb_ref[...])


def _moe_and_norm(x1, h2, route, mod_l, lng, lnb, wg, wu, wd, layer):
    bsz, s, d = x1.shape
    n_tok = bsz * s
    nt = n_tok // SORT_T
    n_rows = nt * SORT_R
    x1f = x1.reshape(n_tok, d)
    h2f = h2.reshape(n_tok, d)
    rt = route.reshape(n_tok, LANES)

    xs, ws, pos, cnt = pl.pallas_call(
        _moe_sort_kernel,
        grid=(nt,),
        in_specs=[pl.BlockSpec((SORT_T, d), lambda i: (i, 0)),
                  pl.BlockSpec((SORT_T, LANES), lambda i: (i, 0))],
        out_specs=[pl.BlockSpec((SORT_R, d), lambda i: (i, 0)),
                   pl.BlockSpec((SORT_R, LANES), lambda i: (i, 0)),
                   pl.BlockSpec((SORT_T, LANES), lambda i: (i, 0)),
                   pl.BlockSpec((None, 1, LANES), lambda i: (i, 0, 0))],
        out_shape=[jax.ShapeDtypeStruct((n_rows, d), BF16),
                   jax.ShapeDtypeStruct((n_rows, LANES), F32),
                   jax.ShapeDtypeStruct((n_tok, LANES), F32),
                   jax.ShapeDtypeStruct((nt, 1, LANES), F32)],
        compiler_params=_cparams(("parallel",)),
        name="moe_sort",
    )(h2f, rt)

    cnt_i = cnt[:, 0, :MOE_EXPERTS].astype(I32)
    pcnt = (cnt_i + SEG_Q - 1) // SEG_Q * SEG_Q
    seg = (jnp.cumsum(pcnt, axis=1) - pcnt).reshape(-1).astype(I32)
    nch = (pcnt // SEG_Q).reshape(-1).astype(I32)

    n_src = n_rows // SEG_Q + MOE_EXPERTS * EXP_CPB
    n_blk = n_src // EXP_CPB
    smem = pl.BlockSpec(memory_space=pltpu.SMEM)
    src, block_e, block_n, n_used = pl.pallas_call(
        functools.partial(_chunk_list_kernel, n_tiles=nt),
        in_specs=[smem, smem],
        out_specs=[smem, smem, smem, smem],
        out_shape=[jax.ShapeDtypeStruct((n_src,), I32),
                   jax.ShapeDtypeStruct((n_blk,), I32),
                   jax.ShapeDtypeStruct((n_blk,), I32),
                   jax.ShapeDtypeStruct((1,), I32)],
        name="moe_chunk_list",
    )(seg, nch)

    ys = pl.pallas_call(
        _expert_kernel,
        grid_spec=pltpu.PrefetchScalarGridSpec(
            num_scalar_prefetch=4,
            grid=(n_blk,),
            in_specs=[pl.BlockSpec((None, None, d, MOE_HIDDEN), lambda b, src, be, bc, nb: (layer, be[b], 0, 0)),
                      pl.BlockSpec((None, None, d, MOE_HIDDEN), lambda b, src, be, bc, nb: (layer, be[b], 0, 0)),
                      pl.BlockSpec((None, None, MOE_HIDDEN, d), lambda b, src, be, bc, nb: (layer, be[b], 0, 0)),
                      pl.BlockSpec(memory_space=pl.ANY)],
            out_specs=pl.BlockSpec(memory_space=pl.ANY),
            scratch_shapes=[pltpu.VMEM((d, 2 * MOE_HIDDEN), BF16),
                            pltpu.VMEM((MOE_HIDDEN, d), BF16),
                            pltpu.VMEM((2, EXP_BM, d), BF16),
                            pltpu.VMEM((2, EXP_BM, d), BF16),
                            pltpu.SemaphoreType.DMA((2,)),
                            pltpu.SemaphoreType.DMA((2,))]),
        out_shape=jax.ShapeDtypeStruct((n_rows, d), BF16),
        input_output_aliases={7: 0},
        compiler_params=_cparams(("arbitrary",)),
        name="moe_experts",
    )(src, block_e, block_n, n_used, wg, wu, wd, xs)

    per_b = s // SORT_T
    x2 = pl.pallas_call(
        _combine_kernel,
        grid=(nt,),
        in_specs=[pl.BlockSpec((SORT_R, d), lambda i: (i, 0)),
                  pl.BlockSpec((SORT_R, LANES), lambda i: (i, 0)),
                  pl.BlockSpec((SORT_T, LANES), lambda i: (i, 0)),
                  pl.BlockSpec((SORT_T, d), lambda i: (i, 0)),
                  pl.BlockSpec((None, 6, d), lambda i: (i // per_b, 0, 0)),
                  pl.BlockSpec((1, d), lambda i: (0, 0)),
                  pl.BlockSpec((1, d), lambda i: (0, 0))],
        out_specs=pl.BlockSpec((SORT_T, d), lambda i: (i, 0)),
        out_shape=jax.ShapeDtypeStruct((n_tok, d), F32),
        compiler_params=_cparams(("parallel",)),
        name="moe_combine_norm",
    )(ys, ws, pos, x1f, mod_l, lng, lnb)
    return x2.reshape(bsz, s, d)


def _rope_table_kernel(pos_ref, inv_ref, sgn_ref, cos_ref, sin_ref):
    ang = pos_ref[...] * inv_ref[...]
    cos_ref[...] = jnp.cos(ang)
    sin_ref[...] = jnp.sin(ang) * sgn_ref[...]


def _rope_tables(positions):
    bsz, s = positions.shape
    half = B_HEAD_DIM // 2
    inv = jnp.power(jnp.float32(ROPE_THETA), -jnp.arange(half, dtype=F32) * 2.0 / B_HEAD_DIM)
    inv_row = jnp.tile(inv, LANES // half).reshape(1, LANES)
    sgn_row = jnp.tile(jnp.concatenate([-jnp.ones((half,), F32), jnp.ones((half,), F32)]),
                       LANES // B_HEAD_DIM).reshape(1, LANES)
    pos = jnp.broadcast_to(positions.astype(F32)[:, :, None], (bsz, s, LANES))
    tok = pl.BlockSpec((None, ROW_T, LANES), lambda b, i: (b, i, 0))
    row = pl.BlockSpec((1, LANES), lambda b, i: (0, 0))
    return pl.pallas_call(
        _rope_table_kernel,
        grid=(bsz, s // ROW_T),
        in_specs=[tok, row, row],
        out_specs=[tok, tok],
        out_shape=[jax.ShapeDtypeStruct((bsz, s, LANES), F32)] * 2,
        compiler_params=_cparams(("parallel", "parallel")),
        name="rope_tables",
    )(pos, inv_row, sgn_row)


def _rope(t, cosf, sinf, first_half):
    n = t.shape[1]
    half = B_HEAD_DIM // 2
    fwd = pltpu.roll(t, n - half, 1)
    bwd = pltpu.roll(t, half, 1)
    return t * cosf + jnp.where(first_half, fwd, bwd) * sinf


def _rope_operands(cos_ref, sin_ref, rows):
    reps = B_WIDTH // LANES
    cosf = jnp.concatenate([cos_ref[...]] * reps, axis=1)
    sinf = jnp.concatenate([sin_ref[...]] * reps, axis=1)
    lane = lax.broadcasted_iota(I32, (rows, B_WIDTH), 1)
    first_half = (lane & (B_HEAD_DIM - 1)) < (B_HEAD_DIM // 2)
    return cosf, sinf, first_half


def _store_by_residue(val, out_ref, stage_ref, dil):
    rows, width = val.shape
    if dil == 1:
        out_ref[0] = val.astype(out_ref.dtype)
        return
    for c in range(width // LANES):
        stage_ref[c] = val[:, c * LANES:(c + 1) * LANES]
    for r in range(dil):
        for c in range(width // LANES):
            out_ref[r, :, c * LANES:(c + 1) * LANES] = (
                stage_ref[c, pl.ds(r, rows // dil, stride=dil), :].astype(out_ref.dtype))


def _load_by_residue(in_ref, stage_ref, dil):
    _, sub, width = in_ref.shape
    if dil == 1:
        return in_ref[0].astype(F32)
    for r in range(dil):
        for c in range(width // LANES):
            stage_ref[c, pl.ds(r, sub, stride=dil), :] = in_ref[r, :, c * LANES:(c + 1) * LANES].astype(F32)
    return jnp.concatenate([stage_ref[c] for c in range(width // LANES)], axis=1)


def _class_spec(dil, width):
    return pl.BlockSpec((None, dil, ROW_T // dil, width), lambda b, i: (b, 0, i, 0))


def _class_shape(bsz, s, dil, width, dtype):
    return jax.ShapeDtypeStruct((bsz, dil, s // dil, width), dtype)


def _kv_kernel(x_ref, w_ref, cos_ref, sin_ref, k0, k1, k2, v0, v1, v2, stage_ref):
    xb = x_ref[...].astype(BF16)
    cosf, sinf, first_half = _rope_operands(cos_ref, sin_ref, ROW_T)
    for g, (k_ref, v_ref) in enumerate(((k0, v0), (k1, v1), (k2, v2))):
        kg = _dot(xb, w_ref[:, g * B_WIDTH:(g + 1) * B_WIDTH])
        _store_by_residue(_rope(kg, cosf, sinf, first_half), k_ref, stage_ref, B_DILATIONS[g])
        vo = (B_GROUPS + g) * B_WIDTH
        _store_by_residue(_dot(xb, w_ref[:, vo:vo + B_WIDTH]), v_ref, stage_ref, B_DILATIONS[g])


def _kv_project(x, wkv, cos_t, sin_t):
    bsz, s, d = x.shape
    tok = lambda w: pl.BlockSpec((None, ROW_T, w), lambda b, i: (b, i, 0))
    dils = B_DILATIONS * 2
    return pl.pallas_call(
        _kv_kernel,
        grid=(bsz, s // ROW_T),
        in_specs=[tok(d), pl.BlockSpec(wkv.shape, lambda b, i: (0, 0)), tok(LANES), tok(LANES)],
        out_specs=[_class_spec(dl, B_WIDTH) for dl in dils],
        out_shape=[_class_shape(bsz, s, dl, B_WIDTH, BF16) for dl in dils],
        scratch_shapes=[pltpu.VMEM((B_WIDTH // LANES, ROW_T, LANES), F32)],
        compiler_params=_cparams(("parallel", "parallel")),
        name="kv_project",
    )(x, wkv, cos_t, sin_t)


def _q_kernel(x_ref, mod_ref, w_ref, cos_ref, sin_ref, q0, q1, q2, stage_ref):
    md = mod_ref[...]
    hb = (x_ref[...] * (1.0 + md[1:2, :]) + md[0:1, :]).astype(BF16)
    cosf, sinf, first_half = _rope_operands(cos_ref, sin_ref, ROW_T)
    for g, q_ref in enumerate((q0, q1, q2)):
        qg = _dot(hb, w_ref[:, g * B_WIDTH:(g + 1) * B_WIDTH])
        _store_by_residue(_rope(qg, cosf, sinf, first_half) * (B_HEAD_DIM ** -0.5), q_ref, stage_ref,
                          B_DILATIONS[g])


def _q_project(x, mod_l, wq, cos_t, sin_t):
    bsz, s, d = x.shape
    tok = lambda w: pl.BlockSpec((None, ROW_T, w), lambda b, i: (b, i, 0))
    return pl.pallas_call(
        _q_kernel,
        grid=(bsz, s // ROW_T),
        in_specs=[tok(d), pl.BlockSpec((None, 6, d), lambda b, i: (b, 0, 0)),
                  pl.BlockSpec(wq.shape, lambda b, i: (0, 0)), tok(LANES), tok(LANES)],
        out_specs=[_class_spec(dl, B_WIDTH) for dl in B_DILATIONS],
        out_shape=[_class_shape(bsz, s, dl, B_WIDTH, BF16) for dl in B_DILATIONS],
        scratch_shapes=[pltpu.VMEM((B_WIDTH // LANES, ROW_T, LANES), F32)],
        compiler_params=_cparams(("parallel", "parallel")),
        name="q_project",
    )(x, mod_l, wq, cos_t, sin_t)


def _attn_kernel(q_ref, kc_ref, kp_ref, vc_ref, vp_ref, o_ref, lse_ref, kw_ref, vx_ref, *, nsub):
    wb = B_WB
    npair = B_WIDTH // LANES
    first_step = pl.program_id(2) == 0
    kw_ref[0:wb, :] = kp_ref[...]
    kw_ref[wb:, :] = kc_ref[...]
    for p in range(npair):
        vx_ref[p, 0:wb, 0:LANES] = vp_ref[:, p * LANES:(p + 1) * LANES]
        vx_ref[p, wb:, 0:LANES] = vc_ref[:, p * LANES:(p + 1) * LANES]
        vx_ref[p, :, LANES:2 * LANES] = jnp.ones((vx_ref.shape[1], LANES), BF16)
    qi = lax.broadcasted_iota(I32, (wb, 2 * wb), 0)
    kj = lax.broadcasted_iota(I32, (wb, 2 * wb), 1)
    band = (kj >= qi) & (kj <= qi + wb)
    lane = lax.broadcasted_iota(I32, (wb, LANES), 1)
    low = lane < B_HEAD_DIM

    for j in range(nsub):
        r0 = j * wb
        valid = band & ((kj >= wb) | jnp.logical_not(first_step)) if j == 0 else band
        q = q_ref[pl.ds(r0, wb), :]
        kwin = kw_ref[pl.ds(r0, 2 * wb), :]
        outs = []
        m_all = jnp.zeros((wb, LANES), F32)
        l_all = jnp.ones((wb, LANES), F32)
        for p in range(npair):
            qp = q[:, p * LANES:(p + 1) * LANES]
            kpair = kwin[:, p * LANES:(p + 1) * LANES]
            vx = vx_ref[p, pl.ds(r0, 2 * wb), :]
            acc = lsum = None
            for hh in range(2):
                mine = low if hh == 0 else jnp.logical_not(low)
                qm = jnp.where(mine, qp, jnp.zeros_like(qp))
                sc = jnp.where(valid, _dot_nt(qm, kpair), -jnp.inf)
                m = jnp.max(sc, axis=-1, keepdims=True)
                pv = _dot(jnp.exp(sc - m).astype(BF16), vx)
                o_h, l_h = pv[:, 0:LANES], pv[:, LANES:2 * LANES]
                m_all = jnp.where(lane == 2 * p + hh, m, m_all)
                l_all = jnp.where(lane == 2 * p + hh, l_h, l_all)
                acc = o_h if hh == 0 else jnp.where(low, acc, o_h)
                lsum = l_h if hh == 0 else jnp.where(low, lsum, l_h)
            outs.append(acc * (1.0 / lsum))
        o_ref[pl.ds(r0, wb), :] = jnp.concatenate(outs, axis=1).astype(BF16)
        lse_ref[pl.ds(r0, wb), :] = m_all + jnp.log(l_all)


def _dilated_attention(q, k, v, nsub):
    bsz, dil, rows, w = q.shape
    blk = B_WB * nsub
    cur = lambda width: pl.BlockSpec((None, None, blk, width), lambda b, r, n: (b, r, n, 0))
    prev = pl.BlockSpec((None, None, B_WB, w), lambda b, r, n: (b, r, jnp.maximum(n * nsub - 1, 0), 0))
    return pl.pallas_call(
        functools.partial(_attn_kernel, nsub=nsub),
        grid=(bsz, dil, rows // blk),
        in_specs=[cur(w), cur(w), prev, cur(w), prev],
        out_specs=[cur(w), cur(LANES)],
        out_shape=[jax.ShapeDtypeStruct((bsz, dil, rows, w), BF16),
                   jax.ShapeDtypeStruct((bsz, dil, rows, LANES), F32)],
        scratch_shapes=[pltpu.VMEM((blk + B_WB, w), BF16),
                        pltpu.VMEM((w // LANES, blk + B_WB, 2 * LANES), BF16)],
        compiler_params=_cparams(("parallel", "parallel", "arbitrary")),
        name=f"dilated_attention_d{dil}",
    )(q, k, k, v, v)


def _attn_out_kernel(x_ref, mod_ref, o0, o1, o2, l0, l1, l2, wo_ref, lng_ref, lnb_ref, wr_ref, br_ref,
                     x1_ref, h2_ref, route_ref, stage_ref):
    md = mod_ref[...]
    g1, sh2, sc2 = md[2:3, :], md[3:4, :], md[4:5, :]
    lses = [_load_by_residue(l_ref, stage_ref, dl) for l_ref, dl in zip((l0, l1, l2), B_DILATIONS)]
    mx = jnp.maximum(jnp.maximum(lses[0], lses[1]), lses[2])
    es = [jnp.exp(l - mx) for l in lses]
    inv = 1.0 / (es[0] + es[1] + es[2])

    def per_head(wc):
        return jnp.concatenate([jnp.broadcast_to(wc[:, h:h + 1], (ROW_T, B_HEAD_DIM)) for h in range(B_HEADS)],
                               axis=1)

    o = jnp.zeros((ROW_T, B_WIDTH), F32)
    for o_ref, e, dl in zip((o0, o1, o2), es, B_DILATIONS):
        o = o + per_head(e * inv) * _load_by_residue(o_ref, stage_ref, dl)
    y = _dot(o.astype(BF16), wo_ref[...])
    x1 = _layer_norm(ALPHA * x_ref[...] + (1.0 + g1) * y, lng_ref[...], lnb_ref[...])
    x1_ref[...] = x1
    h2 = x1 * (1.0 + sc2) + sh2
    h2_ref[...] = h2.astype(h2_ref.dtype)
    route_ref[...] = _route_tile(h2, wr_ref, br_ref)


def _attn_out_layer(x, mod_l, outs, lses, wo, lng, lnb, wr, br):
    bsz, s, d = x.shape
    tok = lambda w: pl.BlockSpec((None, ROW_T, w), lambda b, i: (b, i, 0))
    const = lambda b, i: (0, 0)
    return pl.pallas_call(
        _attn_out_kernel,
        grid=(bsz, s // ROW_T),
        in_specs=[tok(d), pl.BlockSpec((None, 6, d), lambda b, i: (b, 0, 0))]
                 + [_class_spec(dl, B_WIDTH) for dl in B_DILATIONS]
                 + [_class_spec(dl, LANES) for dl in B_DILATIONS]
                 + [pl.BlockSpec((B_WIDTH, d), const), pl.BlockSpec((1, d), const), pl.BlockSpec((1, d), const),
                    pl.BlockSpec((2, d, LANES), lambda b, i: (0, 0, 0)), pl.BlockSpec((1, LANES), const)],
        out_specs=[tok(d), tok(d), tok(LANES)],
        out_shape=[jax.ShapeDtypeStruct((bsz, s, d), F32),
                   jax.ShapeDtypeStruct((bsz, s, d), BF16),
                   jax.ShapeDtypeStruct((bsz, s, LANES), F32)],
        scratch_shapes=[pltpu.VMEM((B_WIDTH // LANES, ROW_T, LANES), F32)],
        compiler_params=_cparams(("parallel", "parallel")),
        name="attn_out_layer",
    )(x, mod_l, *outs, *lses, wo, lng, lnb, wr, br)


def _router_weights(w_r1, b_r1, w_r2, b_r2):
    d = w_r1.shape[0]
    n = MOE_GROUPS + MOE_EXPERTS
    wr = jnp.zeros((d, LANES), F32).at[:, :MOE_GROUPS].set(w_r1).at[:, MOE_GROUPS:n].set(w_r2)
    br = jnp.zeros((1, LANES), F32).at[0, :MOE_GROUPS].set(b_r1).at[0, MOE_GROUPS:n].set(b_r2)
    wr_hi = wr.astype(BF16)
    wr_lo = (wr - wr_hi.astype(F32)).astype(BF16)
    return jnp.stack([wr_hi, wr_lo]), br


def kernel(x, c, positions, ada_w, ada_b, ln_g, ln_b, a_w_in, a_b_gate, a_norm_g, a_w_out, b_w_kv, b_w_q, b_w_o,
           moe_w_r1, moe_b_r1, moe_w_r2, moe_b_r2, moe_w_gate, moe_w_up, moe_w_down):
    bsz, s, d = x.shape
    assert d == D_MODEL and s % (B_WB * B_DILATIONS[-1]) == 0 and s % ROW_T == 0 and s % A_STEP == 0 and A_STEP % A_CHUNK == 0
    assert (bsz * s) % SORT_T == 0 and s % SORT_T == 0
    assert 2 * SORT_T + MOE_EXPERTS * (SEG_Q - 1) <= SORT_R and SORT_R % SEG_Q == 0
    mod = _adaln_mod(c, ada_w, ada_b)
    cos_t, sin_t = _rope_tables(positions)
    kv = None
    for l in range(DEPTH):
        lng1, lnb1 = ln_g[l, 0].reshape(1, d), ln_b[l, 0].reshape(1, d)
        lng2, lnb2 = ln_g[l, 1].reshape(1, d), ln_b[l, 1].reshape(1, d)
        wr, br = _router_weights(moe_w_r1[l], moe_b_r1[l], moe_w_r2[l], moe_b_r2[l])
        if l < N_A:
            n_main = 2 * A_QK + 2 * A_V
            win = jnp.zeros((d, n_main + LANES), BF16)
            win = win.at[:, :n_main + 2 * A_HEADS].set(a_w_in[l].astype(BF16))
            bg = jnp.zeros((1, LANES), F32).at[0, :2 * A_HEADS].set(a_b_gate[l])
            x1, h2, route = _mlstm_layer(x, mod[l], win, bg, a_norm_g[l].reshape(1, A_V),
                                         a_w_out[l].astype(BF16), lng1, lnb1, wr, br)
        else:
            lb = l - N_A
            qs = _q_project(x, mod[l], b_w_q[lb].astype(BF16), cos_t, sin_t)
            outs, lses = [], []
            for g in range(B_GROUPS):
                nsub = min(ATT_J[g], s // (B_WB * B_DILATIONS[g]))
                o_g, l_g = _dilated_attention(qs[g], kv[g], kv[B_GROUPS + g], nsub)
                outs.append(o_g)
                lses.append(l_g)
            x1, h2, route = _attn_out_layer(x, mod[l], outs, lses, b_w_o[lb].astype(BF16), lng1, lnb1, wr, br)
        x = _moe_and_norm(x1, h2, route, mod[l], lng2, lnb2, moe_w_gate, moe_w_up, moe_w_down, l)
        if l == N_A - 1:
            kv = _kv_project(x, b_w_kv.astype(BF16), cos_t, sin_t)
    return x
```

```python
import functools

import jax
import jax.numpy as jnp
from jax import lax
from jax.experimental import pallas as pl
from jax.experimental.pallas import tpu as pltpu

F32 = jnp.float32
BF16 = jnp.bfloat16
I32 = jnp.int32
HI = lax.Precision.HIGHEST

D_MODEL = 1024
DEPTH = 4
N_A = DEPTH // 2
ALPHA = (2.0 * DEPTH) ** 0.25
LN_EPS = 1e-5
A_HEADS = 4
A_DQK = D_MODEL // 8
A_DV = D_MODEL // 4
A_NORM_EPS = 1e-6
A_QK = A_HEADS * A_DQK
A_V = A_HEADS * A_DV
B_WINDOWS = (128, 512, 2048)
B_DILATIONS = (1, 4, 16)
B_GROUPS = 3
B_HEADS = 8
B_HEAD_DIM = 64
B_WIDTH = B_HEADS * B_HEAD_DIM
B_WB = 128
ROPE_THETA = 10000.0
MOE_GROUPS = 4
MOE_EPG = 8
MOE_EXPERTS = MOE_GROUPS * MOE_EPG
MOE_HIDDEN = D_MODEL // 4

LANES = 128
SUBLANES = 8
VMEM_LIMIT = 56 * 1024 * 1024

A_CHUNK = 256
A_STEP = 512
SORT_T = 512
SEG_Q = 16
SORT_R = 1536
PERM_BLK = 256
EXP_BM = 512
EXP_CPB = EXP_BM // SEG_Q
LIST_UNROLL = 4
ROW_T = 1024
ATT_J = (8, 8, 4)
MOD_TN = 1536


def _cparams(sem):
    return pltpu.CompilerParams(dimension_semantics=sem, vmem_limit_bytes=VMEM_LIMIT)


def _dot(a, b):
    return jnp.dot(a, b, preferred_element_type=F32)


def _dot_nt(a, b):
    return lax.dot_general(a, b, (((1,), (1,)), ((), ())), preferred_element_type=F32)


def _dot_tn(a, b):
    return lax.dot_general(a, b, (((0,), (0,)), ((), ())), preferred_element_type=F32)


def _layer_norm(v, g, b):
    mu = jnp.mean(v, axis=-1, keepdims=True)
    vc = v - mu
    var = jnp.mean(vc * vc, axis=-1, keepdims=True)
    return vc * lax.rsqrt(var + LN_EPS) * g + b


def _sigmoid(v):
    return 1.0 / (1.0 + jnp.exp(-v))


def _split_hi_lo(v):
    hi = v.astype(BF16)
    return hi, (v - hi.astype(F32)).astype(BF16)


def _route_tile(h2, wr_ref, br_ref):
    h_hi, h_lo = _split_hi_lo(h2)
    lg = _dot(h_hi, wr_ref[0]) + (_dot(h_hi, wr_ref[1]) + _dot(h_lo, wr_ref[0])) + br_ref[...]
    lane = lax.broadcasted_iota(I32, lg.shape, 1).astype(F32)
    neg = -jnp.inf
    big = 1000.0
    m1 = jnp.where(lane < MOE_GROUPS, lg, neg)
    mx = jnp.max(m1, axis=-1, keepdims=True)
    pg = 1.0 / jnp.sum(jnp.exp(m1 - mx), axis=-1, keepdims=True)
    gi = jnp.min(jnp.where(m1 == mx, lane, big), axis=-1, keepdims=True)
    lo = MOE_GROUPS + gi * MOE_EPG
    m2 = jnp.where((lane >= lo) & (lane < lo + MOE_EPG), lg, neg)
    v0 = jnp.max(m2, axis=-1, keepdims=True)
    j0 = jnp.min(jnp.where(m2 == v0, lane, big), axis=-1, keepdims=True)
    m3 = jnp.where(lane == j0, neg, m2)
    v1 = jnp.max(m3, axis=-1, keepdims=True)
    j1 = jnp.min(jnp.where(m3 == v1, lane, big), axis=-1, keepdims=True)
    t = jnp.exp(v1 - v0)
    wa = 1.0 / (1.0 + t)
    wb = t * wa
    out = jnp.where(lane == 0, j0 - MOE_GROUPS,
          jnp.where(lane == 1, j1 - MOE_GROUPS,
          jnp.where(lane == 2, pg * wa,
          jnp.where(lane == 3, pg * wb, 0.0))))
    return out


def _mod_kernel(c_ref, w_ref, b_ref, o_ref):
    c = c_ref[...]
    c_hi, c_lo = _split_hi_lo(c * _sigmoid(c))
    w_hi, w_lo = _split_hi_lo(w_ref[0])
    o_ref[0] = _dot(c_hi, w_hi) + (_dot(c_hi, w_lo) + _dot(c_lo, w_hi)) + b_ref[0]


def _adaln_mod(c, ada_w, ada_b):
    bsz, d = c.shape
    depth, _, n6 = ada_w.shape
    rows = SUBLANES
    c_pad = jnp.zeros((rows, d), F32).at[:bsz].set(c)
    out = pl.pallas_call(
        _mod_kernel,
        grid=(depth, n6 // MOD_TN),
        in_specs=[pl.BlockSpec((rows, d), lambda l, j: (0, 0)),
                  pl.BlockSpec((1, d, MOD_TN), lambda l, j: (l, 0, j)),
                  pl.BlockSpec((1, 1, MOD_TN), lambda l, j: (l, 0, j))],
        out_specs=pl.BlockSpec((1, rows, MOD_TN), lambda l, j: (l, 0, j)),
        out_shape=jax.ShapeDtypeStruct((depth, rows, n6), F32),
        compiler_params=_cparams(("parallel", "parallel")),
        name="adaln_mod",
    )(c_pad, ada_w, ada_b.reshape(depth, 1, n6))
    return out[:, :bsz].reshape(depth, bsz, 6, d)


def _mlstm_layer_kernel(x_ref, mod_ref, win_ref, bg_ref, ng_ref, wout_ref, lng_ref, lnb_ref,
                        wr_ref, br_ref, x1_ref, h2_ref, route_ref, c_ref, n_ref, m_ref):
    L = A_CHUNK
    n_chunks = A_STEP // L

    @pl.when(pl.program_id(1) == 0)
    def _():
        c_ref[...] = jnp.zeros_like(c_ref)
        n_ref[...] = jnp.zeros_like(n_ref)
        m_ref[...] = jnp.zeros_like(m_ref)

    md = mod_ref[...]
    sh1, sc1, g1, sh2, sc2, g2 = [md[j:j + 1, :] for j in range(6)]
    o_q, o_k, o_v, o_o, o_g = 0, A_QK, 2 * A_QK, 2 * A_QK + A_V, 2 * A_QK + 2 * A_V

    row = lax.broadcasted_iota(I32, (L, L), 0)
    col = lax.broadcasted_iota(I32, (L, L), 1)
    causal = row >= col
    tri = jnp.where(causal, 1.0, 0.0).astype(BF16)

    def proj_parts(ci):
        rows = slice(ci * L, (ci + 1) * L)
        out = {}

        def start():
            out["hb"] = (x_ref[rows, :] * (1.0 + sc1) + sh1).astype(BF16)
            out["q"] = _dot(out["hb"], win_ref[:, o_q:o_k])

        def kpart():
            out["k"] = _dot(out["hb"], win_ref[:, o_k:o_v]) * (A_DQK ** -0.5)

        def vpart():
            out["v"] = _dot(out["hb"], win_ref[:, o_v:o_o])

        def opart():
            out["og"] = _dot(out["hb"], win_ref[:, o_o:o_g])
            out["gates"] = _dot(out["hb"], win_ref[:, o_g:o_g + LANES]) + bg_ref[...]

        return out, [start, kpart, vpart, opart]

    def gate_terms(pr):
        gates = pr["gates"]
        lf = jnp.minimum(gates, 0.0) - jnp.log(1.0 + jnp.exp(-jnp.abs(gates)))
        lf = pltpu.roll(lf, LANES - A_HEADS, 1)
        lf_hi, lf_lo = _split_hi_lo(lf)
        lf_lo2 = (lf - lf_hi.astype(F32) - lf_lo.astype(F32)).astype(BF16)
        b_col = _dot(tri, lf_hi) + (_dot(tri, lf_lo) + _dot(tri, lf_lo2))
        g_col = gates - b_col
        pm = g_col
        sh = 1
        while sh < L:
            pm = jnp.maximum(pm, jnp.concatenate([jnp.full((sh, LANES), -jnp.inf, F32), pm[:L - sh]], axis=0))
            sh *= 2
        return b_col, g_col, pm, g_col.T

    def head(pr, gt, h):
        b_col, g_col, pm, g_t = gt
        qh = pr["q"][:, h * A_DQK:(h + 1) * A_DQK]
        kh = pr["k"][:, h * A_DQK:(h + 1) * A_DQK]
        vh = pr["v"][:, h * A_DV:(h + 1) * A_DV]
        qb, kb, vb = qh.astype(BF16), kh.astype(BF16), vh.astype(BF16)
        b_c = b_col[:, h:h + 1]
        g_c = g_col[:, h:h + 1]
        g_r = g_t[h:h + 1, :]
        m_st = m_ref[h:h + 1, 0:1]
        c_st = c_ref[h]
        n_st = n_ref[h:h + 1, :]

        a_c = jnp.maximum(m_st, pm[:, h:h + 1])
        dw = jnp.exp(jnp.where(causal, g_r - a_c, -jnp.inf))
        iw = jnp.exp(m_st - a_c)
        sc = _dot_nt(qb, kb) * dw
        num = _dot(sc.astype(BF16), vb) + iw * _dot(qb, c_st.astype(BF16))
        den = jnp.sum(sc, axis=-1, keepdims=True) + iw * jnp.sum(qh * n_st, axis=-1, keepdims=True)
        hh = num * (1.0 / jnp.maximum(jnp.abs(den), jnp.exp(-(b_c + a_c))))

        a_last = a_c[L - 1:L, :]
        ws = jnp.exp(g_c - a_last)
        decay = jnp.exp(m_st - a_last)
        kw = kh * ws
        c_ref[h] = decay * c_st + _dot_tn(kw.astype(BF16), vb)
        n_ref[h:h + 1, :] = decay * n_st + jnp.sum(kw, axis=0, keepdims=True)
        m_ref[h:h + 1, :] = jnp.broadcast_to(b_c[L - 1:L, :] + a_last, (1, LANES))

        mu = jnp.mean(hh, axis=-1, keepdims=True)
        hc = hh - mu
        var = jnp.mean(hc * hc, axis=-1, keepdims=True)
        hn = hc * lax.rsqrt(var + A_NORM_EPS)
        og = pr["og"][:, h * A_DV:(h + 1) * A_DV]
        return (hn * ng_ref[:, h * A_DV:(h + 1) * A_DV] * _sigmoid(og)).astype(BF16)

    def tail_parts(ci, ys):
        rows = slice(ci * L, (ci + 1) * L)
        st = {}

        def outproj():
            st["y"] = _dot(jnp.concatenate(ys, axis=1), wout_ref[...])

        def norm():
            x1 = _layer_norm(ALPHA * x_ref[rows, :] + (1.0 + g1) * st["y"], lng_ref[...], lnb_ref[...])
            x1_ref[rows, :] = x1
            st["h2"] = x1 * (1.0 + sc2) + sh2
            h2_ref[rows, :] = st["h2"].astype(h2_ref.dtype)

        def router():
            route_ref[rows, :] = _route_tile(st["h2"], wr_ref, br_ref)

        return [outproj, norm, router]

    pr, parts = proj_parts(0)
    for part in parts:
        part()
    pending_tail = []
    for ci in range(n_chunks):
        nxt, nxt_parts = proj_parts(ci + 1) if ci + 1 < n_chunks else (None, [])
        fill = nxt_parts + pending_tail
        gt = gate_terms(pr)
        ys = []
        for h in range(A_HEADS):
            ys.append(head(pr, gt, h))
            share = -(-len(fill) // (A_HEADS - h))
            for part in fill[:share]:
                part()
            fill = fill[share:]
        pending_tail = tail_parts(ci, ys)
        pr = nxt
    for part in pending_tail:
        part()


def _mlstm_layer(x, mod_l, win, bg, ng, wout, lng, lnb, wr, br):
    bsz, s, d = x.shape
    nw = win.shape[1]
    const = lambda b, i: (0, 0)
    tok = pl.BlockSpec((None, A_STEP, d), lambda b, i: (b, i, 0))
    return pl.pallas_call(
        _mlstm_layer_kernel,
        grid=(bsz, s // A_STEP),
        in_specs=[tok,
                  pl.BlockSpec((None, 6, d), lambda b, i: (b, 0, 0)),
                  pl.BlockSpec((d, nw), const),
                  pl.BlockSpec((1, LANES), const),
                  pl.BlockSpec((1, A_V), const),
                  pl.BlockSpec((A_V, d), const),
                  pl.BlockSpec((1, d), const),
                  pl.BlockSpec((1, d), const),
                  pl.BlockSpec((2, d, LANES), lambda b, i: (0, 0, 0)),
                  pl.BlockSpec((1, LANES), const)],
        out_specs=[tok, tok, pl.BlockSpec((None, A_STEP, LANES), lambda b, i: (b, i, 0))],
        out_shape=[jax.ShapeDtypeStruct((bsz, s, d), F32),
                   jax.ShapeDtypeStruct((bsz, s, d), BF16),
                   jax.ShapeDtypeStruct((bsz, s, LANES), F32)],
        scratch_shapes=[pltpu.VMEM((A_HEADS, A_DQK, A_DV), F32),
                        pltpu.VMEM((SUBLANES, A_DQK), F32),
                        pltpu.VMEM((SUBLANES, LANES), F32)],
        compiler_params=_cparams(("parallel", "arbitrary")),
        name="mlstm_layer",
    )(x, mod_l, win, bg, ng, wout, lng, lnb, wr, br)


def _moe_sort_kernel(h2_ref, route_ref, xs_ref, ws_ref, pos_ref, cnt_ref):
    n, nr = SORT_T, SORT_R
    r = route_ref[...]
    lane = lax.broadcasted_iota(I32, (n, LANES), 1).astype(F32)
    oh0 = lane == r[:, 0:1]
    oh1 = lane == r[:, 1:2]
    oh = jnp.where(oh0 | oh1, 1.0, 0.0)
    row = lax.broadcasted_iota(I32, (n, n), 0)
    col = lax.broadcasted_iota(I32, (n, n), 1)
    before = jnp.where(row > col, 1.0, 0.0).astype(BF16)
    excl = _dot(before, oh.astype(BF16))
    cnt = jnp.sum(oh, axis=0, keepdims=True)
    pcnt = jnp.ceil(cnt * (1.0 / SEG_Q)) * SEG_Q
    ej = lax.broadcasted_iota(I32, (LANES, LANES), 0)
    el = lax.broadcasted_iota(I32, (LANES, LANES), 1)
    upper = jnp.where(ej < el, 1.0, 0.0).astype(F32)
    seg = jnp.dot(jnp.broadcast_to(pcnt, (SUBLANES, LANES)), upper,
                  preferred_element_type=F32, precision=HI)[0:1, :]
    base = excl + seg
    p0 = jnp.sum(jnp.where(oh0, base, 0.0), axis=-1, keepdims=True)
    p1 = jnp.sum(jnp.where(oh1, base, 0.0), axis=-1, keepdims=True)
    pos = jnp.where(lane == 0, p0, jnp.where(lane == 1, p1, 0.0))
    pos_ref[...] = pos
    cnt_ref[...] = cnt
    pos_t = pos.T

    def split3(w):
        hi = w.astype(BF16)
        r1 = w - hi.astype(F32)
        mid = r1.astype(BF16)
        lo = (r1 - mid.astype(F32)).astype(BF16)
        return jnp.where(lane == 0, hi.astype(F32),
               jnp.where(lane == 1, mid.astype(F32),
               jnp.where(lane == 2, lo.astype(F32), 0.0))).astype(BF16)

    w0, w1 = split3(r[:, 2:3]), split3(r[:, 3:4])
    h2 = h2_ref[...]
    for kb in range(nr // PERM_BLK):
        rows = slice(kb * PERM_BLK, (kb + 1) * PERM_BLK)
        ri = (lax.broadcasted_iota(I32, (PERM_BLK, n), 0) + kb * PERM_BLK).astype(F32)
        sel0 = ri == pos_t[0:1, :]
        sel1 = ri == pos_t[1:2, :]
        perm = jnp.where(sel0 | sel1, 1.0, 0.0).astype(BF16)
        xs_ref[rows, :] = _dot(perm, h2).astype(BF16)
        wsum = (_dot(jnp.where(sel0, 1.0, 0.0).astype(BF16), w0)
                + _dot(jnp.where(sel1, 1.0, 0.0).astype(BF16), w1))
        ws = wsum[:, 0:1] + wsum[:, 1:2] + wsum[:, 2:3]
        ws_ref[rows, :] = jnp.broadcast_to(ws, (PERM_BLK, LANES))


def _chunk_list_kernel(seg_ref, nch_ref, src_ref, be_ref, bc_ref, nb_ref, *, n_tiles):
    n_src = src_ref.shape[0]
    n_blk = be_ref.shape[0]

    def clear_src(k, c):
        src_ref[k] = 0
        return c

    lax.fori_loop(0, n_src, clear_src, 0, unroll=16)

    def per_expert(e, carry):
        k0, nb0 = carry

        def per_tile(i, k):
            s = seg_ref[i * MOE_EXPERTS + e]
            n = nch_ref[i * MOE_EXPERTS + e]

            row0 = i * SORT_R + s
            for c in range(LIST_UNROLL):
                @pl.when(c < n)
                def _(c=c):
                    src_ref[k + c] = row0 + c * SEG_Q

            def per_chunk(c, carry):
                src_ref[k + c] = row0 + c * SEG_Q
                return carry

            lax.fori_loop(LIST_UNROLL, n, per_chunk, 0)
            return k + n

        k1 = lax.fori_loop(0, n_tiles, per_tile, k0)
        tot = k1 - k0
        nblk = (tot + EXP_CPB - 1) // EXP_CPB

        def per_block(j, c):
            be_ref[nb0 + j] = e
            bc_ref[nb0 + j] = jnp.minimum(EXP_CPB, tot - j * EXP_CPB)
            return c

        lax.fori_loop(0, nblk, per_block, 0)
        return k0 + nblk * EXP_CPB, nb0 + nblk

    _, nb = lax.fori_loop(0, MOE_EXPERTS, per_expert, (jnp.int32(0), jnp.int32(0)))
    nb_ref[0] = nb

    def spare_block(j, c):
        be_ref[j] = MOE_EXPERTS - 1
        bc_ref[j] = 0
        return c

    lax.fori_loop(nb, n_blk, spare_block, 0)


def _chunk_copy(src_ref, src_row, dst_ref, dst_row, sem):
    return pltpu.make_async_copy(src_ref.at[pl.ds(pl.multiple_of(src_row, SEG_Q), SEG_Q), :],
                                 dst_ref.at[pl.ds(pl.multiple_of(dst_row, SEG_Q), SEG_Q), :], sem)


def _expert_kernel(src_ref, be_ref, bc_ref, nb_ref, wg_ref, wu_ref, wd_ref, xs_ref, ys_ref,
                   wgu_s, wd_s, xg, yg, sem_in, sem_out):
    b = pl.program_id(0)
    last = pl.num_programs(0) - 1
    nb = nb_ref[0]

    def clamp(blk):
        return jnp.clip(blk, 0, last)

    def count(blk):
        return jnp.where((blk >= 0) & (blk <= last), bc_ref[clamp(blk)], 0)

    def per_chunk(n, fn):
        for c in range(EXP_CPB):
            @pl.when(c < n)
            def _(c=c):
                fn(c)

    def gather(blk, slot):
        per_chunk(count(blk), lambda c: _chunk_copy(
            xs_ref, src_ref[clamp(blk) * EXP_CPB + c], xg.at[slot], c * SEG_Q, sem_in.at[slot]).start())

    def wait_gather(blk, slot):
        per_chunk(count(blk), lambda c: _chunk_copy(xs_ref, 0, xg.at[slot], 0, sem_in.at[slot]).wait())

    def scatter(blk, slot):
        per_chunk(count(blk), lambda c: _chunk_copy(
            yg.at[slot], c * SEG_Q, ys_ref, src_ref[clamp(blk) * EXP_CPB + c], sem_out.at[slot]).start())

    def wait_scatter(blk, slot):
        per_chunk(count(blk), lambda c: _chunk_copy(yg.at[slot], 0, ys_ref, 0, sem_out.at[slot]).wait())

    @pl.when(b == 0)
    def _():
        xg[...] = jnp.zeros_like(xg)
        gather(0, 0)

    @pl.when((b == 0) | (be_ref[b] != be_ref[jnp.maximum(b - 1, 0)]))
    def _():
        wgu_s[:, 0:MOE_HIDDEN] = wg_ref[...].astype(BF16)
        wgu_s[:, MOE_HIDDEN:2 * MOE_HIDDEN] = wu_ref[...].astype(BF16)
        wd_s[...] = wd_ref[...].astype(BF16)

    @pl.when(b < nb)
    def _():
        slot = b & 1
        gather(b + 1, 1 - slot)
        wait_gather(b, slot)
        gu = _dot(xg[slot], wgu_s[...])
        g = gu[:, 0:MOE_HIDDEN]
        u = gu[:, MOE_HIDDEN:2 * MOE_HIDDEN]
        y = _dot((g * _sigmoid(g) * u).astype(BF16), wd_s[...])
        wait_scatter(b - 2, slot)
        yg[slot] = y.astype(BF16)
        scatter(b, slot)

    @pl.when(b == last)
    def _():
        wait_scatter(nb - 2, nb & 1)
        wait_scatter(nb - 1, (nb - 1) & 1)


def _combine_kernel(ys_ref, ws_ref, pos_ref, x1_ref, mod_ref, lng_ref, lnb_ref, x2_ref):
    n, nr = SORT_T, SORT_R
    pos = pos_ref[...]
    y = jnp.zeros((n, D_MODEL), F32)
    for kb in range(nr // PERM_BLK):
        rows = slice(kb * PERM_BLK, (kb + 1) * PERM_BLK)
        ysw = (ys_ref[rows, :].astype(F32) * ws_ref[rows, 0:1]).astype(BF16)
        ci = (lax.broadcasted_iota(I32, (n, PERM_BLK), 1) + kb * PERM_BLK).astype(F32)
        unperm = jnp.where((ci == pos[:, 0:1]) | (ci == pos[:, 1:2]), 1.0, 0.0).astype(BF16)
        y = y + _dot(unperm, ysw)
    g2 = mod_ref[5:6, :]
    x2_ref[...] = _layer_norm(ALPHA * x1_ref[...] + (1.0 + g2) * y, lng_ref[...], lnb_ref[...])


def _moe_and_norm(x1, h2, route, mod_l, lng, lnb, wg, wu, wd, layer):
    bsz, s, d = x1.shape
    n_tok = bsz * s
    nt = n_tok // SORT_T
    n_rows = nt * SORT_R
    x1f = x1.reshape(n_tok, d)
    h2f = h2.reshape(n_tok, d)
    rt = route.reshape(n_tok, LANES)

    xs, ws, pos, cnt = pl.pallas_call(
        _moe_sort_kernel,
        grid=(nt,),
        in_specs=[pl.BlockSpec((SORT_T, d), lambda i: (i, 0)),
                  pl.BlockSpec((SORT_T, LANES), lambda i: (i, 0))],
        out_specs=[pl.BlockSpec((SORT_R, d), lambda i: (i, 0)),
                   pl.BlockSpec((SORT_R, LANES), lambda i: (i, 0)),
                   pl.BlockSpec((SORT_T, LANES), lambda i: (i, 0)),
                   pl.BlockSpec((None, 1, LANES), lambda i: (i, 0, 0))],
        out_shape=[jax.ShapeDtypeStruct((n_rows, d), BF16),
                   jax.ShapeDtypeStruct((n_rows, LANES), F32),
                   jax.ShapeDtypeStruct((n_tok, LANES), F32),
                   jax.ShapeDtypeStruct((nt, 1, LANES), F32)],
        compiler_params=_cparams(("parallel",)),
        name="moe_sort",
    )(h2f, rt)

    cnt_i = cnt[:, 0, :MOE_EXPERTS].astype(I32)
    pcnt = (cnt_i + SEG_Q - 1) // SEG_Q * SEG_Q
    seg = (jnp.cumsum(pcnt, axis=1) - pcnt).reshape(-1).astype(I32)
    nch = (pcnt // SEG_Q).reshape(-1).astype(I32)

    n_src = n_rows // SEG_Q + MOE_EXPERTS * EXP_CPB
    n_blk = n_src // EXP_CPB
    smem = pl.BlockSpec(memory_space=pltpu.SMEM)
    src, block_e, block_n, n_used = pl.pallas_call(
        functools.partial(_chunk_list_kernel, n_tiles=nt),
        in_specs=[smem, smem],
        out_specs=[smem, smem, smem, smem],
        out_shape=[jax.ShapeDtypeStruct((n_src,), I32),
                   jax.ShapeDtypeStruct((n_blk,), I32),
                   jax.ShapeDtypeStruct((n_blk,), I32),
                   jax.ShapeDtypeStruct((1,), I32)],
        name="moe_chunk_list",
    )(seg, nch)

    ys = pl.pallas_call(
        _expert_kernel,
        grid_spec=pltpu.PrefetchScalarGridSpec(
            num_scalar_prefetch=4,
            grid=(n_blk,),
            in_specs=[pl.BlockSpec((None, None, d, MOE_HIDDEN), lambda b, src, be, bc, nb: (layer, be[b], 0, 0)),
                      pl.BlockSpec((None, None, d, MOE_HIDDEN), lambda b, src, be, bc, nb: (layer, be[b], 0, 0)),
                      pl.BlockSpec((None, None, MOE_HIDDEN, d), lambda b, src, be, bc, nb: (layer, be[b], 0, 0)),
                      pl.BlockSpec(memory_space=pl.ANY)],
            out_specs=pl.BlockSpec(memory_space=pl.ANY),
            scratch_shapes=[pltpu.VMEM((d, 2 * MOE_HIDDEN), BF16),
                            pltpu.VMEM((MOE_HIDDEN, d), BF16),
                            pltpu.VMEM((2, EXP_BM, d), BF16),
                            pltpu.VMEM((2, EXP_BM, d), BF16),
                            pltpu.SemaphoreType.DMA((2,)),
                            pltpu.SemaphoreType.DMA((2,))]),
        out_shape=jax.ShapeDtypeStruct((n_rows, d), BF16),
        input_output_aliases={7: 0},
        compiler_params=_cparams(("arbitrary",)),
        name="moe_experts",
    )(src, block_e, block_n, n_used, wg, wu, wd, xs)

    per_b = s // SORT_T
    x2 = pl.pallas_call(
        _combine_kernel,
        grid=(nt,),
        in_specs=[pl.BlockSpec((SORT_R, d), lambda i: (i, 0)),
                  pl.BlockSpec((SORT_R, LANES), lambda i: (i, 0)),
                  pl.BlockSpec((SORT_T, LANES), lambda i: (i, 0)),
                  pl.BlockSpec((SORT_T, d), lambda i: (i, 0)),
                  pl.BlockSpec((None, 6, d), lambda i: (i // per_b, 0, 0)),
                  pl.BlockSpec((1, d), lambda i: (0, 0)),
                  pl.BlockSpec((1, d), lambda i: (0, 0))],
        out_specs=pl.BlockSpec((SORT_T, d), lambda i: (i, 0)),
        out_shape=jax.ShapeDtypeStruct((n_tok, d), F32),
        compiler_params=_cparams(("parallel",)),
        name="moe_combine_norm",
    )(ys, ws, pos, x1f, mod_l, lng, lnb)
    return x2.reshape(bsz, s, d)


def _rope_table_kernel(pos_ref, inv_ref, sgn_ref, cos_ref, sin_ref):
    ang = pos_ref[...] * inv_ref[...]
    cos_ref[...] = jnp.cos(ang)
    sin_ref[...] = jnp.sin(ang) * sgn_ref[...]


def _rope_tables(positions):
    bsz, s = positions.shape
    half = B_HEAD_DIM // 2
    inv = jnp.power(jnp.float32(ROPE_THETA), -jnp.arange(half, dtype=F32) * 2.0 / B_HEAD_DIM)
    inv_row = jnp.tile(inv, LANES // half).reshape(1, LANES)
    sgn_row = jnp.tile(jnp.concatenate([-jnp.ones((half,), F32), jnp.ones((half,), F32)]),
                       LANES // B_HEAD_DIM).reshape(1, LANES)
    pos = jnp.broadcast_to(positions.astype(F32)[:, :, None], (bsz, s, LANES))
    tok = pl.BlockSpec((None, ROW_T, LANES), lambda b, i: (b, i, 0))
    row = pl.BlockSpec((1, LANES), lambda b, i: (0, 0))
    return pl.pallas_call(
        _rope_table_kernel,
        grid=(bsz, s // ROW_T),
        in_specs=[tok, row, row],
        out_specs=[tok, tok],
        out_shape=[jax.ShapeDtypeStruct((bsz, s, LANES), F32)] * 2,
        compiler_params=_cparams(("parallel", "parallel")),
        name="rope_tables",
    )(pos, inv_row, sgn_row)


def _rope(t, cosf, sinf, first_half):
    n = t.shape[1]
    half = B_HEAD_DIM // 2
    fwd = pltpu.roll(t, n - half, 1)
    bwd = pltpu.roll(t, half, 1)
    return t * cosf + jnp.where(first_half, fwd, bwd) * sinf


def _rope_operands(cos_ref, sin_ref, rows):
    reps = B_WIDTH // LANES
    cosf = jnp.concatenate([cos_ref[...]] * reps, axis=1)
    sinf = jnp.concatenate([sin_ref[...]] * reps, axis=1)
    lane = lax.broadcasted_iota(I32, (rows, B_WIDTH), 1)
    first_half = (lane & (B_HEAD_DIM - 1)) < (B_HEAD_DIM // 2)
    return cosf, sinf, first_half


def _store_by_residue(val, out_ref, stage_ref, dil):
    rows, width = val.shape
    if dil == 1:
        out_ref[0] = val.astype(out_ref.dtype)
        return
    for c in range(width // LANES):
        stage_ref[c] = val[:, c * LANES:(c + 1) * LANES]
    for r in range(dil):
        for c in range(width // LANES):
            out_ref[r, :, c * LANES:(c + 1) * LANES] = (
                stage_ref[c, pl.ds(r, rows // dil, stride=dil), :].astype(out_ref.dtype))


def _load_by_residue(in_ref, stage_ref, dil):
    _, sub, width = in_ref.shape
    if dil == 1:
        return in_ref[0].astype(F32)
    for r in range(dil):
        for c in range(width // LANES):
            stage_ref[c, pl.ds(r, sub, stride=dil), :] = in_ref[r, :, c * LANES:(c + 1) * LANES].astype(F32)
    return jnp.concatenate([stage_ref[c] for c in range(width // LANES)], axis=1)


def _class_spec(dil, width):
    return pl.BlockSpec((None, dil, ROW_T // dil, width), lambda b, i: (b, 0, i, 0))


def _class_shape(bsz, s, dil, width, dtype):
    return jax.ShapeDtypeStruct((bsz, dil, s // dil, width), dtype)


def _kv_kernel(x_ref, w_ref, cos_ref, sin_ref, k0, k1, k2, v0, v1, v2, stage_ref):
    xb = x_ref[...].astype(BF16)
    cosf, sinf, first_half = _rope_operands(cos_ref, sin_ref, ROW_T)
    for g, (k_ref, v_ref) in enumerate(((k0, v0), (k1, v1), (k2, v2))):
        kg = _dot(xb, w_ref[:, g * B_WIDTH:(g + 1) * B_WIDTH])
        _store_by_residue(_rope(kg, cosf, sinf, first_half), k_ref, stage_ref, B_DILATIONS[g])
        vo = (B_GROUPS + g) * B_WIDTH
        _store_by_residue(_dot(xb, w_ref[:, vo:vo + B_WIDTH]), v_ref, stage_ref, B_DILATIONS[g])


def _kv_project(x, wkv, cos_t, sin_t):
    bsz, s, d = x.shape
    tok = lambda w: pl.BlockSpec((None, ROW_T, w), lambda b, i: (b, i, 0))
    dils = B_DILATIONS * 2
    return pl.pallas_call(
        _kv_kernel,
        grid=(bsz, s // ROW_T),
        in_specs=[tok(d), pl.BlockSpec(wkv.shape, lambda b, i: (0, 0)), tok(LANES), tok(LANES)],
        out_specs=[_class_spec(dl, B_WIDTH) for dl in dils],
        out_shape=[_class_shape(bsz, s, dl, B_WIDTH, BF16) for dl in dils],
        scratch_shapes=[pltpu.VMEM((B_WIDTH // LANES, ROW_T, LANES), F32)],
        compiler_params=_cparams(("parallel", "parallel")),
        name="kv_project",
    )(x, wkv, cos_t, sin_t)


def _q_kernel(x_ref, mod_ref, w_ref, cos_ref, sin_ref, q0, q1, q2, stage_ref):
    md = mod_ref[...]
    hb = (x_ref[...] * (1.0 + md[1:2, :]) + md[0:1, :]).astype(BF16)
    cosf, sinf, first_half = _rope_operands(cos_ref, sin_ref, ROW_T)
    for g, q_ref in enumerate((q0, q1, q2)):
        qg = _dot(hb, w_ref[:, g * B_WIDTH:(g + 1) * B_WIDTH])
        _store_by_residue(_rope(qg, cosf, sinf, first_half) * (B_HEAD_DIM ** -0.5), q_ref, stage_ref,
                          B_DILATIONS[g])


def _q_project(x, mod_l, wq, cos_t, sin_t):
    bsz, s, d = x.shape
    tok = lambda w: pl.BlockSpec((None, ROW_T, w), lambda b, i: (b, i, 0))
    return pl.pallas_call(
        _q_kernel,
        grid=(bsz, s // ROW_T),
        in_specs=[tok(d), pl.BlockSpec((None, 6, d), lambda b, i: (b, 0, 0)),
                  pl.BlockSpec(wq.shape, lambda b, i: (0, 0)), tok(LANES), tok(LANES)],
        out_specs=[_class_spec(dl, B_WIDTH) for dl in B_DILATIONS],
        out_shape=[_class_shape(bsz, s, dl, B_WIDTH, BF16) for dl in B_DILATIONS],
        scratch_shapes=[pltpu.VMEM((B_WIDTH // LANES, ROW_T, LANES), F32)],
        compiler_params=_cparams(("parallel", "parallel")),
        name="q_project",
    )(x, mod_l, wq, cos_t, sin_t)


def _attn_kernel(q_ref, kc_ref, kp_ref, vc_ref, vp_ref, o_ref, lse_ref, kw_ref, vx_ref, *, nsub):
    wb = B_WB
    npair = B_WIDTH // LANES
    first_step = pl.program_id(2) == 0
    kw_ref[0:wb, :] = kp_ref[...]
    kw_ref[wb:, :] = kc_ref[...]
    for p in range(npair):
        vx_ref[p, 0:wb, 0:LANES] = vp_ref[:, p * LANES:(p + 1) * LANES]
        vx_ref[p, wb:, 0:LANES] = vc_ref[:, p * LANES:(p + 1) * LANES]
        vx_ref[p, :, LANES:2 * LANES] = jnp.ones((vx_ref.shape[1], LANES), BF16)
    qi = lax.broadcasted_iota(I32, (wb, 2 * wb), 0)
    kj = lax.broadcasted_iota(I32, (wb, 2 * wb), 1)
    band = (kj >= qi) & (kj <= qi + wb)
    lane = lax.broadcasted_iota(I32, (wb, LANES), 1)
    low = lane < B_HEAD_DIM

    for j in range(nsub):
        r0 = j * wb
        valid = band & ((kj >= wb) | jnp.logical_not(first_step)) if j == 0 else band
        q = q_ref[pl.ds(r0, wb), :]
        kwin = kw_ref[pl.ds(r0, 2 * wb), :]
        outs = []
        m_all = jnp.zeros((wb, LANES), F32)
        l_all = jnp.ones((wb, LANES), F32)
        for p in range(npair):
            qp = q[:, p * LANES:(p + 1) * LANES]
            kpair = kwin[:, p * LANES:(p + 1) * LANES]
            vx = vx_ref[p, pl.ds(r0, 2 * wb), :]
            acc = lsum = None
            for hh in range(2):
                mine = low if hh == 0 else jnp.logical_not(low)
                qm = jnp.where(mine, qp, jnp.zeros_like(qp))
                sc = jnp.where(valid, _dot_nt(qm, kpair), -jnp.inf)
                m = jnp.max(sc, axis=-1, keepdims=True)
                pv = _dot(jnp.exp(sc - m).astype(BF16), vx)
                o_h, l_h = pv[:, 0:LANES], pv[:, LANES:2 * LANES]
                m_all = jnp.where(lane == 2 * p + hh, m, m_all)
                l_all = jnp.where(lane == 2 * p + hh, l_h, l_all)
                acc = o_h if hh == 0 else jnp.where(low, acc, o_h)
                lsum = l_h if hh == 0 else jnp.where(low, lsum, l_h)
            outs.append(acc * (1.0 / lsum))
        o_ref[pl.ds(r0, wb), :] = jnp.concatenate(outs, axis=1).astype(BF16)
        lse_ref[pl.ds(r0, wb), :] = m_all + jnp.log(l_all)


def _dilated_attention(q, k, v, nsub):
    bsz, dil, rows, w = q.shape
    blk = B_WB * nsub
    cur = lambda width: pl.BlockSpec((None, None, blk, width), lambda b, r, n: (b, r, n, 0))
    prev = pl.BlockSpec((None, None, B_WB, w), lambda b, r, n: (b, r, jnp.maximum(n * nsub - 1, 0), 0))
    return pl.pallas_call(
        functools.partial(_attn_kernel, nsub=nsub),
        grid=(bsz, dil, rows // blk),
        in_specs=[cur(w), cur(w), prev, cur(w), prev],
        out_specs=[cur(w), cur(LANES)],
        out_shape=[jax.ShapeDtypeStruct((bsz, dil, rows, w), BF16),
                   jax.ShapeDtypeStruct((bsz, dil, rows, LANES), F32)],
        scratch_shapes=[pltpu.VMEM((blk + B_WB, w), BF16),
                        pltpu.VMEM((w // LANES, blk + B_WB, 2 * LANES), BF16)],
        compiler_params=_cparams(("parallel", "parallel", "arbitrary")),
        name=f"dilated_attention_d{dil}",
    )(q, k, k, v, v)


def _attn_out_kernel(x_ref, mod_ref, o0, o1, o2, l0, l1, l2, wo_ref, lng_ref, lnb_ref, wr_ref, br_ref,
                     x1_ref, h2_ref, route_ref, stage_ref):
    md = mod_ref[...]
    g1, sh2, sc2 = md[2:3, :], md[3:4, :], md[4:5, :]
    lses = [_load_by_residue(l_ref, stage_ref, dl) for l_ref, dl in zip((l0, l1, l2), B_DILATIONS)]
    mx = jnp.maximum(jnp.maximum(lses[0], lses[1]), lses[2])
    es = [jnp.exp(l - mx) for l in lses]
    inv = 1.0 / (es[0] + es[1] + es[2])

    def per_head(wc):
        return jnp.concatenate([jnp.broadcast_to(wc[:, h:h + 1], (ROW_T, B_HEAD_DIM)) for h in range(B_HEADS)],
                               axis=1)

    o = jnp.zeros((ROW_T, B_WIDTH), F32)
    for o_ref, e, dl in zip((o0, o1, o2), es, B_DILATIONS):
        o = o + per_head(e * inv) * _load_by_residue(o_ref, stage_ref, dl)
    y = _dot(o.astype(BF16), wo_ref[...])
    x1 = _layer_norm(ALPHA * x_ref[...] + (1.0 + g1) * y, lng_ref[...], lnb_ref[...])
    x1_ref[...] = x1
    h2 = x1 * (1.0 + sc2) + sh2
    h2_ref[...] = h2.astype(h2_ref.dtype)
    route_ref[...] = _route_tile(h2, wr_ref, br_ref)


def _attn_out_layer(x, mod_l, outs, lses, wo, lng, lnb, wr, br):
    bsz, s, d = x.shape
    tok = lambda w: pl.BlockSpec((None, ROW_T, w), lambda b, i: (b, i, 0))
    const = lambda b, i: (0, 0)
    return pl.pallas_call(
        _attn_out_kernel,
        grid=(bsz, s // ROW_T),
        in_specs=[tok(d), pl.BlockSpec((None, 6, d), lambda b, i: (b, 0, 0))]
                 + [_class_spec(dl, B_WIDTH) for dl in B_DILATIONS]
                 + [_class_spec(dl, LANES) for dl in B_DILATIONS]
                 + [pl.BlockSpec((B_WIDTH, d), const), pl.BlockSpec((1, d), const), pl.BlockSpec((1, d), const),
                    pl.BlockSpec((2, d, LANES), lambda b, i: (0, 0, 0)), pl.BlockSpec((1, LANES), const)],
        out_specs=[tok(d), tok(d), tok(LANES)],
        out_shape=[jax.ShapeDtypeStruct((bsz, s, d), F32),
                   jax.ShapeDtypeStruct((bsz, s, d), BF16),
                   jax.ShapeDtypeStruct((bsz, s, LANES), F32)],
        scratch_shapes=[pltpu.VMEM((B_WIDTH // LANES, ROW_T, LANES), F32)],
        compiler_params=_cparams(("parallel", "parallel")),
        name="attn_out_layer",
    )(x, mod_l, *outs, *lses, wo, lng, lnb, wr, br)


def _router_weights(w_r1, b_r1, w_r2, b_r2):
    d = w_r1.shape[0]
    n = MOE_GROUPS + MOE_EXPERTS
    wr = jnp.zeros((d, LANES), F32).at[:, :MOE_GROUPS].set(w_r1).at[:, MOE_GROUPS:n].set(w_r2)
    br = jnp.zeros((1, LANES), F32).at[0, :MOE_GROUPS].set(b_r1).at[0, MOE_GROUPS:n].set(b_r2)
    wr_hi = wr.astype(BF16)
    wr_lo = (wr - wr_hi.astype(F32)).astype(BF16)
    return jnp.stack([wr_hi, wr_lo]), br


def kernel(x, c, positions, ada_w, ada_b, ln_g, ln_b, a_w_in, a_b_gate, a_norm_g, a_w_out, b_w_kv, b_w_q, b_w_o,
           moe_w_r1, moe_b_r1, moe_w_r2, moe_b_r2, moe_w_gate, moe_w_up, moe_w_down):
    bsz, s, d = x.shape
    assert d == D_MODEL and s % (B_WB * B_DILATIONS[-1]) == 0 and s % ROW_T == 0 and s % A_STEP == 0 and A_STEP % A_CHUNK == 0
    assert (bsz * s) % SORT_T == 0 and s % SORT_T == 0
    assert all(w // dl == B_WB for w, dl in zip(B_WINDOWS, B_DILATIONS))
    assert 2 * SORT_T + MOE_EXPERTS * (SEG_Q - 1) <= SORT_R and SORT_R % SEG_Q == 0
    mod = _adaln_mod(c, ada_w, ada_b)
    cos_t, sin_t = _rope_tables(positions)
    kv = None
    for l in range(DEPTH):
        lng1, lnb1 = ln_g[l, 0].reshape(1, d), ln_b[l, 0].reshape(1, d)
        lng2, lnb2 = ln_g[l, 1].reshape(1, d), ln_b[l, 1].reshape(1, d)
        wr, br = _router_weights(moe_w_r1[l], moe_b_r1[l], moe_w_r2[l], moe_b_r2[l])
        if l < N_A:
            n_main = 2 * A_QK + 2 * A_V
            win = jnp.zeros((d, n_main + LANES), BF16)
            win = win.at[:, :n_main + 2 * A_HEADS].set(a_w_in[l].astype(BF16))
            bg = jnp.zeros((1, LANES), F32).at[0, :2 * A_HEADS].set(a_b_gate[l])
            x1, h2, route = _mlstm_layer(x, mod[l], win, bg, a_norm_g[l].reshape(1, A_V),
                                         a_w_out[l].astype(BF16), lng1, lnb1, wr, br)
        else:
            lb = l - N_A
            qs = _q_project(x, mod[l], b_w_q[lb].astype(BF16), cos_t, sin_t)
            outs, lses = [], []
            for g in range(B_GROUPS):
                nsub = min(ATT_J[g], s // (B_WB * B_DILATIONS[g]))
                o_g, l_g = _dilated_attention(qs[g], kv[g], kv[B_GROUPS + g], nsub)
                outs.append(o_g)
                lses.append(l_g)
            x1, h2, route = _attn_out_layer(x, mod[l], outs, lses, b_w_o[lb].astype(BF16), lng1, lnb1, wr, br)
        x = _moe_and_norm(x1, h2, route, mod[l], lng2, lnb2, moe_w_gate, moe_w_up, moe_w_down, l)
        if l == N_A - 1:
            kv = _kv_project(x, b_w_kv.astype(BF16), cos_t, sin_t)
    return x
```

```python
import functools

import jax
import jax.numpy as jnp
from jax import lax
from jax.experimental import pallas as pl
from jax.experimental.pallas import tpu as pltpu

F32 = jnp.float32
BF16 = jnp.bfloat16
I32 = jnp.int32
HI = lax.Precision.HIGHEST

D_MODEL = 1024
DEPTH = 4
N_A = DEPTH // 2
ALPHA = (2.0 * DEPTH) ** 0.25
LN_EPS = 1e-5
A_HEADS = 4
A_DQK = D_MODEL // 8
A_DV = D_MODEL // 4
A_NORM_EPS = 1e-6
A_QK = A_HEADS * A_DQK
A_V = A_HEADS * A_DV
B_WINDOWS = (128, 512, 2048)
B_DILATIONS = (1, 4, 16)
B_GROUPS = 3
B_HEADS = 8
B_HEAD_DIM = 64
B_WIDTH = B_HEADS * B_HEAD_DIM
B_WB = 128
ROPE_THETA = 10000.0
MOE_GROUPS = 4
MOE_EPG = 8
MOE_EXPERTS = MOE_GROUPS * MOE_EPG
MOE_HIDDEN = D_MODEL // 4

LANES = 128
SUBLANES = 8
VMEM_LIMIT = 56 * 1024 * 1024

A_CHUNK = 256
A_STEP = 512
SORT_T = 512
SEG_Q = 16
SORT_R = 1536
PERM_BLK = 256
EXP_BM = 512
EXP_CPB = EXP_BM // SEG_Q
LIST_UNROLL = 4
ROW_T = 1024
ATT_J = (16, 16, 4)
MOD_TN = 1536


def _cparams(sem):
    return pltpu.CompilerParams(dimension_semantics=sem, vmem_limit_bytes=VMEM_LIMIT)


def _dot(a, b):
    return jnp.dot(a, b, preferred_element_type=F32)


def _dot_nt(a, b):
    return lax.dot_general(a, b, (((1,), (1,)), ((), ())), preferred_element_type=F32)


def _dot_tn(a, b):
    return lax.dot_general(a, b, (((0,), (0,)), ((), ())), preferred_element_type=F32)


def _layer_norm(v, g, b):
    mu = jnp.mean(v, axis=-1, keepdims=True)
    vc = v - mu
    var = jnp.mean(vc * vc, axis=-1, keepdims=True)
    return vc * lax.rsqrt(var + LN_EPS) * g + b


def _sigmoid(v):
    return 1.0 / (1.0 + jnp.exp(-v))


def _split_hi_lo(v):
    hi = v.astype(BF16)
    return hi, (v - hi.astype(F32)).astype(BF16)


def _route_tile(h2, wr_ref, br_ref):
    h_hi, h_lo = _split_hi_lo(h2)
    lg = _dot(h_hi, wr_ref[0]) + (_dot(h_hi, wr_ref[1]) + _dot(h_lo, wr_ref[0])) + br_ref[...]
    lane = lax.broadcasted_iota(I32, lg.shape, 1).astype(F32)
    neg = -jnp.inf
    big = 1000.0
    m1 = jnp.where(lane < MOE_GROUPS, lg, neg)
    mx = jnp.max(m1, axis=-1, keepdims=True)
    pg = 1.0 / jnp.sum(jnp.exp(m1 - mx), axis=-1, keepdims=True)
    gi = jnp.min(jnp.where(m1 == mx, lane, big), axis=-1, keepdims=True)
    lo = MOE_GROUPS + gi * MOE_EPG
    m2 = jnp.where((lane >= lo) & (lane < lo + MOE_EPG), lg, neg)
    v0 = jnp.max(m2, axis=-1, keepdims=True)
    j0 = jnp.min(jnp.where(m2 == v0, lane, big), axis=-1, keepdims=True)
    m3 = jnp.where(lane == j0, neg, m2)
    v1 = jnp.max(m3, axis=-1, keepdims=True)
    j1 = jnp.min(jnp.where(m3 == v1, lane, big), axis=-1, keepdims=True)
    t = jnp.exp(v1 - v0)
    wa = 1.0 / (1.0 + t)
    wb = t * wa
    out = jnp.where(lane == 0, j0 - MOE_GROUPS,
          jnp.where(lane == 1, j1 - MOE_GROUPS,
          jnp.where(lane == 2, pg * wa,
          jnp.where(lane == 3, pg * wb, 0.0))))
    return out


def _mod_kernel(c_ref, w_ref, b_ref, o_ref):
    c = c_ref[...]
    c_hi, c_lo = _split_hi_lo(c * _sigmoid(c))
    w_hi, w_lo = _split_hi_lo(w_ref[0])
    o_ref[0] = _dot(c_hi, w_hi) + (_dot(c_hi, w_lo) + _dot(c_lo, w_hi)) + b_ref[0]


def _adaln_mod(c, ada_w, ada_b):
    bsz, d = c.shape
    depth, _, n6 = ada_w.shape
    rows = SUBLANES
    c_pad = jnp.zeros((rows, d), F32).at[:bsz].set(c)
    out = pl.pallas_call(
        _mod_kernel,
        grid=(depth, n6 // MOD_TN),
        in_specs=[pl.BlockSpec((rows, d), lambda l, j: (0, 0)),
                  pl.BlockSpec((1, d, MOD_TN), lambda l, j: (l, 0, j)),
                  pl.BlockSpec((1, 1, MOD_TN), lambda l, j: (l, 0, j))],
        out_specs=pl.BlockSpec((1, rows, MOD_TN), lambda l, j: (l, 0, j)),
        out_shape=jax.ShapeDtypeStruct((depth, rows, n6), F32),
        compiler_params=_cparams(("parallel", "parallel")),
        name="adaln_mod",
    )(c_pad, ada_w, ada_b.reshape(depth, 1, n6))
    return out[:, :bsz].reshape(depth, bsz, 6, d)


def _mlstm_layer_kernel(x_ref, mod_ref, win_ref, bg_ref, ng_ref, wout_ref, lng_ref, lnb_ref,
                        wr_ref, br_ref, x1_ref, h2_ref, route_ref, c_ref, n_ref, m_ref):
    L = A_CHUNK
    n_chunks = A_STEP // L

    @pl.when(pl.program_id(1) == 0)
    def _():
        c_ref[...] = jnp.zeros_like(c_ref)
        n_ref[...] = jnp.zeros_like(n_ref)
        m_ref[...] = jnp.zeros_like(m_ref)

    md = mod_ref[...]
    sh1, sc1, g1, sh2, sc2, g2 = [md[j:j + 1, :] for j in range(6)]
    o_q, o_k, o_v, o_o, o_g = 0, A_QK, 2 * A_QK, 2 * A_QK + A_V, 2 * A_QK + 2 * A_V

    row = lax.broadcasted_iota(I32, (L, L), 0)
    col = lax.broadcasted_iota(I32, (L, L), 1)
    causal = row >= col
    tri = jnp.where(causal, 1.0, 0.0).astype(BF16)

    def proj_parts(ci):
        rows = slice(ci * L, (ci + 1) * L)
        out = {}

        def start():
            out["hb"] = (x_ref[rows, :] * (1.0 + sc1) + sh1).astype(BF16)
            out["q"] = _dot(out["hb"], win_ref[:, o_q:o_k])

        def kpart():
            out["k"] = _dot(out["hb"], win_ref[:, o_k:o_v]) * (A_DQK ** -0.5)

        def vpart():
            out["v"] = _dot(out["hb"], win_ref[:, o_v:o_o])

        def opart():
            out["og"] = _dot(out["hb"], win_ref[:, o_o:o_g])
            out["gates"] = _dot(out["hb"], win_ref[:, o_g:o_g + LANES]) + bg_ref[...]

        return out, [start, kpart, vpart, opart]

    def gate_terms(pr):
        gates = pr["gates"]
        lf = jnp.minimum(gates, 0.0) - jnp.log(1.0 + jnp.exp(-jnp.abs(gates)))
        lf = pltpu.roll(lf, LANES - A_HEADS, 1)
        lf_hi, lf_lo = _split_hi_lo(lf)
        lf_lo2 = (lf - lf_hi.astype(F32) - lf_lo.astype(F32)).astype(BF16)
        b_col = _dot(tri, lf_hi) + (_dot(tri, lf_lo) + _dot(tri, lf_lo2))
        g_col = gates - b_col
        pm = g_col
        sh = 1
        while sh < L:
            pm = jnp.maximum(pm, jnp.concatenate([jnp.full((sh, LANES), -jnp.inf, F32), pm[:L - sh]], axis=0))
            sh *= 2
        return b_col, g_col, pm, g_col.T

    def head(pr, gt, h):
        b_col, g_col, pm, g_t = gt
        qh = pr["q"][:, h * A_DQK:(h + 1) * A_DQK]
        kh = pr["k"][:, h * A_DQK:(h + 1) * A_DQK]
        vh = pr["v"][:, h * A_DV:(h + 1) * A_DV]
        qb, kb, vb = qh.astype(BF16), kh.astype(BF16), vh.astype(BF16)
        b_c = b_col[:, h:h + 1]
        g_c = g_col[:, h:h + 1]
        g_r = g_t[h:h + 1, :]
        m_st = m_ref[h:h + 1, 0:1]
        c_st = c_ref[h]
        n_st = n_ref[h:h + 1, :]

        a_c = jnp.maximum(m_st, pm[:, h:h + 1])
        dw = jnp.exp(jnp.where(causal, g_r - a_c, -jnp.inf))
        iw = jnp.exp(m_st - a_c)
        sc = _dot_nt(qb, kb) * dw
        num = _dot(sc.astype(BF16), vb) + iw * _dot(qb, c_st.astype(BF16))
        den = jnp.sum(sc, axis=-1, keepdims=True) + iw * jnp.sum(qh * n_st, axis=-1, keepdims=True)
        hh = num * (1.0 / jnp.maximum(jnp.abs(den), jnp.exp(-(b_c + a_c))))

        a_last = a_c[L - 1:L, :]
        ws = jnp.exp(g_c - a_last)
        decay = jnp.exp(m_st - a_last)
        kw = kh * ws
        c_ref[h] = decay * c_st + _dot_tn(kw.astype(BF16), vb)
        n_ref[h:h + 1, :] = decay * n_st + jnp.sum(kw, axis=0, keepdims=True)
        m_ref[h:h + 1, :] = jnp.broadcast_to(b_c[L - 1:L, :] + a_last, (1, LANES))

        mu = jnp.mean(hh, axis=-1, keepdims=True)
        hc = hh - mu
        var = jnp.mean(hc * hc, axis=-1, keepdims=True)
        hn = hc * lax.rsqrt(var + A_NORM_EPS)
        og = pr["og"][:, h * A_DV:(h + 1) * A_DV]
        return (hn * ng_ref[:, h * A_DV:(h + 1) * A_DV] * _sigmoid(og)).astype(BF16)

    def tail_parts(ci, ys):
        rows = slice(ci * L, (ci + 1) * L)
        st = {}

        def outproj():
            st["y"] = _dot(jnp.concatenate(ys, axis=1), wout_ref[...])

        def norm():
            x1 = _layer_norm(ALPHA * x_ref[rows, :] + (1.0 + g1) * st["y"], lng_ref[...], lnb_ref[...])
            x1_ref[rows, :] = x1
            st["h2"] = x1 * (1.0 + sc2) + sh2
            h2_ref[rows, :] = st["h2"].astype(h2_ref.dtype)

        def router():
            route_ref[rows, :] = _route_tile(st["h2"], wr_ref, br_ref)

        return [outproj, norm, router]

    pr, parts = proj_parts(0)
    for part in parts:
        part()
    pending_tail = []
    for ci in range(n_chunks):
        nxt, nxt_parts = proj_parts(ci + 1) if ci + 1 < n_chunks else (None, [])
        fill = nxt_parts + pending_tail
        gt = gate_terms(pr)
        ys = []
        for h in range(A_HEADS):
            ys.append(head(pr, gt, h))
            share = -(-len(fill) // (A_HEADS - h))
            for part in fill[:share]:
                part()
            fill = fill[share:]
        pending_tail = tail_parts(ci, ys)
        pr = nxt
    for part in pending_tail:
        part()


def _mlstm_layer(x, mod_l, win, bg, ng, wout, lng, lnb, wr, br):
    bsz, s, d = x.shape
    nw = win.shape[1]
    const = lambda b, i: (0, 0)
    tok = pl.BlockSpec((None, A_STEP, d), lambda b, i: (b, i, 0))
    return pl.pallas_call(
        _mlstm_layer_kernel,
        grid=(bsz, s // A_STEP),
        in_specs=[tok,
                  pl.BlockSpec((None, 6, d), lambda b, i: (b, 0, 0)),
                  pl.BlockSpec((d, nw), const),
                  pl.BlockSpec((1, LANES), const),
                  pl.BlockSpec((1, A_V), const),
                  pl.BlockSpec((A_V, d), const),
                  pl.BlockSpec((1, d), const),
                  pl.BlockSpec((1, d), const),
                  pl.BlockSpec((2, d, LANES), lambda b, i: (0, 0, 0)),
                  pl.BlockSpec((1, LANES), const)],
        out_specs=[tok, tok, pl.BlockSpec((None, A_STEP, LANES), lambda b, i: (b, i, 0))],
        out_shape=[jax.ShapeDtypeStruct((bsz, s, d), F32),
                   jax.ShapeDtypeStruct((bsz, s, d), BF16),
                   jax.ShapeDtypeStruct((bsz, s, LANES), F32)],
        scratch_shapes=[pltpu.VMEM((A_HEADS, A_DQK, A_DV), F32),
                        pltpu.VMEM((SUBLANES, A_DQK), F32),
                        pltpu.VMEM((SUBLANES, LANES), F32)],
        compiler_params=_cparams(("parallel", "arbitrary")),
        name="mlstm_layer",
    )(x, mod_l, win, bg, ng, wout, lng, lnb, wr, br)


def _moe_sort_kernel(h2_ref, route_ref, xs_ref, ws_ref, pos_ref, cnt_ref):
    n, nr = SORT_T, SORT_R
    r = route_ref[...]
    lane = lax.broadcasted_iota(I32, (n, LANES), 1).astype(F32)
    oh0 = lane == r[:, 0:1]
    oh1 = lane == r[:, 1:2]
    oh = jnp.where(oh0 | oh1, 1.0, 0.0)
    row = lax.broadcasted_iota(I32, (n, n), 0)
    col = lax.broadcasted_iota(I32, (n, n), 1)
    before = jnp.where(row > col, 1.0, 0.0).astype(BF16)
    excl = _dot(before, oh.astype(BF16))
    cnt = jnp.sum(oh, axis=0, keepdims=True)
    pcnt = jnp.ceil(cnt * (1.0 / SEG_Q)) * SEG_Q
    ej = lax.broadcasted_iota(I32, (LANES, LANES), 0)
    el = lax.broadcasted_iota(I32, (LANES, LANES), 1)
    upper = jnp.where(ej < el, 1.0, 0.0).astype(F32)
    seg = jnp.dot(jnp.broadcast_to(pcnt, (SUBLANES, LANES)), upper,
                  preferred_element_type=F32, precision=HI)[0:1, :]
    base = excl + seg
    p0 = jnp.sum(jnp.where(oh0, base, 0.0), axis=-1, keepdims=True)
    p1 = jnp.sum(jnp.where(oh1, base, 0.0), axis=-1, keepdims=True)
    pos = jnp.where(lane == 0, p0, jnp.where(lane == 1, p1, 0.0))
    pos_ref[...] = pos
    cnt_ref[...] = cnt
    pos_t = pos.T

    def split3(w):
        hi = w.astype(BF16)
        r1 = w - hi.astype(F32)
        mid = r1.astype(BF16)
        lo = (r1 - mid.astype(F32)).astype(BF16)
        return jnp.where(lane == 0, hi.astype(F32),
               jnp.where(lane == 1, mid.astype(F32),
               jnp.where(lane == 2, lo.astype(F32), 0.0))).astype(BF16)

    w0, w1 = split3(r[:, 2:3]), split3(r[:, 3:4])
    h2 = h2_ref[...]
    for kb in range(nr // PERM_BLK):
        rows = slice(kb * PERM_BLK, (kb + 1) * PERM_BLK)
        ri = (lax.broadcasted_iota(I32, (PERM_BLK, n), 0) + kb * PERM_BLK).astype(F32)
        sel0 = ri == pos_t[0:1, :]
        sel1 = ri == pos_t[1:2, :]
        perm = jnp.where(sel0 | sel1, 1.0, 0.0).astype(BF16)
        xs_ref[rows, :] = _dot(perm, h2).astype(BF16)
        wsum = (_dot(jnp.where(sel0, 1.0, 0.0).astype(BF16), w0)
                + _dot(jnp.where(sel1, 1.0, 0.0).astype(BF16), w1))
        ws = wsum[:, 0:1] + wsum[:, 1:2] + wsum[:, 2:3]
        ws_ref[rows, :] = jnp.broadcast_to(ws, (PERM_BLK, LANES))


def _chunk_list_kernel(seg_ref, nch_ref, src_ref, be_ref, bc_ref, nb_ref, *, n_tiles):
    n_src = src_ref.shape[0]
    n_blk = be_ref.shape[0]

    def clear_src(k, c):
        src_ref[k] = 0
        return c

    lax.fori_loop(0, n_src, clear_src, 0, unroll=16)

    def per_expert(e, carry):
        k0, nb0 = carry

        def per_tile(i, k):
            s = seg_ref[i * MOE_EXPERTS + e]
            n = nch_ref[i * MOE_EXPERTS + e]

            row0 = i * SORT_R + s
            for c in range(LIST_UNROLL):
                @pl.when(c < n)
                def _(c=c):
                    src_ref[k + c] = row0 + c * SEG_Q

            def per_chunk(c, carry):
                src_ref[k + c] = row0 + c * SEG_Q
                return carry

            lax.fori_loop(LIST_UNROLL, n, per_chunk, 0)
            return k + n

        k1 = lax.fori_loop(0, n_tiles, per_tile, k0)
        tot = k1 - k0
        nblk = (tot + EXP_CPB - 1) // EXP_CPB

        def per_block(j, c):
            be_ref[nb0 + j] = e
            bc_ref[nb0 + j] = jnp.minimum(EXP_CPB, tot - j * EXP_CPB)
            return c

        lax.fori_loop(0, nblk, per_block, 0)
        return k0 + nblk * EXP_CPB, nb0 + nblk

    _, nb = lax.fori_loop(0, MOE_EXPERTS, per_expert, (jnp.int32(0), jnp.int32(0)))
    nb_ref[0] = nb

    def spare_block(j, c):
        be_ref[j] = MOE_EXPERTS - 1
        bc_ref[j] = 0
        return c

    lax.fori_loop(nb, n_blk, spare_block, 0)


def _chunk_copy(src_ref, src_row, dst_ref, dst_row, sem):
    return pltpu.make_async_copy(src_ref.at[pl.ds(pl.multiple_of(src_row, SEG_Q), SEG_Q), :],
                                 dst_ref.at[pl.ds(pl.multiple_of(dst_row, SEG_Q), SEG_Q), :], sem)


def _expert_kernel(src_ref, be_ref, bc_ref, nb_ref, wg_ref, wu_ref, wd_ref, xs_ref, ys_ref,
                   wgu_s, wd_s, xg, yg, sem_in, sem_out):
    b = pl.program_id(0)
    last = pl.num_programs(0) - 1
    nb = nb_ref[0]

    def clamp(blk):
        return jnp.clip(blk, 0, last)

    def count(blk):
        return jnp.where((blk >= 0) & (blk <= last), bc_ref[clamp(blk)], 0)

    def per_chunk(n, fn):
        for c in range(EXP_CPB):
            @pl.when(c < n)
            def _(c=c):
                fn(c)

    def gather(blk, slot):
        per_chunk(count(blk), lambda c: _chunk_copy(
            xs_ref, src_ref[clamp(blk) * EXP_CPB + c], xg.at[slot], c * SEG_Q, sem_in.at[slot]).start())

    def wait_gather(blk, slot):
        per_chunk(count(blk), lambda c: _chunk_copy(xs_ref, 0, xg.at[slot], 0, sem_in.at[slot]).wait())

    def scatter(blk, slot):
        per_chunk(count(blk), lambda c: _chunk_copy(
            yg.at[slot], c * SEG_Q, ys_ref, src_ref[clamp(blk) * EXP_CPB + c], sem_out.at[slot]).start())

    def wait_scatter(blk, slot):
        per_chunk(count(blk), lambda c: _chunk_copy(yg.at[slot], 0, ys_ref, 0, sem_out.at[slot]).wait())

    @pl.when(b == 0)
    def _():
        xg[...] = jnp.zeros_like(xg)
        gather(0, 0)

    @pl.when((b == 0) | (be_ref[b] != be_ref[jnp.maximum(b - 1, 0)]))
    def _():
        wgu_s[:, 0:MOE_HIDDEN] = wg_ref[...].astype(BF16)
        wgu_s[:, MOE_HIDDEN:2 * MOE_HIDDEN] = wu_ref[...].astype(BF16)
        wd_s[...] = wd_ref[...].astype(BF16)

    @pl.when(b < nb)
    def _():
        slot = b & 1
        gather(b + 1, 1 - slot)
        wait_gather(b, slot)
        gu = _dot(xg[slot], wgu_s[...])
        g = gu[:, 0:MOE_HIDDEN]
        u = gu[:, MOE_HIDDEN:2 * MOE_HIDDEN]
        y = _dot((g * _sigmoid(g) * u).astype(BF16), wd_s[...])
        wait_scatter(b - 2, slot)
        yg[slot] = y.astype(BF16)
        scatter(b, slot)

    @pl.when(b == last)
    def _():
        wait_scatter(nb - 2, nb & 1)
        wait_scatter(nb - 1, (nb - 1) & 1)


def _combine_kernel(ys_ref, ws_ref, pos_ref, x1_ref, mod_ref, lng_ref, lnb_ref, x2_ref):
    n, nr = SORT_T, SORT_R
    pos = pos_ref[...]
    y = jnp.zeros((n, D_MODEL), F32)
    for kb in range(nr // PERM_BLK):
        rows = slice(kb * PERM_BLK, (kb + 1) * PERM_BLK)
        ysw = (ys_ref[rows, :].astype(F32) * ws_ref[rows, 0:1]).astype(BF16)
        ci = (lax.broadcasted_iota(I32, (n, PERM_BLK), 1) + kb * PERM_BLK).astype(F32)
        unperm = jnp.where((ci == pos[:, 0:1]) | (ci == pos[:, 1:2]), 1.0, 0.0).astype(BF16)
        y = y + _dot(unperm, ysw)
    g2 = mod_ref[5:6, :]
    x2_ref[...] = _layer_norm(ALPHA * x1_ref[...] + (1.0 + g2) * y, lng_ref[...], lnb_ref[...])


def _moe_and_norm(x1, h2, route, mod_l, lng, lnb, wg, wu, wd, layer):
    bsz, s, d = x1.shape
    n_tok = bsz * s
    nt = n_tok // SORT_T
    n_rows = nt * SORT_R
    x1f = x1.reshape(n_tok, d)
    h2f = h2.reshape(n_tok, d)
    rt = route.reshape(n_tok, LANES)

    xs, ws, pos, cnt = pl.pallas_call(
        _moe_sort_kernel,
        grid=(nt,),
        in_specs=[pl.BlockSpec((SORT_T, d), lambda i: (i, 0)),
                  pl.BlockSpec((SORT_T, LANES), lambda i: (i, 0))],
        out_specs=[pl.BlockSpec((SORT_R, d), lambda i: (i, 0)),
                   pl.BlockSpec((SORT_R, LANES), lambda i: (i, 0)),
                   pl.BlockSpec((SORT_T, LANES), lambda i: (i, 0)),
                   pl.BlockSpec((None, 1, LANES), lambda i: (i, 0, 0))],
        out_shape=[jax.ShapeDtypeStruct((n_rows, d), BF16),
                   jax.ShapeDtypeStruct((n_rows, LANES), F32),
                   jax.ShapeDtypeStruct((n_tok, LANES), F32),
                   jax.ShapeDtypeStruct((nt, 1, LANES), F32)],
        compiler_params=_cparams(("parallel",)),
        name="moe_sort",
    )(h2f, rt)

    cnt_i = cnt[:, 0, :MOE_EXPERTS].astype(I32)
    pcnt = (cnt_i + SEG_Q - 1) // SEG_Q * SEG_Q
    seg = (jnp.cumsum(pcnt, axis=1) - pcnt).reshape(-1).astype(I32)
    nch = (pcnt // SEG_Q).reshape(-1).astype(I32)

    n_src = n_rows // SEG_Q + MOE_EXPERTS * EXP_CPB
    n_blk = n_src // EXP_CPB
    smem = pl.BlockSpec(memory_space=pltpu.SMEM)
    src, block_e, block_n, n_used = pl.pallas_call(
        functools.partial(_chunk_list_kernel, n_tiles=nt),
        in_specs=[smem, smem],
        out_specs=[smem, smem, smem, smem],
        out_shape=[jax.ShapeDtypeStruct((n_src,), I32),
                   jax.ShapeDtypeStruct((n_blk,), I32),
                   jax.ShapeDtypeStruct((n_blk,), I32),
                   jax.ShapeDtypeStruct((1,), I32)],
        name="moe_chunk_list",
    )(seg, nch)

    ys = pl.pallas_call(
        _expert_kernel,
        grid_spec=pltpu.PrefetchScalarGridSpec(
            num_scalar_prefetch=4,
            grid=(n_blk,),
            in_specs=[pl.BlockSpec((None, None, d, MOE_HIDDEN), lambda b, src, be, bc, nb: (layer, be[b], 0, 0)),
                      pl.BlockSpec((None, None, d, MOE_HIDDEN), lambda b, src, be, bc, nb: (layer, be[b], 0, 0)),
                      pl.BlockSpec((None, None, MOE_HIDDEN, d), lambda b, src, be, bc, nb: (layer, be[b], 0, 0)),
                      pl.BlockSpec(memory_space=pl.ANY)],
            out_specs=pl.BlockSpec(memory_space=pl.ANY),
            scratch_shapes=[pltpu.VMEM((d, 2 * MOE_HIDDEN), BF16),
                            pltpu.VMEM((MOE_HIDDEN, d), BF16),
                            pltpu.VMEM((2, EXP_BM, d), BF16),
                            pltpu.VMEM((2, EXP_BM, d), BF16),
                            pltpu.SemaphoreType.DMA((2,)),
                            pltpu.SemaphoreType.DMA((2,))]),
        out_shape=jax.ShapeDtypeStruct((n_rows, d), BF16),
        input_output_aliases={7: 0},
        compiler_params=_cparams(("arbitrary",)),
        name="moe_experts",
    )(src, block_e, block_n, n_used, wg, wu, wd, xs)

    per_b = s // SORT_T
    x2 = pl.pallas_call(
        _combine_kernel,
        grid=(nt,),
        in_specs=[pl.BlockSpec((SORT_R, d), lambda i: (i, 0)),
                  pl.BlockSpec((SORT_R, LANES), lambda i: (i, 0)),
                  pl.BlockSpec((SORT_T, LANES), lambda i: (i, 0)),
                  pl.BlockSpec((SORT_T, d), lambda i: (i, 0)),
                  pl.BlockSpec((None, 6, d), lambda i: (i // per_b, 0, 0)),
                  pl.BlockSpec((1, d), lambda i: (0, 0)),
                  pl.BlockSpec((1, d), lambda i: (0, 0))],
        out_specs=pl.BlockSpec((SORT_T, d), lambda i: (i, 0)),
        out_shape=jax.ShapeDtypeStruct((n_tok, d), F32),
        compiler_params=_cparams(("parallel",)),
        name="moe_combine_norm",
    )(ys, ws, pos, x1f, mod_l, lng, lnb)
    return x2.reshape(bsz, s, d)


def _rope_table_kernel(pos_ref, inv_ref, sgn_ref, cos_ref, sin_ref):
    ang = pos_ref[...] * inv_ref[...]
    cos_ref[...] = jnp.cos(ang)
    sin_ref[...] = jnp.sin(ang) * sgn_ref[...]


def _rope_tables(positions):
    bsz, s = positions.shape
    half = B_HEAD_DIM // 2
    inv = jnp.power(jnp.float32(ROPE_THETA), -jnp.arange(half, dtype=F32) * 2.0 / B_HEAD_DIM)
    inv_row = jnp.tile(inv, LANES // half).reshape(1, LANES)
    sgn_row = jnp.tile(jnp.concatenate([-jnp.ones((half,), F32), jnp.ones((half,), F32)]),
                       LANES // B_HEAD_DIM).reshape(1, LANES)
    pos = jnp.broadcast_to(positions.astype(F32)[:, :, None], (bsz, s, LANES))
    tok = pl.BlockSpec((None, ROW_T, LANES), lambda b, i: (b, i, 0))
    row = pl.BlockSpec((1, LANES), lambda b, i: (0, 0))
    return pl.pallas_call(
        _rope_table_kernel,
        grid=(bsz, s // ROW_T),
        in_specs=[tok, row, row],
        out_specs=[tok, tok],
        out_shape=[jax.ShapeDtypeStruct((bsz, s, LANES), F32)] * 2,
        compiler_params=_cparams(("parallel", "parallel")),
        name="rope_tables",
    )(pos, inv_row, sgn_row)


def _rope(t, cosf, sinf, first_half):
    n = t.shape[1]
    half = B_HEAD_DIM // 2
    fwd = pltpu.roll(t, n - half, 1)
    bwd = pltpu.roll(t, half, 1)
    return t * cosf + jnp.where(first_half, fwd, bwd) * sinf


def _rope_operands(cos_ref, sin_ref, rows):
    reps = B_WIDTH // LANES
    cosf = jnp.concatenate([cos_ref[...]] * reps, axis=1)
    sinf = jnp.concatenate([sin_ref[...]] * reps, axis=1)
    lane = lax.broadcasted_iota(I32, (rows, B_WIDTH), 1)
    first_half = (lane & (B_HEAD_DIM - 1)) < (B_HEAD_DIM // 2)
    return cosf, sinf, first_half


def _store_by_residue(val, out_ref, stage_ref, dil):
    rows, width = val.shape
    if dil == 1:
        out_ref[0] = val.astype(out_ref.dtype)
        return
    for c in range(width // LANES):
        stage_ref[c] = val[:, c * LANES:(c + 1) * LANES]
    for r in range(dil):
        for c in range(width // LANES):
            out_ref[r, :, c * LANES:(c + 1) * LANES] = (
                stage_ref[c, pl.ds(r, rows // dil, stride=dil), :].astype(out_ref.dtype))


def _load_by_residue(in_ref, stage_ref, dil):
    _, sub, width = in_ref.shape
    if dil == 1:
        return in_ref[0].astype(F32)
    for r in range(dil):
        for c in range(width // LANES):
            stage_ref[c, pl.ds(r, sub, stride=dil), :] = in_ref[r, :, c * LANES:(c + 1) * LANES].astype(F32)
    return jnp.concatenate([stage_ref[c] for c in range(width // LANES)], axis=1)


def _class_spec(dil, width):
    return pl.BlockSpec((None, dil, ROW_T // dil, width), lambda b, i: (b, 0, i, 0))


def _class_shape(bsz, s, dil, width, dtype):
    return jax.ShapeDtypeStruct((bsz, dil, s // dil, width), dtype)


def _kv_kernel(x_ref, w_ref, cos_ref, sin_ref, k0, k1, k2, v0, v1, v2, stage_ref):
    xb = x_ref[...].astype(BF16)
    cosf, sinf, first_half = _rope_operands(cos_ref, sin_ref, ROW_T)
    for g, (k_ref, v_ref) in enumerate(((k0, v0), (k1, v1), (k2, v2))):
        kg = _dot(xb, w_ref[:, g * B_WIDTH:(g + 1) * B_WIDTH])
        _store_by_residue(_rope(kg, cosf, sinf, first_half), k_ref, stage_ref, B_DILATIONS[g])
        vo = (B_GROUPS + g) * B_WIDTH
        _store_by_residue(_dot(xb, w_ref[:, vo:vo + B_WIDTH]), v_ref, stage_ref, B_DILATIONS[g])


def _kv_project(x, wkv, cos_t, sin_t):
    bsz, s, d = x.shape
    tok = lambda w: pl.BlockSpec((None, ROW_T, w), lambda b, i: (b, i, 0))
    dils = B_DILATIONS * 2
    return pl.pallas_call(
        _kv_kernel,
        grid=(bsz, s // ROW_T),
        in_specs=[tok(d), pl.BlockSpec(wkv.shape, lambda b, i: (0, 0)), tok(LANES), tok(LANES)],
        out_specs=[_class_spec(dl, B_WIDTH) for dl in dils],
        out_shape=[_class_shape(bsz, s, dl, B_WIDTH, BF16) for dl in dils],
        scratch_shapes=[pltpu.VMEM((B_WIDTH // LANES, ROW_T, LANES), F32)],
        compiler_params=_cparams(("parallel", "parallel")),
        name="kv_project",
    )(x, wkv, cos_t, sin_t)


def _q_kernel(x_ref, mod_ref, w_ref, cos_ref, sin_ref, q0, q1, q2, stage_ref):
    md = mod_ref[...]
    hb = (x_ref[...] * (1.0 + md[1:2, :]) + md[0:1, :]).astype(BF16)
    cosf, sinf, first_half = _rope_operands(cos_ref, sin_ref, ROW_T)
    for g, q_ref in enumerate((q0, q1, q2)):
        qg = _dot(hb, w_ref[:, g * B_WIDTH:(g + 1) * B_WIDTH])
        _store_by_residue(_rope(qg, cosf, sinf, first_half) * (B_HEAD_DIM ** -0.5), q_ref, stage_ref,
                          B_DILATIONS[g])


def _q_project(x, mod_l, wq, cos_t, sin_t):
    bsz, s, d = x.shape
    tok = lambda w: pl.BlockSpec((None, ROW_T, w), lambda b, i: (b, i, 0))
    return pl.pallas_call(
        _q_kernel,
        grid=(bsz, s // ROW_T),
        in_specs=[tok(d), pl.BlockSpec((None, 6, d), lambda b, i: (b, 0, 0)),
                  pl.BlockSpec(wq.shape, lambda b, i: (0, 0)), tok(LANES), tok(LANES)],
        out_specs=[_class_spec(dl, B_WIDTH) for dl in B_DILATIONS],
        out_shape=[_class_shape(bsz, s, dl, B_WIDTH, BF16) for dl in B_DILATIONS],
        scratch_shapes=[pltpu.VMEM((B_WIDTH // LANES, ROW_T, LANES), F32)],
        compiler_params=_cparams(("parallel", "parallel")),
        name="q_project",
    )(x, mod_l, wq, cos_t, sin_t)


def _attn_kernel(q_ref, kc_ref, kp_ref, vc_ref, vp_ref, o_ref, lse_ref, kw_ref, vx_ref, *, nsub):
    wb = B_WB
    npair = B_WIDTH // LANES
    first_step = pl.program_id(2) == 0
    kw_ref[0:wb, :] = kp_ref[...]
    kw_ref[wb:, :] = kc_ref[...]
    for p in range(npair):
        vx_ref[p, 0:wb, 0:LANES] = vp_ref[:, p * LANES:(p + 1) * LANES]
        vx_ref[p, wb:, 0:LANES] = vc_ref[:, p * LANES:(p + 1) * LANES]
        vx_ref[p, :, LANES:2 * LANES] = jnp.ones((vx_ref.shape[1], LANES), BF16)
    qi = lax.broadcasted_iota(I32, (wb, 2 * wb), 0)
    kj = lax.broadcasted_iota(I32, (wb, 2 * wb), 1)
    band = (kj >= qi) & (kj <= qi + wb)
    lane = lax.broadcasted_iota(I32, (wb, LANES), 1)
    low = lane < B_HEAD_DIM

    for j in range(nsub):
        r0 = j * wb
        valid = band & ((kj >= wb) | jnp.logical_not(first_step)) if j == 0 else band
        q = q_ref[pl.ds(r0, wb), :]
        kwin = kw_ref[pl.ds(r0, 2 * wb), :]
        outs = []
        m_all = jnp.zeros((wb, LANES), F32)
        l_all = jnp.ones((wb, LANES), F32)
        for p in range(npair):
            qp = q[:, p * LANES:(p + 1) * LANES]
            kpair = kwin[:, p * LANES:(p + 1) * LANES]
            vx = vx_ref[p, pl.ds(r0, 2 * wb), :]
            acc = lsum = None
            for hh in range(2):
                mine = low if hh == 0 else jnp.logical_not(low)
                qm = jnp.where(mine, qp, jnp.zeros_like(qp))
                sc = jnp.where(valid, _dot_nt(qm, kpair), -jnp.inf)
                m = jnp.max(sc, axis=-1, keepdims=True)
                pv = _dot(jnp.exp(sc - m).astype(BF16), vx)
                o_h, l_h = pv[:, 0:LANES], pv[:, LANES:2 * LANES]
                m_all = jnp.where(lane == 2 * p + hh, m, m_all)
                l_all = jnp.where(lane == 2 * p + hh, l_h, l_all)
                acc = o_h if hh == 0 else jnp.where(low, acc, o_h)
                lsum = l_h if hh == 0 else jnp.where(low, lsum, l_h)
            outs.append(acc * (1.0 / lsum))
        o_ref[pl.ds(r0, wb), :] = jnp.concatenate(outs, axis=1).astype(BF16)
        lse_ref[pl.ds(r0, wb), :] = m_all + jnp.log(l_all)


def _dilated_attention(q, k, v, nsub):
    bsz, dil, rows, w = q.shape
    blk = B_WB * nsub
    cur = lambda width: pl.BlockSpec((None, None, blk, width), lambda b, r, n: (b, r, n, 0))
    prev = pl.BlockSpec((None, None, B_WB, w), lambda b, r, n: (b, r, jnp.maximum(n * nsub - 1, 0), 0))
    return pl.pallas_call(
        functools.partial(_attn_kernel, nsub=nsub),
        grid=(bsz, dil, rows // blk),
        in_specs=[cur(w), cur(w), prev, cur(w), prev],
        out_specs=[cur(w), cur(LANES)],
        out_shape=[jax.ShapeDtypeStruct((bsz, dil, rows, w), BF16),
                   jax.ShapeDtypeStruct((bsz, dil, rows, LANES), F32)],
        scratch_shapes=[pltpu.VMEM((blk + B_WB, w), BF16),
                        pltpu.VMEM((w // LANES, blk + B_WB, 2 * LANES), BF16)],
        compiler_params=_cparams(("parallel", "parallel", "arbitrary")),
        name=f"dilated_attention_d{dil}",
    )(q, k, k, v, v)


def _attn_out_kernel(x_ref, mod_ref, o0, o1, o2, l0, l1, l2, wo_ref, lng_ref, lnb_ref, wr_ref, br_ref,
                     x1_ref, h2_ref, route_ref, stage_ref):
    md = mod_ref[...]
    g1, sh2, sc2 = md[2:3, :], md[3:4, :], md[4:5, :]
    lses = [_load_by_residue(l_ref, stage_ref, dl) for l_ref, dl in zip((l0, l1, l2), B_DILATIONS)]
    mx = jnp.maximum(jnp.maximum(lses[0], lses[1]), lses[2])
    es = [jnp.exp(l - mx) for l in lses]
    inv = 1.0 / (es[0] + es[1] + es[2])

    def per_head(wc):
        return jnp.concatenate([jnp.broadcast_to(wc[:, h:h + 1], (ROW_T, B_HEAD_DIM)) for h in range(B_HEADS)],
                               axis=1)

    w0, w1 = per_head(es[0] * inv), per_head(es[1] * inv)
    o = jnp.zeros((ROW_T, B_WIDTH), F32)
    for o_ref, w, dl in zip((o0, o1, o2), (w0, w1, 1.0 - w0 - w1), B_DILATIONS):
        o = o + w * _load_by_residue(o_ref, stage_ref, dl)
    y = _dot(o.astype(BF16), wo_ref[...])
    x1 = _layer_norm(ALPHA * x_ref[...] + (1.0 + g1) * y, lng_ref[...], lnb_ref[...])
    x1_ref[...] = x1
    h2 = x1 * (1.0 + sc2) + sh2
    h2_ref[...] = h2.astype(h2_ref.dtype)
    route_ref[...] = _route_tile(h2, wr_ref, br_ref)


def _attn_out_layer(x, mod_l, outs, lses, wo, lng, lnb, wr, br):
    bsz, s, d = x.shape
    tok = lambda w: pl.BlockSpec((None, ROW_T, w), lambda b, i: (b, i, 0))
    const = lambda b, i: (0, 0)
    return pl.pallas_call(
        _attn_out_kernel,
        grid=(bsz, s // ROW_T),
        in_specs=[tok(d), pl.BlockSpec((None, 6, d), lambda b, i: (b, 0, 0))]
                 + [_class_spec(dl, B_WIDTH) for dl in B_DILATIONS]
                 + [_class_spec(dl, LANES) for dl in B_DILATIONS]
                 + [pl.BlockSpec((B_WIDTH, d), const), pl.BlockSpec((1, d), const), pl.BlockSpec((1, d), const),
                    pl.BlockSpec((2, d, LANES), lambda b, i: (0, 0, 0)), pl.BlockSpec((1, LANES), const)],
        out_specs=[tok(d), tok(d), tok(LANES)],
        out_shape=[jax.ShapeDtypeStruct((bsz, s, d), F32),
                   jax.ShapeDtypeStruct((bsz, s, d), BF16),
                   jax.ShapeDtypeStruct((bsz, s, LANES), F32)],
        scratch_shapes=[pltpu.VMEM((B_WIDTH // LANES, ROW_T, LANES), F32)],
        compiler_params=_cparams(("parallel", "parallel")),
        name="attn_out_layer",
    )(x, mod_l, *outs, *lses, wo, lng, lnb, wr, br)


def _router_weights(w_r1, b_r1, w_r2, b_r2):
    d = w_r1.shape[0]
    n = MOE_GROUPS + MOE_EXPERTS
    wr = jnp.zeros((d, LANES), F32).at[:, :MOE_GROUPS].set(w_r1).at[:, MOE_GROUPS:n].set(w_r2)
    br = jnp.zeros((1, LANES), F32).at[0, :MOE_GROUPS].set(b_r1).at[0, MOE_GROUPS:n].set(b_r2)
    wr_hi = wr.astype(BF16)
    wr_lo = (wr - wr_hi.astype(F32)).astype(BF16)
    return jnp.stack([wr_hi, wr_lo]), br


def kernel(x, c, positions, ada_w, ada_b, ln_g, ln_b, a_w_in, a_b_gate, a_norm_g, a_w_out, b_w_kv, b_w_q, b_w_o,
           moe_w_r1, moe_b_r1, moe_w_r2, moe_b_r2, moe_w_gate, moe_w_up, moe_w_down):
    bsz, s, d = x.shape
    assert d == D_MODEL and s % (B_WB * B_DILATIONS[-1]) == 0 and s % ROW_T == 0 and s % A_STEP == 0 and A_STEP % A_CHUNK == 0
    assert (bsz * s) % SORT_T == 0 and s % SORT_T == 0
    assert all(w // dl == B_WB for w, dl in zip(B_WINDOWS, B_DILATIONS))
    assert 2 * SORT_T + MOE_EXPERTS * (SEG_Q - 1) <= SORT_R and SORT_R % SEG_Q == 0
    mod = _adaln_mod(c, ada_w, ada_b)
    cos_t, sin_t = _rope_tables(positions)
    kv = None
    for l in range(DEPTH):
        lng1, lnb1 = ln_g[l, 0].reshape(1, d), ln_b[l, 0].reshape(1, d)
        lng2, lnb2 = ln_g[l, 1].reshape(1, d), ln_b[l, 1].reshape(1, d)
        wr, br = _router_weights(moe_w_r1[l], moe_b_r1[l], moe_w_r2[l], moe_b_r2[l])
        if l < N_A:
            n_main = 2 * A_QK + 2 * A_V
            win = jnp.zeros((d, n_main + LANES), BF16)
            win = win.at[:, :n_main + 2 * A_HEADS].set(a_w_in[l].astype(BF16))
            bg = jnp.zeros((1, LANES), F32).at[0, :2 * A_HEADS].set(a_b_gate[l])
            x1, h2, route = _mlstm_layer(x, mod[l], win, bg, a_norm_g[l].reshape(1, A_V),
                                         a_w_out[l].astype(BF16), lng1, lnb1, wr, br)
        else:
            lb = l - N_A
            qs = _q_project(x, mod[l], b_w_q[lb].astype(BF16), cos_t, sin_t)
            outs, lses = [], []
            for g in range(B_GROUPS):
                nsub = min(ATT_J[g], s // (B_WB * B_DILATIONS[g]))
                o_g, l_g = _dilated_attention(qs[g], kv[g], kv[B_GROUPS + g], nsub)
                outs.append(o_g)
                lses.append(l_g)
            x1, h2, route = _attn_out_layer(x, mod[l], outs, lses, b_w_o[lb].astype(BF16), lng1, lnb1, wr, br)
        x = _moe_and_norm(x1, h2, route, mod[l], lng2, lnb2, moe_w_gate, moe_w_up, moe_w_down, l)
        if l == N_A - 1:
            kv = _kv_project(x, b_w_kv.astype(BF16), cos_t, sin_t)
    return x
```

```python
import functools

import jax
import jax.numpy as jnp
from jax import lax
from jax.experimental import pallas as pl
from jax.experimental.pallas import tpu as pltpu

F32 = jnp.float32
BF16 = jnp.bfloat16
I32 = jnp.int32
HI = lax.Precision.HIGHEST

D_MODEL = 1024
DEPTH = 4
N_A = DEPTH // 2
ALPHA = (2.0 * DEPTH) ** 0.25
LN_EPS = 1e-5
A_HEADS = 4
A_DQK = D_MODEL // 8
A_DV = D_MODEL // 4
A_NORM_EPS = 1e-6
A_QK = A_HEADS * A_DQK
A_V = A_HEADS * A_DV
B_WINDOWS = (128, 512, 2048)
B_DILATIONS = (1, 4, 16)
B_GROUPS = 3
B_HEADS = 8
B_HEAD_DIM = 64
B_WIDTH = B_HEADS * B_HEAD_DIM
B_WB = 128
ROPE_THETA = 10000.0
MOE_GROUPS = 4
MOE_EPG = 8
MOE_EXPERTS = MOE_GROUPS * MOE_EPG
MOE_HIDDEN = D_MODEL // 4

LANES = 128
SUBLANES = 8
VMEM_LIMIT = 56 * 1024 * 1024

A_CHUNK = 256
A_STEP = 512
SORT_T = 512
SEG_Q = 16
SORT_R = 1536
PERM_BLK = 256
EXP_BM = 512
EXP_CPB = EXP_BM // SEG_Q
LIST_UNROLL = 4
ROW_T = 1024
ATT_J = (16, 16, 4)
MOD_TN = 1536


def _cparams(sem):
    return pltpu.CompilerParams(dimension_semantics=sem, vmem_limit_bytes=VMEM_LIMIT)


def _dot(a, b):
    return jnp.dot(a, b, preferred_element_type=F32)


def _dot_nt(a, b):
    return lax.dot_general(a, b, (((1,), (1,)), ((), ())), preferred_element_type=F32)


def _dot_tn(a, b):
    return lax.dot_general(a, b, (((0,), (0,)), ((), ())), preferred_element_type=F32)


def _layer_norm(v, g, b):
    mu = jnp.mean(v, axis=-1, keepdims=True)
    vc = v - mu
    var = jnp.mean(vc * vc, axis=-1, keepdims=True)
    return vc * lax.rsqrt(var + LN_EPS) * g + b


def _sigmoid(v):
    return 1.0 / (1.0 + jnp.exp(-v))


def _split_hi_lo(v):
    hi = v.astype(BF16)
    return hi, (v - hi.astype(F32)).astype(BF16)


def _route_tile(h2, wr_ref, br_ref):
    h_hi, h_lo = _split_hi_lo(h2)
    lg = _dot(h_hi, wr_ref[0]) + (_dot(h_hi, wr_ref[1]) + _dot(h_lo, wr_ref[0])) + br_ref[...]
    lane = lax.broadcasted_iota(I32, lg.shape, 1).astype(F32)
    neg = -jnp.inf
    big = 1000.0
    m1 = jnp.where(lane < MOE_GROUPS, lg, neg)
    mx = jnp.max(m1, axis=-1, keepdims=True)
    pg = 1.0 / jnp.sum(jnp.exp(m1 - mx), axis=-1, keepdims=True)
    gi = jnp.min(jnp.where(m1 == mx, lane, big), axis=-1, keepdims=True)
    lo = MOE_GROUPS + gi * MOE_EPG
    m2 = jnp.where((lane >= lo) & (lane < lo + MOE_EPG), lg, neg)
    v0 = jnp.max(m2, axis=-1, keepdims=True)
    j0 = jnp.min(jnp.where(m2 == v0, lane, big), axis=-1, keepdims=True)
    m3 = jnp.where(lane == j0, neg, m2)
    v1 = jnp.max(m3, axis=-1, keepdims=True)
    j1 = jnp.min(jnp.where(m3 == v1, lane, big), axis=-1, keepdims=True)
    t = jnp.exp(v1 - v0)
    wa = 1.0 / (1.0 + t)
    wb = t * wa
    out = jnp.where(lane == 0, j0 - MOE_GROUPS,
          jnp.where(lane == 1, j1 - MOE_GROUPS,
          jnp.where(lane == 2, pg * wa,
          jnp.where(lane == 3, pg * wb, 0.0))))
    return out


def _mod_kernel(c_ref, w_ref, b_ref, o_ref):
    c = c_ref[...]
    c_hi, c_lo = _split_hi_lo(c * _sigmoid(c))
    w_hi, w_lo = _split_hi_lo(w_ref[0])
    o_ref[0] = _dot(c_hi, w_hi) + (_dot(c_hi, w_lo) + _dot(c_lo, w_hi)) + b_ref[0]


def _adaln_mod(c, ada_w, ada_b):
    bsz, d = c.shape
    depth, _, n6 = ada_w.shape
    rows = SUBLANES
    c_pad = jnp.zeros((rows, d), F32).at[:bsz].set(c)
    out = pl.pallas_call(
        _mod_kernel,
        grid=(depth, n6 // MOD_TN),
        in_specs=[pl.BlockSpec((rows, d), lambda l, j: (0, 0)),
                  pl.BlockSpec((1, d, MOD_TN), lambda l, j: (l, 0, j)),
                  pl.BlockSpec((1, 1, MOD_TN), lambda l, j: (l, 0, j))],
        out_specs=pl.BlockSpec((1, rows, MOD_TN), lambda l, j: (l, 0, j)),
        out_shape=jax.ShapeDtypeStruct((depth, rows, n6), F32),
        compiler_params=_cparams(("parallel", "parallel")),
        name="adaln_mod",
    )(c_pad, ada_w, ada_b.reshape(depth, 1, n6))
    return out[:, :bsz].reshape(depth, bsz, 6, d)


def _mlstm_layer_kernel(x_ref, mod_ref, win_ref, bg_ref, ng_ref, wout_ref, lng_ref, lnb_ref,
                        wr_ref, br_ref, x1_ref, h2_ref, route_ref, c_ref, n_ref, m_ref):
    L = A_CHUNK
    n_chunks = A_STEP // L

    @pl.when(pl.program_id(1) == 0)
    def _():
        c_ref[...] = jnp.zeros_like(c_ref)
        n_ref[...] = jnp.zeros_like(n_ref)
        m_ref[...] = jnp.zeros_like(m_ref)

    md = mod_ref[...]
    sh1, sc1, g1, sh2, sc2, g2 = [md[j:j + 1, :] for j in range(6)]
    o_q, o_k, o_v, o_o, o_g = 0, A_QK, 2 * A_QK, 2 * A_QK + A_V, 2 * A_QK + 2 * A_V

    row = lax.broadcasted_iota(I32, (L, L), 0)
    col = lax.broadcasted_iota(I32, (L, L), 1)
    causal = row >= col
    tri = jnp.where(causal, 1.0, 0.0).astype(BF16)

    def proj_parts(ci):
        rows = slice(ci * L, (ci + 1) * L)
        out = {}

        def start():
            out["hb"] = (x_ref[rows, :] * (1.0 + sc1) + sh1).astype(BF16)
            out["q"] = _dot(out["hb"], win_ref[:, o_q:o_k])

        def kpart():
            out["k"] = _dot(out["hb"], win_ref[:, o_k:o_v]) * (A_DQK ** -0.5)

        def vpart():
            out["v"] = _dot(out["hb"], win_ref[:, o_v:o_o])

        def opart():
            out["og"] = _dot(out["hb"], win_ref[:, o_o:o_g])
            out["gates"] = _dot(out["hb"], win_ref[:, o_g:o_g + LANES]) + bg_ref[...]

        return out, [start, kpart, vpart, opart]

    def gate_terms(pr):
        gates = pr["gates"]
        lf = jnp.minimum(gates, 0.0) - jnp.log(1.0 + jnp.exp(-jnp.abs(gates)))
        lf = pltpu.roll(lf, LANES - A_HEADS, 1)
        lf_hi, lf_lo = _split_hi_lo(lf)
        lf_lo2 = (lf - lf_hi.astype(F32) - lf_lo.astype(F32)).astype(BF16)
        b_col = _dot(tri, lf_hi) + (_dot(tri, lf_lo) + _dot(tri, lf_lo2))
        g_col = gates - b_col
        pm = g_col
        sh = 1
        while sh < L:
            pm = jnp.maximum(pm, jnp.concatenate([jnp.full((sh, LANES), -jnp.inf, F32), pm[:L - sh]], axis=0))
            sh *= 2
        m_prev = m_ref[0:1, :]
        a_all = jnp.maximum(m_prev, pm)
        a_last = a_all[L - 1:L, :]
        terms = dict(a=a_all, iw=jnp.exp(m_prev - a_all), en=jnp.exp(-(b_col + a_all)),
                     ws=jnp.exp(g_col - a_last), decay=jnp.exp(m_prev - a_last), g_t=g_col.T)
        m_ref[0:1, :] = b_col[L - 1:L, :] + a_last
        return terms

    def head(pr, gt, h):
        col = lambda name: gt[name][:, h:h + 1]
        qh = pr["q"][:, h * A_DQK:(h + 1) * A_DQK]
        kh = pr["k"][:, h * A_DQK:(h + 1) * A_DQK]
        vh = pr["v"][:, h * A_DV:(h + 1) * A_DV]
        qb, kb, vb = qh.astype(BF16), kh.astype(BF16), vh.astype(BF16)
        g_r = gt["g_t"][h:h + 1, :]
        c_st = c_ref[h]
        n_st = n_ref[h:h + 1, :]

        dw = jnp.exp(jnp.where(causal, g_r - col("a"), -jnp.inf))
        iw = col("iw")
        sc = _dot_nt(qb, kb) * dw
        num = _dot(sc.astype(BF16), vb) + iw * _dot(qb, c_st.astype(BF16))
        den = jnp.sum(sc, axis=-1, keepdims=True) + iw * jnp.sum(qh * n_st, axis=-1, keepdims=True)
        hh = num * (1.0 / jnp.maximum(jnp.abs(den), col("en")))

        decay = col("decay")
        kw = kh * col("ws")
        c_ref[h] = decay * c_st + _dot_tn(kw.astype(BF16), vb)
        n_ref[h:h + 1, :] = decay * n_st + jnp.sum(kw, axis=0, keepdims=True)

        mu = jnp.mean(hh, axis=-1, keepdims=True)
        hc = hh - mu
        var = jnp.mean(hc * hc, axis=-1, keepdims=True)
        hn = hc * lax.rsqrt(var + A_NORM_EPS)
        og = pr["og"][:, h * A_DV:(h + 1) * A_DV]
        return (hn * ng_ref[:, h * A_DV:(h + 1) * A_DV] * _sigmoid(og)).astype(BF16)

    def tail_parts(ci, ys):
        rows = slice(ci * L, (ci + 1) * L)
        st = {}

        def outproj():
            st["y"] = _dot(jnp.concatenate(ys, axis=1), wout_ref[...])

        def norm():
            x1 = _layer_norm(ALPHA * x_ref[rows, :] + (1.0 + g1) * st["y"], lng_ref[...], lnb_ref[...])
            x1_ref[rows, :] = x1
            st["h2"] = x1 * (1.0 + sc2) + sh2
            h2_ref[rows, :] = st["h2"].astype(h2_ref.dtype)

        def router():
            route_ref[rows, :] = _route_tile(st["h2"], wr_ref, br_ref)

        return [outproj, norm, router]

    pr, parts = proj_parts(0)
    for part in parts:
        part()
    pending_tail = []
    for ci in range(n_chunks):
        nxt, nxt_parts = proj_parts(ci + 1) if ci + 1 < n_chunks else (None, [])
        fill = nxt_parts + pending_tail
        gt = gate_terms(pr)
        ys = []
        for h in range(A_HEADS):
            ys.append(head(pr, gt, h))
            share = -(-len(fill) // (A_HEADS - h))
            for part in fill[:share]:
                part()
            fill = fill[share:]
        pending_tail = tail_parts(ci, ys)
        pr = nxt
    for part in pending_tail:
        part()


def _mlstm_layer(x, mod_l, win, bg, ng, wout, lng, lnb, wr, br):
    bsz, s, d = x.shape
    nw = win.shape[1]
    const = lambda b, i: (0, 0)
    tok = pl.BlockSpec((None, A_STEP, d), lambda b, i: (b, i, 0))
    return pl.pallas_call(
        _mlstm_layer_kernel,
        grid=(bsz, s // A_STEP),
        in_specs=[tok,
                  pl.BlockSpec((None, 6, d), lambda b, i: (b, 0, 0)),
                  pl.BlockSpec((d, nw), const),
                  pl.BlockSpec((1, LANES), const),
                  pl.BlockSpec((1, A_V), const),
                  pl.BlockSpec((A_V, d), const),
                  pl.BlockSpec((1, d), const),
                  pl.BlockSpec((1, d), const),
                  pl.BlockSpec((2, d, LANES), lambda b, i: (0, 0, 0)),
                  pl.BlockSpec((1, LANES), const)],
        out_specs=[tok, tok, pl.BlockSpec((None, A_STEP, LANES), lambda b, i: (b, i, 0))],
        out_shape=[jax.ShapeDtypeStruct((bsz, s, d), F32),
                   jax.ShapeDtypeStruct((bsz, s, d), BF16),
                   jax.ShapeDtypeStruct((bsz, s, LANES), F32)],
        scratch_shapes=[pltpu.VMEM((A_HEADS, A_DQK, A_DV), F32),
                        pltpu.VMEM((SUBLANES, A_DQK), F32),
                        pltpu.VMEM((SUBLANES, LANES), F32)],
        compiler_params=_cparams(("parallel", "arbitrary")),
        name="mlstm_layer",
    )(x, mod_l, win, bg, ng, wout, lng, lnb, wr, br)


def _moe_sort_kernel(h2_ref, route_ref, xs_ref, ws_ref, pos_ref, cnt_ref):
    n, nr = SORT_T, SORT_R
    r = route_ref[...]
    lane = lax.broadcasted_iota(I32, (n, LANES), 1).astype(F32)
    oh0 = lane == r[:, 0:1]
    oh1 = lane == r[:, 1:2]
    oh = jnp.where(oh0 | oh1, 1.0, 0.0)
    row = lax.broadcasted_iota(I32, (n, n), 0)
    col = lax.broadcasted_iota(I32, (n, n), 1)
    before = jnp.where(row > col, 1.0, 0.0).astype(BF16)
    excl = _dot(before, oh.astype(BF16))
    cnt = jnp.sum(oh, axis=0, keepdims=True)
    pcnt = jnp.ceil(cnt * (1.0 / SEG_Q)) * SEG_Q
    ej = lax.broadcasted_iota(I32, (LANES, LANES), 0)
    el = lax.broadcasted_iota(I32, (LANES, LANES), 1)
    upper = jnp.where(ej < el, 1.0, 0.0).astype(F32)
    seg = jnp.dot(jnp.broadcast_to(pcnt, (SUBLANES, LANES)), upper,
                  preferred_element_type=F32, precision=HI)[0:1, :]
    base = excl + seg
    p0 = jnp.sum(jnp.where(oh0, base, 0.0), axis=-1, keepdims=True)
    p1 = jnp.sum(jnp.where(oh1, base, 0.0), axis=-1, keepdims=True)
    pos = jnp.where(lane == 0, p0, jnp.where(lane == 1, p1, 0.0))
    pos_ref[...] = pos
    cnt_ref[...] = cnt
    pos_t = pos.T

    def split3(w):
        hi = w.astype(BF16)
        r1 = w - hi.astype(F32)
        mid = r1.astype(BF16)
        lo = (r1 - mid.astype(F32)).astype(BF16)
        return jnp.where(lane == 0, hi.astype(F32),
               jnp.where(lane == 1, mid.astype(F32),
               jnp.where(lane == 2, lo.astype(F32), 0.0))).astype(BF16)

    w0, w1 = split3(r[:, 2:3]), split3(r[:, 3:4])
    h2 = h2_ref[...]
    for kb in range(nr // PERM_BLK):
        rows = slice(kb * PERM_BLK, (kb + 1) * PERM_BLK)
        ri = (lax.broadcasted_iota(I32, (PERM_BLK, n), 0) + kb * PERM_BLK).astype(F32)
        sel0 = ri == pos_t[0:1, :]
        sel1 = ri == pos_t[1:2, :]
        perm = jnp.where(sel0 | sel1, 1.0, 0.0).astype(BF16)
        xs_ref[rows, :] = _dot(perm, h2).astype(BF16)
        wsum = (_dot(jnp.where(sel0, 1.0, 0.0).astype(BF16), w0)
                + _dot(jnp.where(sel1, 1.0, 0.0).astype(BF16), w1))
        ws = wsum[:, 0:1] + wsum[:, 1:2] + wsum[:, 2:3]
        ws_ref[rows, :] = jnp.broadcast_to(ws, (PERM_BLK, LANES))


def _chunk_list_kernel(seg_ref, nch_ref, src_ref, be_ref, bc_ref, nb_ref, *, n_tiles):
    n_src = src_ref.shape[0]
    n_blk = be_ref.shape[0]

    def clear_src(k, c):
        src_ref[k] = 0
        return c

    lax.fori_loop(0, n_src, clear_src, 0, unroll=16)

    def per_expert(e, carry):
        k0, nb0 = carry

        def per_tile(i, k):
            s = seg_ref[i * MOE_EXPERTS + e]
            n = nch_ref[i * MOE_EXPERTS + e]

            row0 = i * SORT_R + s
            for c in range(LIST_UNROLL):
                @pl.when(c < n)
                def _(c=c):
                    src_ref[k + c] = row0 + c * SEG_Q

            def per_chunk(c, carry):
                src_ref[k + c] = row0 + c * SEG_Q
                return carry

            lax.fori_loop(LIST_UNROLL, n, per_chunk, 0)
            return k + n

        k1 = lax.fori_loop(0, n_tiles, per_tile, k0)
        tot = k1 - k0
        nblk = (tot + EXP_CPB - 1) // EXP_CPB

        def per_block(j, c):
            be_ref[nb0 + j] = e
            bc_ref[nb0 + j] = jnp.minimum(EXP_CPB, tot - j * EXP_CPB)
            return c

        lax.fori_loop(0, nblk, per_block, 0)
        return k0 + nblk * EXP_CPB, nb0 + nblk

    _, nb = lax.fori_loop(0, MOE_EXPERTS, per_expert, (jnp.int32(0), jnp.int32(0)))
    nb_ref[0] = nb

    def spare_block(j, c):
        be_ref[j] = MOE_EXPERTS - 1
        bc_ref[j] = 0
        return c

    lax.fori_loop(nb, n_blk, spare_block, 0)


def _chunk_copy(src_ref, src_row, dst_ref, dst_row, sem):
    return pltpu.make_async_copy(src_ref.at[pl.ds(pl.multiple_of(src_row, SEG_Q), SEG_Q), :],
                                 dst_ref.at[pl.ds(pl.multiple_of(dst_row, SEG_Q), SEG_Q), :], sem)


def _expert_kernel(src_ref, be_ref, bc_ref, nb_ref, wg_ref, wu_ref, wd_ref, xs_ref, ys_ref,
                   wgu_s, wd_s, xg, yg, sem_in, sem_out):
    b = pl.program_id(0)
    last = pl.num_programs(0) - 1
    nb = nb_ref[0]

    def clamp(blk):
        return jnp.clip(blk, 0, last)

    def count(blk):
        return jnp.where((blk >= 0) & (blk <= last), bc_ref[clamp(blk)], 0)

    def per_chunk(n, fn):
        for c in range(EXP_CPB):
            @pl.when(c < n)
            def _(c=c):
                fn(c)

    def gather(blk, slot):
        per_chunk(count(blk), lambda c: _chunk_copy(
            xs_ref, src_ref[clamp(blk) * EXP_CPB + c], xg.at[slot], c * SEG_Q, sem_in.at[slot]).start())

    def wait_gather(blk, slot):
        per_chunk(count(blk), lambda c: _chunk_copy(xs_ref, 0, xg.at[slot], 0, sem_in.at[slot]).wait())

    def scatter(blk, slot):
        per_chunk(count(blk), lambda c: _chunk_copy(
            yg.at[slot], c * SEG_Q, ys_ref, src_ref[clamp(blk) * EXP_CPB + c], sem_out.at[slot]).start())

    def wait_scatter(blk, slot):
        per_chunk(count(blk), lambda c: _chunk_copy(yg.at[slot], 0, ys_ref, 0, sem_out.at[slot]).wait())

    @pl.when(b == 0)
    def _():
        xg[...] = jnp.zeros_like(xg)
        gather(0, 0)

    @pl.when((b == 0) | (be_ref[b] != be_ref[jnp.maximum(b - 1, 0)]))
    def _():
        wgu_s[:, 0:MOE_HIDDEN] = wg_ref[...].astype(BF16)
        wgu_s[:, MOE_HIDDEN:2 * MOE_HIDDEN] = wu_ref[...].astype(BF16)
        wd_s[...] = wd_ref[...].astype(BF16)

    @pl.when(b < nb)
    def _():
        slot = b & 1
        gather(b + 1, 1 - slot)
        wait_gather(b, slot)
        gu = _dot(xg[slot], wgu_s[...])
        g = gu[:, 0:MOE_HIDDEN]
        u = gu[:, MOE_HIDDEN:2 * MOE_HIDDEN]
        y = _dot((g * _sigmoid(g) * u).astype(BF16), wd_s[...])
        wait_scatter(b - 2, slot)
        yg[slot] = y.astype(BF16)
        scatter(b, slot)

    @pl.when(b == last)
    def _():
        wait_scatter(nb - 2, nb & 1)
        wait_scatter(nb - 1, (nb - 1) & 1)


def _combine_kernel(ys_ref, ws_ref, pos_ref, x1_ref, mod_ref, lng_ref, lnb_ref, x2_ref):
    n, nr = SORT_T, SORT_R
    pos = pos_ref[...]
    y = jnp.zeros((n, D_MODEL), F32)
    for kb in range(nr // PERM_BLK):
        rows = slice(kb * PERM_BLK, (kb + 1) * PERM_BLK)
        ysw = (ys_ref[rows, :].astype(F32) * ws_ref[rows, 0:1]).astype(BF16)
        ci = (lax.broadcasted_iota(I32, (n, PERM_BLK), 1) + kb * PERM_BLK).astype(F32)
        unperm = jnp.where((ci == pos[:, 0:1]) | (ci == pos[:, 1:2]), 1.0, 0.0).astype(BF16)
        y = y + _dot(unperm, ysw)
    g2 = mod_ref[5:6, :]
    x2_ref[...] = _layer_norm(ALPHA * x1_ref[...] + (1.0 + g2) * y, lng_ref[...], lnb_ref[...])


def _moe_and_norm(x1, h2, route, mod_l, lng, lnb, wg, wu, wd, layer):
    bsz, s, d = x1.shape
    n_tok = bsz * s
    nt = n_tok // SORT_T
    n_rows = nt * SORT_R
    x1f = x1.reshape(n_tok, d)
    h2f = h2.reshape(n_tok, d)
    rt = route.reshape(n_tok, LANES)

    xs, ws, pos, cnt = pl.pallas_call(
        _moe_sort_kernel,
        grid=(nt,),
        in_specs=[pl.BlockSpec((SORT_T, d), lambda i: (i, 0)),
                  pl.BlockSpec((SORT_T, LANES), lambda i: (i, 0))],
        out_specs=[pl.BlockSpec((SORT_R, d), lambda i: (i, 0)),
                   pl.BlockSpec((SORT_R, LANES), lambda i: (i, 0)),
                   pl.BlockSpec((SORT_T, LANES), lambda i: (i, 0)),
                   pl.BlockSpec((None, 1, LANES), lambda i: (i, 0, 0))],
        out_shape=[jax.ShapeDtypeStruct((n_rows, d), BF16),
                   jax.ShapeDtypeStruct((n_rows, LANES), F32),
                   jax.ShapeDtypeStruct((n_tok, LANES), F32),
                   jax.ShapeDtypeStruct((nt, 1, LANES), F32)],
        compiler_params=_cparams(("parallel",)),
        name="moe_sort",
    )(h2f, rt)

    cnt_i = cnt[:, 0, :MOE_EXPERTS].astype(I32)
    pcnt = (cnt_i + SEG_Q - 1) // SEG_Q * SEG_Q
    seg = (jnp.cumsum(pcnt, axis=1) - pcnt).reshape(-1).astype(I32)
    nch = (pcnt // SEG_Q).reshape(-1).astype(I32)

    n_src = n_rows // SEG_Q + MOE_EXPERTS * EXP_CPB
    n_blk = n_src // EXP_CPB
    smem = pl.BlockSpec(memory_space=pltpu.SMEM)
    src, block_e, block_n, n_used = pl.pallas_call(
        functools.partial(_chunk_list_kernel, n_tiles=nt),
        in_specs=[smem, smem],
        out_specs=[smem, smem, smem, smem],
        out_shape=[jax.ShapeDtypeStruct((n_src,), I32),
                   jax.ShapeDtypeStruct((n_blk,), I32),
                   jax.ShapeDtypeStruct((n_blk,), I32),
                   jax.ShapeDtypeStruct((1,), I32)],
        name="moe_chunk_list",
    )(seg, nch)

    ys = pl.pallas_call(
        _expert_kernel,
        grid_spec=pltpu.PrefetchScalarGridSpec(
            num_scalar_prefetch=4,
            grid=(n_blk,),
            in_specs=[pl.BlockSpec((None, None, d, MOE_HIDDEN), lambda b, src, be, bc, nb: (layer, be[b], 0, 0)),
                      pl.BlockSpec((None, None, d, MOE_HIDDEN), lambda b, src, be, bc, nb: (layer, be[b], 0, 0)),
                      pl.BlockSpec((None, None, MOE_HIDDEN, d), lambda b, src, be, bc, nb: (layer, be[b], 0, 0)),
                      pl.BlockSpec(memory_space=pl.ANY)],
            out_specs=pl.BlockSpec(memory_space=pl.ANY),
            scratch_shapes=[pltpu.VMEM((d, 2 * MOE_HIDDEN), BF16),
                            pltpu.VMEM((MOE_HIDDEN, d), BF16),
                            pltpu.VMEM((2, EXP_BM, d), BF16),
                            pltpu.VMEM((2, EXP_BM, d), BF16),
                            pltpu.SemaphoreType.DMA((2,)),
                            pltpu.SemaphoreType.DMA((2,))]),
        out_shape=jax.ShapeDtypeStruct((n_rows, d), BF16),
        input_output_aliases={7: 0},
        compiler_params=_cparams(("arbitrary",)),
        name="moe_experts",
    )(src, block_e, block_n, n_used, wg, wu, wd, xs)

    per_b = s // SORT_T
    x2 = pl.pallas_call(
        _combine_kernel,
        grid=(nt,),
        in_specs=[pl.BlockSpec((SORT_R, d), lambda i: (i, 0)),
                  pl.BlockSpec((SORT_R, LANES), lambda i: (i, 0)),
                  pl.BlockSpec((SORT_T, LANES), lambda i: (i, 0)),
                  pl.BlockSpec((SORT_T, d), lambda i: (i, 0)),
                  pl.BlockSpec((None, 6, d), lambda i: (i // per_b, 0, 0)),
                  pl.BlockSpec((1, d), lambda i: (0, 0)),
                  pl.BlockSpec((1, d), lambda i: (0, 0))],
        out_specs=pl.BlockSpec((SORT_T, d), lambda i: (i, 0)),
        out_shape=jax.ShapeDtypeStruct((n_tok, d), F32),
        compiler_params=_cparams(("parallel",)),
        name="moe_combine_norm",
    )(ys, ws, pos, x1f, mod_l, lng, lnb)
    return x2.reshape(bsz, s, d)


def _rope_table_kernel(pos_ref, inv_ref, sgn_ref, cos_ref, sin_ref):
    ang = pos_ref[...] * inv_ref[...]
    cos_ref[...] = jnp.cos(ang)
    sin_ref[...] = jnp.sin(ang) * sgn_ref[...]


def _rope_tables(positions):
    bsz, s = positions.shape
    half = B_HEAD_DIM // 2
    inv = jnp.power(jnp.float32(ROPE_THETA), -jnp.arange(half, dtype=F32) * 2.0 / B_HEAD_DIM)
    inv_row = jnp.tile(inv, LANES // half).reshape(1, LANES)
    sgn_row = jnp.tile(jnp.concatenate([-jnp.ones((half,), F32), jnp.ones((half,), F32)]),
                       LANES // B_HEAD_DIM).reshape(1, LANES)
    pos = jnp.broadcast_to(positions.astype(F32)[:, :, None], (bsz, s, LANES))
    tok = pl.BlockSpec((None, ROW_T, LANES), lambda b, i: (b, i, 0))
    row = pl.BlockSpec((1, LANES), lambda b, i: (0, 0))
    return pl.pallas_call(
        _rope_table_kernel,
        grid=(bsz, s // ROW_T),
        in_specs=[tok, row, row],
        out_specs=[tok, tok],
        out_shape=[jax.ShapeDtypeStruct((bsz, s, LANES), F32)] * 2,
        compiler_params=_cparams(("parallel", "parallel")),
        name="rope_tables",
    )(pos, inv_row, sgn_row)


def _rope(t, cosf, sinf, first_half):
    n = t.shape[1]
    half = B_HEAD_DIM // 2
    fwd = pltpu.roll(t, n - half, 1)
    bwd = pltpu.roll(t, half, 1)
    return t * cosf + jnp.where(first_half, fwd, bwd) * sinf


def _rope_operands(cos_ref, sin_ref, rows):
    reps = B_WIDTH // LANES
    cosf = jnp.concatenate([cos_ref[...]] * reps, axis=1)
    sinf = jnp.concatenate([sin_ref[...]] * reps, axis=1)
    lane = lax.broadcasted_iota(I32, (rows, B_WIDTH), 1)
    first_half = (lane & (B_HEAD_DIM - 1)) < (B_HEAD_DIM // 2)
    return cosf, sinf, first_half


def _store_by_residue(val, out_ref, stage_ref, dil):
    rows, width = val.shape
    if dil == 1:
        out_ref[0] = val.astype(out_ref.dtype)
        return
    for c in range(width // LANES):
        stage_ref[c] = val[:, c * LANES:(c + 1) * LANES]
    for r in range(dil):
        for c in range(width // LANES):
            out_ref[r, :, c * LANES:(c + 1) * LANES] = (
                stage_ref[c, pl.ds(r, rows // dil, stride=dil), :].astype(out_ref.dtype))


def _load_by_residue(in_ref, stage_ref, dil):
    _, sub, width = in_ref.shape
    if dil == 1:
        return in_ref[0].astype(F32)
    for r in range(dil):
        for c in range(width // LANES):
            stage_ref[c, pl.ds(r, sub, stride=dil), :] = in_ref[r, :, c * LANES:(c + 1) * LANES].astype(F32)
    return jnp.concatenate([stage_ref[c] for c in range(width // LANES)], axis=1)


def _class_spec(dil, width):
    return pl.BlockSpec((None, dil, ROW_T // dil, width), lambda b, i: (b, 0, i, 0))


def _class_shape(bsz, s, dil, width, dtype):
    return jax.ShapeDtypeStruct((bsz, dil, s // dil, width), dtype)


def _kv_kernel(x_ref, w_ref, cos_ref, sin_ref, k0, k1, k2, v0, v1, v2, stage_ref):
    xb = x_ref[...].astype(BF16)
    cosf, sinf, first_half = _rope_operands(cos_ref, sin_ref, ROW_T)
    for g, (k_ref, v_ref) in enumerate(((k0, v0), (k1, v1), (k2, v2))):
        kg = _dot(xb, w_ref[:, g * B_WIDTH:(g + 1) * B_WIDTH])
        _store_by_residue(_rope(kg, cosf, sinf, first_half), k_ref, stage_ref, B_DILATIONS[g])
        vo = (B_GROUPS + g) * B_WIDTH
        _store_by_residue(_dot(xb, w_ref[:, vo:vo + B_WIDTH]), v_ref, stage_ref, B_DILATIONS[g])


def _kv_project(x, wkv, cos_t, sin_t):
    bsz, s, d = x.shape
    tok = lambda w: pl.BlockSpec((None, ROW_T, w), lambda b, i: (b, i, 0))
    dils = B_DILATIONS * 2
    return pl.pallas_call(
        _kv_kernel,
        grid=(bsz, s // ROW_T),
        in_specs=[tok(d), pl.BlockSpec(wkv.shape, lambda b, i: (0, 0)), tok(LANES), tok(LANES)],
        out_specs=[_class_spec(dl, B_WIDTH) for dl in dils],
        out_shape=[_class_shape(bsz, s, dl, B_WIDTH, BF16) for dl in dils],
        scratch_shapes=[pltpu.VMEM((B_WIDTH // LANES, ROW_T, LANES), F32)],
        compiler_params=_cparams(("parallel", "parallel")),
        name="kv_project",
    )(x, wkv, cos_t, sin_t)


def _q_kernel(x_ref, mod_ref, w_ref, cos_ref, sin_ref, q0, q1, q2, stage_ref):
    md = mod_ref[...]
    hb = (x_ref[...] * (1.0 + md[1:2, :]) + md[0:1, :]).astype(BF16)
    cosf, sinf, first_half = _rope_operands(cos_ref, sin_ref, ROW_T)
    for g, q_ref in enumerate((q0, q1, q2)):
        qg = _dot(hb, w_ref[:, g * B_WIDTH:(g + 1) * B_WIDTH])
        _store_by_residue(_rope(qg, cosf, sinf, first_half) * (B_HEAD_DIM ** -0.5), q_ref, stage_ref,
                          B_DILATIONS[g])


def _q_project(x, mod_l, wq, cos_t, sin_t):
    bsz, s, d = x.shape
    tok = lambda w: pl.BlockSpec((None, ROW_T, w), lambda b, i: (b, i, 0))
    return pl.pallas_call(
        _q_kernel,
        grid=(bsz, s // ROW_T),
        in_specs=[tok(d), pl.BlockSpec((None, 6, d), lambda b, i: (b, 0, 0)),
                  pl.BlockSpec(wq.shape, lambda b, i: (0, 0)), tok(LANES), tok(LANES)],
        out_specs=[_class_spec(dl, B_WIDTH) for dl in B_DILATIONS],
        out_shape=[_class_shape(bsz, s, dl, B_WIDTH, BF16) for dl in B_DILATIONS],
        scratch_shapes=[pltpu.VMEM((B_WIDTH // LANES, ROW_T, LANES), F32)],
        compiler_params=_cparams(("parallel", "parallel")),
        name="q_project",
    )(x, mod_l, wq, cos_t, sin_t)


def _attn_kernel(q_ref, kc_ref, kp_ref, vc_ref, vp_ref, o_ref, lse_ref, kw_ref, vx_ref, *, nsub):
    wb = B_WB
    npair = B_WIDTH // LANES
    first_step = pl.program_id(2) == 0
    kw_ref[0:wb, :] = kp_ref[...]
    kw_ref[wb:, :] = kc_ref[...]
    for p in range(npair):
        vx_ref[p, 0:wb, 0:LANES] = vp_ref[:, p * LANES:(p + 1) * LANES]
        vx_ref[p, wb:, 0:LANES] = vc_ref[:, p * LANES:(p + 1) * LANES]
        vx_ref[p, :, LANES:2 * LANES] = jnp.ones((vx_ref.shape[1], LANES), BF16)
    qi = lax.broadcasted_iota(I32, (wb, 2 * wb), 0)
    kj = lax.broadcasted_iota(I32, (wb, 2 * wb), 1)
    band = (kj >= qi) & (kj <= qi + wb)
    lane = lax.broadcasted_iota(I32, (wb, LANES), 1)
    low = lane < B_HEAD_DIM

    for j in range(nsub):
        r0 = j * wb
        valid = band & ((kj >= wb) | jnp.logical_not(first_step)) if j == 0 else band
        q = q_ref[pl.ds(r0, wb), :]
        kwin = kw_ref[pl.ds(r0, 2 * wb), :]
        outs = []
        m_all = jnp.zeros((wb, LANES), F32)
        l_all = jnp.ones((wb, LANES), F32)
        for p in range(npair):
            qp = q[:, p * LANES:(p + 1) * LANES]
            kpair = kwin[:, p * LANES:(p + 1) * LANES]
            vx = vx_ref[p, pl.ds(r0, 2 * wb), :]
            acc = lsum = None
            for hh in range(2):
                mine = low if hh == 0 else jnp.logical_not(low)
                qm = jnp.where(mine, qp, jnp.zeros_like(qp))
                sc = jnp.where(valid, _dot_nt(qm, kpair), -jnp.inf)
                m = jnp.max(sc, axis=-1, keepdims=True)
                pv = _dot(jnp.exp(sc - m).astype(BF16), vx)
                o_h, l_h = pv[:, 0:LANES], pv[:, LANES:2 * LANES]
                m_all = jnp.where(lane == 2 * p + hh, m, m_all)
                l_all = jnp.where(lane == 2 * p + hh, l_h, l_all)
                acc = o_h if hh == 0 else jnp.where(low, acc, o_h)
                lsum = l_h if hh == 0 else jnp.where(low, lsum, l_h)
            outs.append(acc * (1.0 / lsum))
        o_ref[pl.ds(r0, wb), :] = jnp.concatenate(outs, axis=1).astype(BF16)
        lse_ref[pl.ds(r0, wb), :] = m_all + jnp.log(l_all)


def _dilated_attention(q, k, v, nsub):
    bsz, dil, rows, w = q.shape
    blk = B_WB * nsub
    cur = lambda width: pl.BlockSpec((None, None, blk, width), lambda b, r, n: (b, r, n, 0))
    prev = pl.BlockSpec((None, None, B_WB, w), lambda b, r, n: (b, r, jnp.maximum(n * nsub - 1, 0), 0))
    return pl.pallas_call(
        functools.partial(_attn_kernel, nsub=nsub),
        grid=(bsz, dil, rows // blk),
        in_specs=[cur(w), cur(w), prev, cur(w), prev],
        out_specs=[cur(w), cur(LANES)],
        out_shape=[jax.ShapeDtypeStruct((bsz, dil, rows, w), BF16),
                   jax.ShapeDtypeStruct((bsz, dil, rows, LANES), F32)],
        scratch_shapes=[pltpu.VMEM((blk + B_WB, w), BF16),
                        pltpu.VMEM((w // LANES, blk + B_WB, 2 * LANES), BF16)],
        compiler_params=_cparams(("parallel", "parallel", "arbitrary")),
        name=f"dilated_attention_d{dil}",
    )(q, k, k, v, v)


def _attn_out_kernel(x_ref, mod_ref, o0, o1, o2, l0, l1, l2, wo_ref, lng_ref, lnb_ref, wr_ref, br_ref,
                     x1_ref, h2_ref, route_ref, stage_ref):
    md = mod_ref[...]
    g1, sh2, sc2 = md[2:3, :], md[3:4, :], md[4:5, :]
    lses = [_load_by_residue(l_ref, stage_ref, dl) for l_ref, dl in zip((l0, l1, l2), B_DILATIONS)]
    mx = jnp.maximum(jnp.maximum(lses[0], lses[1]), lses[2])
    es = [jnp.exp(l - mx) for l in lses]
    inv = 1.0 / (es[0] + es[1] + es[2])

    def per_head(wc):
        return jnp.concatenate([jnp.broadcast_to(wc[:, h:h + 1], (ROW_T, B_HEAD_DIM)) for h in range(B_HEADS)],
                               axis=1)

    w0, w1 = per_head(es[0] * inv), per_head(es[1] * inv)
    o = jnp.zeros((ROW_T, B_WIDTH), F32)
    for o_ref, w, dl in zip((o0, o1, o2), (w0, w1, 1.0 - w0 - w1), B_DILATIONS):
        o = o + w * _load_by_residue(o_ref, stage_ref, dl)
    y = _dot(o.astype(BF16), wo_ref[...])
    x1 = _layer_norm(ALPHA * x_ref[...] + (1.0 + g1) * y, lng_ref[...], lnb_ref[...])
    x1_ref[...] = x1
    h2 = x1 * (1.0 + sc2) + sh2
    h2_ref[...] = h2.astype(h2_ref.dtype)
    route_ref[...] = _route_tile(h2, wr_ref, br_ref)


def _attn_out_layer(x, mod_l, outs, lses, wo, lng, lnb, wr, br):
    bsz, s, d = x.shape
    tok = lambda w: pl.BlockSpec((None, ROW_T, w), lambda b, i: (b, i, 0))
    const = lambda b, i: (0, 0)
    return pl.pallas_call(
        _attn_out_kernel,
        grid=(bsz, s // ROW_T),
        in_specs=[tok(d), pl.BlockSpec((None, 6, d), lambda b, i: (b, 0, 0))]
                 + [_class_spec(dl, B_WIDTH) for dl in B_DILATIONS]
                 + [_class_spec(dl, LANES) for dl in B_DILATIONS]
                 + [pl.BlockSpec((B_WIDTH, d), const), pl.BlockSpec((1, d), const), pl.BlockSpec((1, d), const),
                    pl.BlockSpec((2, d, LANES), lambda b, i: (0, 0, 0)), pl.BlockSpec((1, LANES), const)],
        out_specs=[tok(d), tok(d), tok(LANES)],
        out_shape=[jax.ShapeDtypeStruct((bsz, s, d), F32),
                   jax.ShapeDtypeStruct((bsz, s, d), BF16),
                   jax.ShapeDtypeStruct((bsz, s, LANES), F32)],
        scratch_shapes=[pltpu.VMEM((B_WIDTH // LANES, ROW_T, LANES), F32)],
        compiler_params=_cparams(("parallel", "parallel")),
        name="attn_out_layer",
    )(x, mod_l, *outs, *lses, wo, lng, lnb, wr, br)


def _router_weights(w_r1, b_r1, w_r2, b_r2):
    d = w_r1.shape[0]
    n = MOE_GROUPS + MOE_EXPERTS
    wr = jnp.zeros((d, LANES), F32).at[:, :MOE_GROUPS].set(w_r1).at[:, MOE_GROUPS:n].set(w_r2)
    br = jnp.zeros((1, LANES), F32).at[0, :MOE_GROUPS].set(b_r1).at[0, MOE_GROUPS:n].set(b_r2)
    wr_hi = wr.astype(BF16)
    wr_lo = (wr - wr_hi.astype(F32)).astype(BF16)
    return jnp.stack([wr_hi, wr_lo]), br


def kernel(x, c, positions, ada_w, ada_b, ln_g, ln_b, a_w_in, a_b_gate, a_norm_g, a_w_out, b_w_kv, b_w_q, b_w_o,
           moe_w_r1, moe_b_r1, moe_w_r2, moe_b_r2, moe_w_gate, moe_w_up, moe_w_down):
    bsz, s, d = x.shape
    assert d == D_MODEL and s % (B_WB * B_DILATIONS[-1]) == 0 and s % ROW_T == 0 and s % A_STEP == 0 and A_STEP % A_CHUNK == 0
    assert (bsz * s) % SORT_T == 0 and s % SORT_T == 0
    assert all(w // dl == B_WB for w, dl in zip(B_WINDOWS, B_DILATIONS))
    assert 2 * SORT_T + MOE_EXPERTS * (SEG_Q - 1) <= SORT_R and SORT_R % SEG_Q == 0
    mod = _adaln_mod(c, ada_w, ada_b)
    cos_t, sin_t = _rope_tables(positions)
    kv = None
    for l in range(DEPTH):
        lng1, lnb1 = ln_g[l, 0].reshape(1, d), ln_b[l, 0].reshape(1, d)
        lng2, lnb2 = ln_g[l, 1].reshape(1, d), ln_b[l, 1].reshape(1, d)
        wr, br = _router_weights(moe_w_r1[l], moe_b_r1[l], moe_w_r2[l], moe_b_r2[l])
        if l < N_A:
            n_main = 2 * A_QK + 2 * A_V
            win = jnp.zeros((d, n_main + LANES), BF16)
            win = win.at[:, :n_main + 2 * A_HEADS].set(a_w_in[l].astype(BF16))
            bg = jnp.zeros((1, LANES), F32).at[0, :2 * A_HEADS].set(a_b_gate[l])
            x1, h2, route = _mlstm_layer(x, mod[l], win, bg, a_norm_g[l].reshape(1, A_V),
                                         a_w_out[l].astype(BF16), lng1, lnb1, wr, br)
        else:
            lb = l - N_A
            qs = _q_project(x, mod[l], b_w_q[lb].astype(BF16), cos_t, sin_t)
            outs, lses = [], []
            for g in range(B_GROUPS):
                nsub = min(ATT_J[g], s // (B_WB * B_DILATIONS[g]))
                o_g, l_g = _dilated_attention(qs[g], kv[g], kv[B_GROUPS + g], nsub)
                outs.append(o_g)
                lses.append(l_g)
            x1, h2, route = _attn_out_layer(x, mod[l], outs, lses, b_w_o[lb].astype(BF16), lng1, lnb1, wr, br)
        x = _moe_and_norm(x1, h2, route, mod[l], lng2, lnb2, moe_w_gate, moe_w_up, moe_w_down, l)
        if l == N_A - 1:
            kv = _kv_project(x, b_w_kv.astype(BF16), cos_t, sin_t)
    return x
```
